```python
import jax
import jax.numpy as jnp
from jax import lax
import numpy as np

D_MODEL = 1024
BATCH = 32
SEQ = 2048
DEPTH = 2
DEC_BATCH = 16
DEC_SEQ = 32
PAST_LEN = 4096

CHUNK = 64
Q_BLOCK = 128
D_FF = 2816
NORM_EPS = 1e-6
L2_EPS = 1e-6
GDN_HEADS = 8
GDN_DK = 64
GDN_DV = 64
CONV_W = 4
GDN_QK = GDN_HEADS * GDN_DK
GDN_VW = GDN_HEADS * GDN_DV
GDN_CONV_DIM = 2 * GDN_QK + GDN_VW
MLA_HEADS = 8
Q_RANK = 256
KV_RANK = 128
NOPE_DIM = 64
ROPE_DIM = 32
V_DIM = 64
ROPE_THETA = 10000.0
MLA_OUT = MLA_HEADS * V_DIM
D_MIX = GDN_VW + MLA_OUT
IN_COLS = GDN_CONV_DIM + GDN_VW + 2 * GDN_HEADS + Q_RANK + KV_RANK + ROPE_DIM
MASK_VALUE = -1e30

kernel_name = 'hybrid_gdn_mla_macaron_stream'


def rms_norm(x, w):
    x32 = x.astype(jnp.float32)
    y = x32 * lax.rsqrt(jnp.mean(x32 * x32, axis=-1, keepdims=True) + NORM_EPS)
    return (y * w.astype(jnp.float32)).astype(x.dtype)


def swiglu_ffn(x, w_gate, w_up, w_down):
    return (jax.nn.silu(x @ w_gate) * (x @ w_up)) @ w_down


def rotary(x, pos):
    half = ROPE_DIM // 2
    inv_freq = 1.0 / (ROPE_THETA ** (jnp.arange(half, dtype=jnp.float32) / half))
    ang = pos.astype(jnp.float32)[:, None] * inv_freq[None, :]
    ang = ang.reshape((ang.shape[0],) + (1,) * (x.ndim - 3) + (half,))
    cos, sin = jnp.cos(ang), jnp.sin(ang)
    x32 = x.astype(jnp.float32)
    x1, x2 = x32[..., :half], x32[..., half:]
    return jnp.concatenate([x1 * cos - x2 * sin, x2 * cos + x1 * sin], axis=-1).astype(x.dtype)


def causal_short_conv(x, buf, w):
    xp = jnp.concatenate([buf.astype(x.dtype), x], axis=1)
    y = lax.conv_general_dilated(xp, w[:, None, :].astype(x.dtype), window_strides=(1,), padding='VALID',
                                 dimension_numbers=('NWC', 'WIO', 'NWC'), feature_group_count=x.shape[-1])
    return jax.nn.silu(y), xp[:, -(CONV_W - 1):]


def l2_normalize(x):
    return x * lax.rsqrt(jnp.sum(x * x, axis=-1, keepdims=True) + L2_EPS)


def gated_delta_chunked(q, k, v, g, beta, s0):
    B, T, H, DK = q.shape
    C = CHUNK if T % CHUNK == 0 else T
    N = T // C

    def blk(x):
        return x.reshape(B, N, C, H, -1).transpose(1, 0, 3, 2, 4)

    q = blk(q) * (DK ** -0.5)
    k = blk(k)
    v = blk(v)
    g = jnp.cumsum(g.reshape(B, N, C, H).transpose(1, 0, 3, 2), axis=-1)
    beta = beta.reshape(B, N, C, H).transpose(1, 0, 3, 2)
    causal = jnp.tril(jnp.ones((C, C), dtype=bool))
    strict = jnp.tril(jnp.ones((C, C), dtype=bool), -1)
    diff = g[..., :, None] - g[..., None, :]
    decay = jnp.where(causal, jnp.exp(jnp.where(causal, diff, 0.0)), 0.0)
    kb = k * beta[..., None]
    lower = jnp.where(strict, jnp.einsum('nbhid,nbhjd->nbhij', kb, k) * decay, 0.0)
    a_mat = jnp.eye(C, dtype=jnp.float32) + lower
    rhs = jnp.concatenate([v * beta[..., None], kb * jnp.exp(g)[..., None]], axis=-1)
    sol = lax.linalg.triangular_solve(a_mat, rhs, left_side=True, lower=True)
    value, k_cum = sol[..., :v.shape[-1]], sol[..., v.shape[-1]:]
    attn_intra = jnp.einsum('nbhid,nbhjd->nbhij', q, k) * decay
    q_dec = q * jnp.exp(g)[..., None]
    g_last = g[..., -1]
    k_dec = k * jnp.exp(g_last[..., None] - g)[..., None]

    def step(s, xs):
        qd, ai, val, kc, kd, gl = xs
        v_new = val - jnp.einsum('bhcd,bhde->bhce', kc, s)
        o = jnp.einsum('bhcd,bhde->bhce', qd, s) + jnp.einsum('bhij,bhje->bhie', ai, v_new)
        s = s * jnp.exp(gl)[..., None, None] + jnp.einsum('bhcd,bhce->bhde', kd, v_new)
        return s, o

    s_fin, o = lax.scan(step, s0, (q_dec, attn_intra, value, k_cum, k_dec, g_last))
    o = o.transpose(1, 0, 3, 2, 4).reshape(B, T, H, -1)
    return o, s_fin


def gdn_mixer(qkv, z, a, b, conv_buf, s0, conv_w, a_log, dt_bias, norm_w):
    B, T, _ = qkv.shape
    act, new_buf = causal_short_conv(qkv, conv_buf, conv_w)
    act = act.astype(jnp.float32)
    q = l2_normalize(act[..., :GDN_QK].reshape(B, T, GDN_HEADS, GDN_DK))
    k = l2_normalize(act[..., GDN_QK:2 * GDN_QK].reshape(B, T, GDN_HEADS, GDN_DK))
    v = act[..., 2 * GDN_QK:].reshape(B, T, GDN_HEADS, GDN_DV)
    beta = jax.nn.sigmoid(b.astype(jnp.float32))
    g = -jnp.exp(a_log.astype(jnp.float32)) * jax.nn.softplus(a.astype(jnp.float32) + dt_bias.astype(jnp.float32))
    o, s_new = gated_delta_chunked(q, k, v, g, beta, s0.astype(jnp.float32))
    zf = z.astype(jnp.float32).reshape(B, T, GDN_HEADS, GDN_DV)
    o = o * lax.rsqrt(jnp.mean(o * o, axis=-1, keepdims=True) + NORM_EPS) * norm_w.astype(jnp.float32) * jax.nn.silu(zf)
    return o.reshape(B, T, GDN_VW).astype(qkv.dtype), s_new, new_buf


def chunk_causal_attention(q, k, v, q_pos, k_pos):
    B, Tq, H, Dh = q.shape
    qb = min(Q_BLOCK, Tq)
    nb = Tq // qb
    qs = q.reshape(B, nb, qb, H, Dh).swapaxes(0, 1)
    ps = q_pos.reshape(nb, qb)
    k_chunk = k_pos // CHUNK
    scale = Dh ** -0.5

    def one(args):
        qblk, pblk = args
        s = jnp.einsum('bqhd,bkhd->bhqk', qblk, k, preferred_element_type=jnp.float32) * scale
        allowed = k_chunk[None, :] <= (pblk // CHUNK)[:, None]
        p = jax.nn.softmax(jnp.where(allowed, s, MASK_VALUE), axis=-1)
        return jnp.einsum('bhqk,bkhd->bqhd', p.astype(v.dtype), v)

    o = lax.map(one, (qs, ps))
    return o.swapaxes(0, 1).reshape(B, Tq, H, -1)


def mla_mixer(cq, ckv, kpe, pos, past_ckv, past_kpe, q_norm, kv_norm, w_uq, w_ukv):
    B, T, _ = cq.shape
    q = (rms_norm(cq, q_norm) @ w_uq).reshape(B, T, MLA_HEADS, NOPE_DIM + ROPE_DIM)
    q = jnp.concatenate([q[..., :NOPE_DIM], rotary(q[..., NOPE_DIM:], pos)], axis=-1)
    c_new = rms_norm(ckv, kv_norm)
    kpe_new = rotary(kpe, pos)
    if past_ckv is None:
        c_all, kpe_all, k_pos = c_new, kpe_new, pos
    else:
        c_all = jnp.concatenate([past_ckv.astype(c_new.dtype), c_new], axis=1)
        kpe_all = jnp.concatenate([past_kpe.astype(kpe_new.dtype), kpe_new], axis=1)
        k_pos = jnp.concatenate([jnp.arange(past_ckv.shape[1], dtype=jnp.int32), pos])
    Tk = c_all.shape[1]
    kv = (c_all @ w_ukv).reshape(B, Tk, MLA_HEADS, NOPE_DIM + V_DIM)
    k = jnp.concatenate([kv[..., :NOPE_DIM], jnp.broadcast_to(kpe_all[:, :, None, :], (B, Tk, MLA_HEADS, ROPE_DIM))], axis=-1)
    o = chunk_causal_attention(q, k, kv[..., NOPE_DIM:], pos, k_pos)
    return o.reshape(B, T, MLA_OUT), c_new, kpe_new


def run_trunk(x, pos, past_ckv, past_kpe, s_gdn, s_conv, p):
    c1 = GDN_CONV_DIM
    c2 = c1 + GDN_VW
    c3 = c2 + GDN_HEADS
    c4 = c3 + GDN_HEADS
    c5 = c4 + Q_RANK
    c6 = c5 + KV_RANK
    ckv_rows, kpe_rows, gdn_states, conv_bufs = [], [], [], []
    for l in range(DEPTH):
        x = x + 0.5 * swiglu_ffn(rms_norm(x, p['norm_ffn1'][l]), p['w_ffn1_gate'][l], p['w_ffn1_up'][l], p['w_ffn1_down'][l])
        h = rms_norm(x, p['norm_mix'][l])
        qkv, z, a, b, cq, ckv, kpe = jnp.split(h @ p['w_in'][l], [c1, c2, c3, c4, c5, c6], axis=-1)
        g_out, s_new, buf_new = gdn_mixer(qkv, z, a, b, s_conv[l], s_gdn[l], p['gdn_conv_w'][l],
                                          p['gdn_a_log'][l], p['gdn_dt_bias'][l], p['gdn_norm_w'][l])
        m_out, c_new, kpe_new = mla_mixer(cq, ckv, kpe, pos,
                                          None if past_ckv is None else past_ckv[l],
                                          None if past_kpe is None else past_kpe[l],
                                          p['mla_q_norm'][l], p['mla_kv_norm'][l], p['w_uq'][l], p['w_ukv'][l])
        x = x + jnp.concatenate([g_out, m_out], axis=-1) @ p['w_out'][l]
        x = x + 0.5 * swiglu_ffn(rms_norm(x, p['norm_ffn2'][l]), p['w_ffn2_gate'][l], p['w_ffn2_up'][l], p['w_ffn2_down'][l])
        ckv_rows.append(c_new)
        kpe_rows.append(kpe_new)
        gdn_states.append(s_new.astype(x.dtype))
        conv_bufs.append(buf_new)
    y = rms_norm(x, p['norm_final'])
    return y, jnp.stack(ckv_rows), jnp.stack(kpe_rows), jnp.stack(gdn_states), jnp.stack(conv_bufs)


def setup_inputs(seed: int = 0) -> dict:
    key = jax.random.key(seed)
    ks = jax.random.split(key, 32)
    f32 = jnp.float32

    def normal(k, shape, scale):
        return jax.random.normal(k, shape, f32) * scale

    def gain(k, shape):
        return 1.0 + 0.02 * jax.random.normal(k, shape, f32)

    dt = jnp.exp(jax.random.uniform(ks[14], (DEPTH, GDN_HEADS), f32, float(np.log(1e-3)), float(np.log(1e-1))))
    return {
        'x_prompt': normal(ks[0], (BATCH, SEQ, D_MODEL), 1.0),
        'x_sample': normal(ks[1], (DEC_BATCH, DEC_SEQ, D_MODEL), 1.0),
        'cache_mla_ckv': normal(ks[2], (DEPTH, DEC_BATCH, PAST_LEN, KV_RANK), 1.0),
        'cache_mla_krope': normal(ks[3], (DEPTH, DEC_BATCH, PAST_LEN, ROPE_DIM), 1.0),
        'state_gdn': normal(ks[4], (DEPTH, DEC_BATCH, GDN_HEADS, GDN_DK, GDN_DV), GDN_DK ** -0.5),
        'state_gdn_conv': normal(ks[5], (DEPTH, DEC_BATCH, CONV_W - 1, GDN_CONV_DIM), 1.0),
        'norm_ffn1': gain(ks[6], (DEPTH, D_MODEL)),
        'w_ffn1_gate': normal(ks[7], (DEPTH, D_MODEL, D_FF), D_MODEL ** -0.5),
        'w_ffn1_up': normal(ks[8], (DEPTH, D_MODEL, D_FF), D_MODEL ** -0.5),
        'w_ffn1_down': normal(ks[9], (DEPTH, D_FF, D_MODEL), D_FF ** -0.5),
        'norm_mix': gain(ks[10], (DEPTH, D_MODEL)),
        'w_in': normal(ks[11], (DEPTH, D_MODEL, IN_COLS), D_MODEL ** -0.5),
        'gdn_conv_w': normal(ks[12], (DEPTH, CONV_W, GDN_CONV_DIM), CONV_W ** -0.5),
        'gdn_a_log': jnp.log(jax.random.uniform(ks[13], (DEPTH, GDN_HEADS), f32, 1.0, 16.0)),
        'gdn_dt_bias': dt + jnp.log(-jnp.expm1(-dt)),
        'gdn_norm_w': gain(ks[15], (DEPTH, GDN_DV)),
        'mla_q_norm': gain(ks[16], (DEPTH, Q_RANK)),
        'mla_kv_norm': gain(ks[17], (DEPTH, KV_RANK)),
        'w_uq': normal(ks[18], (DEPTH, Q_RANK, MLA_HEADS * (NOPE_DIM + ROPE_DIM)), Q_RANK ** -0.5),
        'w_ukv': normal(ks[19], (DEPTH, KV_RANK, MLA_HEADS * (NOPE_DIM + V_DIM)), KV_RANK ** -0.5),
        'w_out': normal(ks[20], (DEPTH, D_MIX, D_MODEL), D_MIX ** -0.5),
        'norm_ffn2': gain(ks[21], (DEPTH, D_MODEL)),
        'w_ffn2_gate': normal(ks[22], (DEPTH, D_MODEL, D_FF), D_MODEL ** -0.5),
        'w_ffn2_up': normal(ks[23], (DEPTH, D_MODEL, D_FF), D_MODEL ** -0.5),
        'w_ffn2_down': normal(ks[24], (DEPTH, D_FF, D_MODEL), D_FF ** -0.5),
        'norm_final': gain(ks[25], (D_MODEL,)),
    }


def reference(x_prompt, x_sample, cache_mla_ckv, cache_mla_krope, state_gdn, state_gdn_conv,
              norm_ffn1, w_ffn1_gate, w_ffn1_up, w_ffn1_down, norm_mix, w_in, gdn_conv_w, gdn_a_log,
              gdn_dt_bias, gdn_norm_w, mla_q_norm, mla_kv_norm, w_uq, w_ukv, w_out, norm_ffn2,
              w_ffn2_gate, w_ffn2_up, w_ffn2_down, norm_final):
    params = dict(norm_ffn1=norm_ffn1, w_ffn1_gate=w_ffn1_gate, w_ffn1_up=w_ffn1_up, w_ffn1_down=w_ffn1_down,
                  norm_mix=norm_mix, w_in=w_in, gdn_conv_w=gdn_conv_w, gdn_a_log=gdn_a_log,
                  gdn_dt_bias=gdn_dt_bias, gdn_norm_w=gdn_norm_w, mla_q_norm=mla_q_norm,
                  mla_kv_norm=mla_kv_norm, w_uq=w_uq, w_ukv=w_ukv, w_out=w_out, norm_ffn2=norm_ffn2,
                  w_ffn2_gate=w_ffn2_gate, w_ffn2_up=w_ffn2_up, w_ffn2_down=w_ffn2_down, norm_final=norm_final)
    b_p, t_p = x_prompt.shape[0], x_prompt.shape[1]
    zero_state = jnp.zeros((DEPTH, b_p, GDN_HEADS, GDN_DK, GDN_DV), jnp.float32)
    zero_conv = jnp.zeros((DEPTH, b_p, CONV_W - 1, GDN_CONV_DIM), x_prompt.dtype)
    y_prompt, p_ckv, p_kpe, p_gdn, p_conv = run_trunk(x_prompt, jnp.arange(t_p, dtype=jnp.int32), None, None,
                                                      zero_state, zero_conv, params)
    pos_s = cache_mla_ckv.shape[2] + jnp.arange(x_sample.shape[1], dtype=jnp.int32)
    y_sample, s_ckv, s_kpe, s_gdn, s_conv = run_trunk(x_sample, pos_s, cache_mla_ckv, cache_mla_krope,
                                                      state_gdn, state_gdn_conv, params)
    return (y_prompt, y_sample, p_ckv, p_kpe, p_gdn, p_conv, s_ckv, s_kpe, s_gdn, s_conv)
```

```python
import functools

import numpy as np
import jax
import jax.numpy as jnp
from jax import lax
from jax.experimental import pallas as pl
from jax.experimental.pallas import tpu as pltpu

F32 = jnp.float32
BF16 = jnp.bfloat16

D_MODEL = 1024
D_FF = 2816
NORM_EPS = 1e-6
L2_EPS = 1e-6
CHUNK = 64
GDN_HEADS = 8
GDN_DK = 64
GDN_DV = 64
CONV_W = 4
GDN_QK = GDN_HEADS * GDN_DK
GDN_VW = GDN_HEADS * GDN_DV
GDN_CONV_DIM = 2 * GDN_QK + GDN_VW
MLA_HEADS = 8
Q_RANK = 256
KV_RANK = 128
NOPE_DIM = 64
ROPE_DIM = 32
V_DIM = 64
ROPE_THETA = 10000.0
MLA_OUT = MLA_HEADS * V_DIM
MASK_VALUE = -1e30

LANES = 128
HEAD_PAD = 128
MAIN_COLS = GDN_CONV_DIM + GDN_VW + Q_RANK + KV_RANK
FF_CHUNK = 1408
GROUP = 4
GROUP_W = GROUP * GDN_DK
N_GROUPS = GDN_HEADS // GROUP
G_LANE0 = 32
B_LANE0 = 40
VMEM_LIMIT = 56 * 1024 * 1024
TOK_TILE = 256
KV_TILE = 512
ATT_TQ = 256
ATT_TK = 256
GDN_TB = 256


def _dot(a, b):
    return jnp.dot(a, b, preferred_element_type=F32)


def _dot_nt(a, b):
    return lax.dot_general(a, b, (((1,), (1,)), ((), ())), preferred_element_type=F32)


def _dot_tn(a, b):
    return lax.dot_general(a, b, (((0,), (0,)), ((), ())), preferred_element_type=F32)


def _split_dot(x, w01, passes, left=False):
    acc = None
    r = x
    for i in range(passes):
        part = r.astype(BF16)
        d = _dot(w01, part) if left else _dot(part, w01)
        acc = d if acc is None else acc + d
        if i + 1 < passes:
            r = r - part.astype(F32)
    return acc


def _rms(x, w):
    return x * lax.rsqrt(jnp.mean(x * x, axis=-1, keepdims=True) + NORM_EPS) * w


def _silu(x):
    return x * jax.nn.sigmoid(x)


def _ffn_residual(x, nw, wg_ref, wu_ref, wd_ref):
    xn = _rms(x, nw).astype(BF16)
    acc = None
    for c0 in range(0, D_FF, FF_CHUNK):
        g = _dot(xn, wg_ref[:, c0:c0 + FF_CHUNK])
        u = _dot(xn, wu_ref[:, c0:c0 + FF_CHUNK])
        hid = (_silu(g) * u).astype(BF16)
        d = _dot(hid, wd_ref[c0:c0 + FF_CHUNK, :])
        acc = d if acc is None else acc + d
    return x + 0.5 * acc


def _pre_kernel(x_ref, nf_ref, wg_ref, wu_ref, wd_ref, nm_ref, wmain_ref, wsm_ref, gvec_ref,
                qn_ref, kvn_ref, wq2_ref, rot_ref,
                xo_ref, qkv_ref, z_ref, q_ref, c_ref, kpe_ref, gb_ref):
    x1 = _ffn_residual(x_ref[...], nf_ref[...], wg_ref, wu_ref, wd_ref)
    xo_ref[...] = x1
    h = _rms(x1, nm_ref[...]).astype(BF16)
    main = _dot(h, wmain_ref[...])
    qkv_ref[...] = main[:, :GDN_CONV_DIM]
    z_ref[...] = main[:, GDN_CONV_DIM:GDN_CONV_DIM + GDN_VW]
    c0 = GDN_CONV_DIM + GDN_VW
    cq = main[:, c0:c0 + Q_RANK]
    ckv = main[:, c0 + Q_RANK:]
    rot = rot_ref[...]
    cosq, sinq = rot[:, :LANES], rot[:, LANES:2 * LANES]
    cosk, sink = rot[:, 2 * LANES:3 * LANES], rot[:, 3 * LANES:]
    q2 = _dot(_rms(cq, qn_ref[...]).astype(BF16), wq2_ref[...])
    nq = MLA_HEADS * HEAD_PAD
    cos8 = jnp.concatenate([cosq] * MLA_HEADS, axis=1)
    sin8 = jnp.concatenate([sinq] * MLA_HEADS, axis=1)
    q_ref[...] = (q2[:, :nq] * cos8 + q2[:, nq:] * sin8).astype(BF16)
    c_ref[...] = _rms(ckv, kvn_ref[...])
    small = _dot(h, wsm_ref[...])
    s1, s2 = small[:, :LANES], small[:, LANES:]
    kpe_ref[...] = (s1 * cosk + s2 * sink)[:, :ROPE_DIM]
    gvec = gvec_ref[...]
    xa = s1 + gvec[1:2, :]
    softplus = jnp.maximum(xa, 0.0) + jnp.log1p(jnp.exp(-jnp.abs(xa)))
    gfull = -jnp.exp(gvec[0:1, :]) * softplus
    bfull = jax.nn.sigmoid(s1)
    lane = lax.broadcasted_iota(jnp.int32, s1.shape, 1)
    is_g = (lane >= G_LANE0) & (lane < G_LANE0 + GDN_HEADS)
    is_b = (lane >= B_LANE0) & (lane < B_LANE0 + GDN_HEADS)
    gb_ref[...] = jnp.where(is_g, gfull, jnp.where(is_b, bfull, 0.0))


def _post_kernel(x_ref, g_ref, m_ref, wo_ref, nf_ref, wg_ref, wu_ref, wd_ref, nfin_ref, y_ref, *, final):
    x = x_ref[...] + _dot(g_ref[...], wo_ref[:GDN_VW, :]) + _dot(m_ref[...], wo_ref[GDN_VW:, :])
    x2 = _ffn_residual(x, nf_ref[...], wg_ref, wu_ref, wd_ref)
    y_ref[...] = _rms(x2, nfin_ref[...]) if final else x2


def _const_spec(shape):
    nd = len(shape)
    return pl.BlockSpec(shape, lambda *_: (0,) * nd, pipeline_mode=pl.Buffered(1))


def _params(sem):
    return pltpu.CompilerParams(dimension_semantics=sem, vmem_limit_bytes=VMEM_LIMIT)


def _pre_call(x, lw, rot, seq_len):
    n = x.shape[0]
    tm = TOK_TILE
    rot_blocks = rot.shape[0] // tm
    row = lambda w: pl.BlockSpec((tm, w), lambda i: (i, 0))
    consts = [lw['nf1'], lw['wg1'], lw['wu1'], lw['wd1'], lw['nmix'], lw['wmain'], lw['wsm'], lw['gvec'],
              lw['qnorm'], lw['kvnorm'], lw['wq2']]
    in_specs = [row(D_MODEL)] + [_const_spec(c.shape) for c in consts]
    in_specs.append(pl.BlockSpec((tm, 4 * LANES), lambda i: (i % rot_blocks, 0)))
    out_shape = (
        jax.ShapeDtypeStruct((n, D_MODEL), F32),
        jax.ShapeDtypeStruct((n, GDN_CONV_DIM), F32),
        jax.ShapeDtypeStruct((n, GDN_VW), F32),
        jax.ShapeDtypeStruct((n, MLA_HEADS * HEAD_PAD), BF16),
        jax.ShapeDtypeStruct((n, KV_RANK), F32),
        jax.ShapeDtypeStruct((n, ROPE_DIM), F32),
        jax.ShapeDtypeStruct((n, LANES), F32),
    )
    out_specs = (row(D_MODEL), row(GDN_CONV_DIM), row(GDN_VW), row(MLA_HEADS * HEAD_PAD), row(KV_RANK),
                 row(ROPE_DIM), row(LANES))
    return pl.pallas_call(
        _pre_kernel, grid=(n // tm,), in_specs=in_specs, out_specs=out_specs, out_shape=out_shape,
        compiler_params=_params(("parallel",)), name="ffn_proj",
    )(x, *consts, rot)


def _post_call(x, g_out, m_out, lw, nfin, final):
    n = x.shape[0]
    tm = TOK_TILE
    row = lambda w: pl.BlockSpec((tm, w), lambda i: (i, 0))
    consts = [lw['wo'], lw['nf2'], lw['wg2'], lw['wu2'], lw['wd2'], nfin]
    in_specs = [row(D_MODEL), row(GDN_VW), row(MLA_OUT)] + [_const_spec(c.shape) for c in consts]
    return pl.pallas_call(
        functools.partial(_post_kernel, final=final), grid=(n // tm,), in_specs=in_specs,
        out_specs=row(D_MODEL), out_shape=jax.ShapeDtypeStruct((n, D_MODEL), F32),
        compiler_params=_params(("parallel",)), name="mix_ffn",
    )(x, g_out, m_out, *consts)


def _kv_kernel(c_ref, kpe_ref, wk_ref, we_ref, wv_ref, k_ref, v_ref):
    c = c_ref[...].astype(BF16)
    k = _dot(c, wk_ref[...]) + _dot(kpe_ref[...].astype(BF16), we_ref[...])
    k_ref[...] = k.astype(BF16)
    v_ref[...] = _dot(c, wv_ref[...]).astype(BF16)


def _kv_call(c, kpe, lw):
    n = c.shape[0]
    tm = min(KV_TILE, n)
    row = lambda w: pl.BlockSpec((tm, w), lambda i: (i, 0))
    consts = [lw['wk'], lw['we'], lw['wv']]
    return pl.pallas_call(
        _kv_kernel, grid=(n // tm,),
        in_specs=[row(KV_RANK), row(ROPE_DIM)] + [_const_spec(w.shape) for w in consts],
        out_specs=(row(MLA_HEADS * HEAD_PAD), row(MLA_OUT)),
        out_shape=(jax.ShapeDtypeStruct((n, MLA_HEADS * HEAD_PAD), BF16),
                   jax.ShapeDtypeStruct((n, MLA_OUT), BF16)),
        compiler_params=_params(("parallel",)), name="kv_up",
    )(c, kpe, *consts)


_ATT_SCALE = float((NOPE_DIM + ROPE_DIM) ** -0.5)


def _softmax_step(carry, s, v):
    m, l, acc = carry
    m_new = jnp.maximum(m, jnp.max(s, axis=-1, keepdims=True))
    alpha = jnp.exp(m - m_new)
    p = jnp.exp(s - m_new)
    l = alpha * l + jnp.sum(p, axis=-1, keepdims=True)
    acc = alpha * acc + _dot(p.astype(BF16), v)
    return m_new, l, acc


def _merge_heads(outs):
    lane = lax.broadcasted_iota(jnp.int32, outs[0].shape, 1)
    return jnp.where(lane < V_DIM, outs[0], outs[1])


def _attn_prompt_kernel(q_ref, k_ref, v_ref, o_ref, *, tq, tk):
    qi = pl.program_id(2)
    n_sub = tq // tk
    rq = lax.broadcasted_iota(jnp.int32, (tq, tk), 0) // CHUNK
    ck = lax.broadcasted_iota(jnp.int32, (tq, tk), 1) // CHUNK
    outs = []
    for h in range(2):
        hs = slice(h * HEAD_PAD, (h + 1) * HEAD_PAD)
        q = q_ref[0, :, hs]

        def tile(j, carry, masked_sub=None):
            k0 = pl.multiple_of(j * tk, tk)
            kt = k_ref[0, pl.ds(k0, tk), hs]
            vt = v_ref[0, pl.ds(k0, tk), :]
            s = _dot_nt(q, kt) * _ATT_SCALE
            if masked_sub is not None:
                s = jnp.where(ck + masked_sub * (tk // CHUNK) <= rq, s, MASK_VALUE)
            return _softmax_step(carry, s, vt)

        init = (jnp.full((tq, 1), MASK_VALUE, F32), jnp.zeros((tq, 1), F32), jnp.zeros((tq, HEAD_PAD), F32))
        carry = lax.fori_loop(0, qi * n_sub, tile, init)
        for sub in range(n_sub):
            carry = tile(qi * n_sub + sub, carry, masked_sub=sub)
        m, l, acc = carry
        outs.append(acc / l)
    o_ref[0] = _merge_heads(outs).astype(BF16)


def _attn_prompt_call(q, k, v):
    b, t, _ = q.shape
    tq, tk = ATT_TQ, ATT_TK
    pairs = MLA_HEADS // 2
    return pl.pallas_call(
        functools.partial(_attn_prompt_kernel, tq=tq, tk=tk), grid=(b, pairs, t // tq),
        in_specs=[pl.BlockSpec((1, tq, 2 * HEAD_PAD), lambda bi, p, i: (bi, i, p)),
                  pl.BlockSpec((1, t, 2 * HEAD_PAD), lambda bi, p, i: (bi, 0, p)),
                  pl.BlockSpec((1, t, 2 * V_DIM), lambda bi, p, i: (bi, 0, p))],
        out_specs=pl.BlockSpec((1, tq, 2 * V_DIM), lambda bi, p, i: (bi, i, p)),
        out_shape=jax.ShapeDtypeStruct((b, t, MLA_OUT), BF16),
        compiler_params=_params(("parallel", "parallel", "arbitrary")), name="attn_prompt",
    )(q, k, v)


def _attn_sample_kernel(q_ref, kp_ref, vp_ref, kn_ref, vn_ref, o_ref, *, tq, past):
    pad = LANES - tq
    q_chunk = (past + lax.broadcasted_iota(jnp.int32, (tq, 1), 0)) // CHUNK
    outs = []
    for h in range(2):
        hs = slice(h * HEAD_PAD, (h + 1) * HEAD_PAD)
        q = q_ref[0, :, hs]
        s_p = _dot_nt(q, kp_ref[0, :, hs]) * _ATT_SCALE
        kc_p = lax.broadcasted_iota(jnp.int32, s_p.shape, 1) // CHUNK
        s_p = jnp.where(kc_p <= q_chunk, s_p, MASK_VALUE)
        kn = jnp.concatenate([kn_ref[0, :, hs], jnp.zeros((pad, HEAD_PAD), BF16)], axis=0)
        vn = jnp.concatenate([vn_ref[0], jnp.zeros((pad, 2 * V_DIM), BF16)], axis=0)
        s_n = _dot_nt(q, kn) * _ATT_SCALE
        col = lax.broadcasted_iota(jnp.int32, s_n.shape, 1)
        ok = (col < tq) & ((past + col) // CHUNK <= q_chunk)
        s_n = jnp.where(ok, s_n, MASK_VALUE)
        m = jnp.maximum(jnp.max(s_p, axis=-1, keepdims=True), jnp.max(s_n, axis=-1, keepdims=True))
        p_p = jnp.exp(s_p - m)
        p_n = jnp.exp(s_n - m)
        l = jnp.sum(p_p, axis=-1, keepdims=True) + jnp.sum(p_n, axis=-1, keepdims=True)
        acc = _dot(p_p.astype(BF16), vp_ref[0]) + _dot(p_n.astype(BF16), vn)
        outs.append(acc / l)
    o_ref[0] = _merge_heads(outs).astype(BF16)


def _attn_sample_call(q, k_past, v_past, k_new, v_new):
    b, tq, _ = q.shape
    past = k_past.shape[1]
    pairs = MLA_HEADS // 2
    blk = lambda rows, w: pl.BlockSpec((1, rows, w), lambda bi, p: (bi, 0, p))
    return pl.pallas_call(
        functools.partial(_attn_sample_kernel, tq=tq, past=past), grid=(b, pairs),
        in_specs=[blk(tq, 2 * HEAD_PAD), blk(past, 2 * HEAD_PAD), blk(past, 2 * V_DIM),
                  blk(tq, 2 * HEAD_PAD), blk(tq, 2 * V_DIM)],
        out_specs=blk(tq, 2 * V_DIM),
        out_shape=jax.ShapeDtypeStruct((b, tq, MLA_OUT), BF16),
        compiler_params=_params(("parallel", "parallel")), name="attn_sample",
    )(q, k_past, v_past, k_new, v_new)


def _gdn_kernel(x_ref, z_ref, gb_ref, cbuf_ref, s0_ref, cw_ref, nw_ref, gsum_ref, eg_ref, eb_ref, bdm_ref,
                o_ref, sout_ref, s_scr, tail_scr, xs_scr, *, tb, nt):
    t = pl.program_id(1)
    rows = max(tb, CHUNK)

    @pl.when(t == 0)
    def _():
        s_scr[...] = s0_ref[0]
        tail_scr[...] = cbuf_ref[0]

    xs_scr[0:8, :] = tail_scr[...]
    xs_scr[8:8 + tb, :] = x_ref[0]
    tail_scr[...] = xs_scr[tb:tb + 8, :]
    cw = cw_ref[...]
    conv = None
    for j in range(CONV_W):
        term = xs_scr[5 + j:5 + j + tb, :] * cw[j:j + 1, :]
        conv = term if conv is None else conv + term
    y = _silu(conv)
    gb = gb_ref[0]
    if rows > tb:
        y = jnp.concatenate([y, jnp.zeros((rows - tb, GDN_CONV_DIM), F32)], axis=0)
        gb = jnp.concatenate([gb, jnp.zeros((rows - tb, LANES), F32)], axis=0)
    q, k, v = y[:, :GDN_QK], y[:, GDN_QK:2 * GDN_QK], y[:, 2 * GDN_QK:]
    gsum = gsum_ref[...]
    qn = q * lax.rsqrt(_split_dot(q * q, gsum, 2) + L2_EPS) * (GDN_DK ** -0.5)
    kn = k * lax.rsqrt(_split_dot(k * k, gsum, 2) + L2_EPS)
    ge = _split_dot(gb, eg_ref[...], 3)
    be = _split_dot(gb, eb_ref[...], 3)
    bdm = bdm_ref[...]

    def bd(m):
        mb = m.astype(BF16)
        return jnp.concatenate([mb] * GROUP, axis=0) * bdm

    ri = lax.broadcasted_iota(jnp.int32, (CHUNK, GROUP_W), 0)
    ci = lax.broadcasted_iota(jnp.int32, (CHUNK, GROUP_W), 1) % CHUNK
    causal = ri >= ci
    strict = ri > ci
    eye = jnp.where(ri == ci, 1.0, 0.0).astype(F32)
    tri = jnp.where(lax.broadcasted_iota(jnp.int32, (CHUNK, CHUNK), 0)
                    >= lax.broadcasted_iota(jnp.int32, (CHUNK, CHUNK), 1), 1.0, 0.0).astype(BF16)
    bdm_f32 = bdm.astype(F32)

    o_rows = []
    for c in range(rows // CHUNK):
        rs = slice(c * CHUNK, (c + 1) * CHUNK)
        o_groups = []
        for p in range(N_GROUPS):
            ls = slice(p * GROUP_W, (p + 1) * GROUP_W)
            qc, kc, vc, gec, bec = qn[rs, ls], kn[rs, ls], v[rs, ls], ge[rs, ls], be[rs, ls]
            gcum = _split_dot(gec, tri, 3, left=True)
            grow = jnp.sum(jnp.where(ri <= ci, gec, 0.0), axis=0, keepdims=True)
            eg = jnp.exp(gcum)
            glast = gcum[CHUNK - 1:CHUNK, :]
            kb = kc * bec
            qd = qc * eg
            kd = kc * jnp.exp(glast - gcum)
            decay = jnp.where(causal, jnp.exp(jnp.where(causal, gcum - grow, 0.0)), 0.0)
            kq = _dot_nt(jnp.concatenate([kb, qc], axis=0).astype(BF16), bd(kc))
            lower = jnp.where(strict, kq[:CHUNK] * decay, 0.0)
            attn = kq[CHUNK:] * decay
            m = -lower
            inv = eye + m
            m = _dot(m.astype(BF16), bd(m))
            for _ in range(4):
                r = _dot(jnp.concatenate([m, inv], axis=0).astype(BF16), bd(m))
                m = r[:CHUNK]
                inv = inv + r[CHUNK:]
            inv = inv + _dot(inv.astype(BF16), bd(m))
            invb = inv.astype(BF16)
            value = _dot(invb, bd(vc * bec))
            kcum = _dot(invb, bd(kb * eg))
            s = s_scr[p]
            r = _dot(jnp.concatenate([kcum, qd], axis=0).astype(BF16), s.astype(BF16))
            v_new = value - r[:CHUNK]
            o_groups.append(r[CHUNK:] + _dot(attn.astype(BF16), bd(v_new)))
            upd = _dot_tn(kd.astype(BF16), v_new.astype(BF16))
            s_scr[p] = s * jnp.exp(glast) + upd * bdm_f32
        o_rows.append(jnp.concatenate(o_groups, axis=1))
    o = jnp.concatenate(o_rows, axis=0) if len(o_rows) > 1 else o_rows[0]
    o = o[:tb]
    ms = _split_dot(o * o, gsum, 2) * (1.0 / GDN_DV)
    o_ref[0] = (o * lax.rsqrt(ms + NORM_EPS) * nw_ref[...] * _silu(z_ref[0])).astype(BF16)

    @pl.when(t == nt - 1)
    def _():
        sout_ref[0] = s_scr[...]


def _gdn_call(qkv, z, gb, cbuf8, s0_bd, lw, consts):
    b, t, _ = qkv.shape
    tb = min(GDN_TB, t)
    nt = t // tb
    blk = lambda w: pl.BlockSpec((1, tb, w), lambda bi, ti: (bi, ti, 0))
    per_b = lambda shape: pl.BlockSpec((1,) + shape, lambda bi, ti: (bi,) + (0,) * len(shape))
    cs = [lw['convw'], lw['gnw'], consts['gsum'], consts['eg'], consts['eb'], consts['bdm']]
    return pl.pallas_call(
        functools.partial(_gdn_kernel, tb=tb, nt=nt), grid=(b, nt),
        in_specs=[blk(GDN_CONV_DIM), blk(GDN_VW), blk(LANES), per_b((8, GDN_CONV_DIM)),
                  per_b((N_GROUPS, GROUP_W, GROUP_W))] + [_const_spec(c.shape) for c in cs],
        out_specs=(blk(GDN_VW), per_b((N_GROUPS, GROUP_W, GROUP_W))),
        out_shape=(jax.ShapeDtypeStruct((b, t, GDN_VW), BF16),
                   jax.ShapeDtypeStruct((b, N_GROUPS, GROUP_W, GROUP_W), F32)),
        scratch_shapes=[pltpu.VMEM((N_GROUPS, GROUP_W, GROUP_W), F32), pltpu.VMEM((8, GDN_CONV_DIM), F32),
                        pltpu.VMEM((tb + 8, GDN_CONV_DIM), F32)],
        compiler_params=_params(("parallel", "arbitrary")), name="gdn",
    )(qkv, z, gb, cbuf8, s0_bd, *cs)


def _layer_weights(l, p):
    c1 = GDN_CONV_DIM
    c2 = c1 + GDN_VW
    c3 = c2 + GDN_HEADS
    c4 = c3 + GDN_HEADS
    c5 = c4 + Q_RANK
    c6 = c5 + KV_RANK
    w_in = p['w_in'][l]
    half = ROPE_DIM // 2
    kpe_w = w_in[:, c6:]
    kpe_sw = jnp.concatenate([kpe_w[:, half:], kpe_w[:, :half]], axis=1)
    zcol = lambda n: jnp.zeros((D_MODEL, n), F32)
    wsm = jnp.concatenate([kpe_w, w_in[:, c2:c3], w_in[:, c3:c4], zcol(LANES - B_LANE0 - GDN_HEADS),
                           kpe_sw, zcol(LANES - ROPE_DIM)], axis=1)
    wmain = jnp.concatenate([w_in[:, :c2], w_in[:, c4:c6]], axis=1)
    gvec = jnp.zeros((2, LANES), F32)
    gvec = gvec.at[0, G_LANE0:G_LANE0 + GDN_HEADS].set(p['gdn_a_log'][l])
    gvec = gvec.at[1, G_LANE0:G_LANE0 + GDN_HEADS].set(p['gdn_dt_bias'][l])
    wq = p['w_uq'][l].reshape(Q_RANK, MLA_HEADS, NOPE_DIM + ROPE_DIM)
    nope, rope = wq[..., :NOPE_DIM], wq[..., NOPE_DIM:]
    rope_sw = jnp.concatenate([rope[..., half:], rope[..., :half]], axis=-1)
    zq = lambda n: jnp.zeros((Q_RANK, MLA_HEADS, n), F32)
    tail = HEAD_PAD - NOPE_DIM - ROPE_DIM
    wq_a = jnp.concatenate([nope, rope, zq(tail)], axis=-1).reshape(Q_RANK, MLA_HEADS * HEAD_PAD)
    wq_b = jnp.concatenate([zq(NOPE_DIM), rope_sw, zq(tail)], axis=-1).reshape(Q_RANK, MLA_HEADS * HEAD_PAD)
    wkv = p['w_ukv'][l].reshape(KV_RANK, MLA_HEADS, NOPE_DIM + V_DIM)
    wk = jnp.concatenate([wkv[..., :NOPE_DIM], jnp.zeros((KV_RANK, MLA_HEADS, HEAD_PAD - NOPE_DIM), F32)],
                         axis=-1).reshape(KV_RANK, MLA_HEADS * HEAD_PAD)
    wv = wkv[..., NOPE_DIM:].reshape(KV_RANK, MLA_OUT)
    we = np.zeros((ROPE_DIM, MLA_HEADS, HEAD_PAD), np.float32)
    for r in range(ROPE_DIM):
        we[r, :, NOPE_DIM + r] = 1.0
    bf = lambda a: a.astype(BF16)
    r2 = lambda a: a.reshape(1, -1)
    return dict(
        nf1=r2(p['norm_ffn1'][l]), wg1=bf(p['w_ffn1_gate'][l]), wu1=bf(p['w_ffn1_up'][l]),
        wd1=bf(p['w_ffn1_down'][l]), nmix=r2(p['norm_mix'][l]), wmain=bf(wmain), wsm=bf(wsm), gvec=gvec,
        qnorm=r2(p['mla_q_norm'][l]), kvnorm=r2(p['mla_kv_norm'][l]),
        wq2=bf(jnp.concatenate([wq_a, wq_b], axis=1)), wk=bf(wk), wv=bf(wv),
        we=jnp.asarray(we.reshape(ROPE_DIM, MLA_HEADS * HEAD_PAD), BF16),
        wo=bf(p['w_out'][l]), nf2=r2(p['norm_ffn2'][l]), wg2=bf(p['w_ffn2_gate'][l]),
        wu2=bf(p['w_ffn2_up'][l]), wd2=bf(p['w_ffn2_down'][l]),
        convw=p['gdn_conv_w'][l], gnw=jnp.tile(p['gdn_norm_w'][l], GDN_HEADS).reshape(1, GDN_VW),
    )


def _gdn_consts():
    lane = np.arange(GDN_QK)
    gsum = (lane[:, None] // GDN_DK == lane[None, :] // GDN_DK).astype(np.float32)
    eg = np.zeros((LANES, GDN_QK), np.float32)
    eb = np.zeros((LANES, GDN_QK), np.float32)
    for h in range(GDN_HEADS):
        eg[G_LANE0 + h, h * GDN_DK:(h + 1) * GDN_DK] = 1.0
        eb[B_LANE0 + h, h * GDN_DK:(h + 1) * GDN_DK] = 1.0
    gl = np.arange(GROUP_W)
    bdm = (gl[:, None] // GDN_DK == gl[None, :] // GDN_DK).astype(np.float32)
    return dict(gsum=jnp.asarray(gsum, BF16), eg=jnp.asarray(eg, BF16), eb=jnp.asarray(eb, BF16),
                bdm=jnp.asarray(bdm, BF16))


def _rotary_table(pos, rows):
    half = ROPE_DIM // 2
    inv_freq = 1.0 / (ROPE_THETA ** (jnp.arange(half, dtype=F32) / half))
    ang = pos.astype(F32)[:, None] * inv_freq[None, :]
    cos, sin = jnp.cos(ang), jnp.sin(ang)
    t = pos.shape[0]
    cos2 = jnp.concatenate([cos, cos], axis=1)
    sin2 = jnp.concatenate([-sin, sin], axis=1)
    z = lambda n: jnp.zeros((t, n), F32)
    tail = HEAD_PAD - NOPE_DIM - ROPE_DIM
    tab = jnp.concatenate([jnp.ones((t, NOPE_DIM), F32), cos2, z(tail), z(NOPE_DIM), sin2, z(tail),
                           cos2, z(LANES - ROPE_DIM), sin2, z(LANES - ROPE_DIM)], axis=1)
    return jnp.tile(tab, (rows // t, 1)) if rows > t else tab


def _state_to_blockdiag(s):
    b = s.shape[0]
    s = s.reshape(b, N_GROUPS, GROUP, GDN_DK, GDN_DV)
    eye = jnp.eye(GROUP, dtype=s.dtype)
    bdiag = jnp.einsum('bghij,hk->bghikj', s, eye)
    return bdiag.reshape(b, N_GROUPS, GROUP_W, GROUP_W)


def _blockdiag_to_state(sb):
    b = sb.shape[0]
    s6 = sb.reshape(b, N_GROUPS, GROUP, GDN_DK, GROUP, GDN_DV)
    heads = [s6[:, :, h, :, h, :] for h in range(GROUP)]
    return jnp.stack(heads, axis=2).reshape(b, GDN_HEADS, GDN_DK, GDN_DV)


def _run_trunk(x, pos, past_ckv, past_kpe, s_gdn, s_conv, weights, consts):
    b, t, _ = x.shape
    n = b * t
    rot = _rotary_table(pos, max(t, TOK_TILE))
    xf = x.reshape(n, D_MODEL)
    ckv_rows, kpe_rows, states, conv_bufs = [], [], [], []
    depth = len(weights)
    for l, lw in enumerate(weights):
        xf, qkv, z, q, c_new, kpe_new, gb = _pre_call(xf, lw, rot, t)
        k_new, v_new = _kv_call(c_new, kpe_new, lw)
        q3 = q.reshape(b, t, -1)
        k3 = k_new.reshape(b, t, -1)
        v3 = v_new.reshape(b, t, -1)
        if past_ckv is None:
            m_out = _attn_prompt_call(q3, k3, v3)
        else:
            pl_len = past_ckv.shape[2]
            k_past, v_past = _kv_call(past_ckv[l].reshape(b * pl_len, KV_RANK),
                                      past_kpe[l].reshape(b * pl_len, ROPE_DIM), lw)
            m_out = _attn_sample_call(q3, k_past.reshape(b, pl_len, -1), v_past.reshape(b, pl_len, -1), k3, v3)
        qkv3 = qkv.reshape(b, t, GDN_CONV_DIM)
        cbuf8 = jnp.pad(s_conv[l], ((0, 0), (8 - (CONV_W - 1), 0), (0, 0)))
        g_out, s_bd = _gdn_call(qkv3, z.reshape(b, t, GDN_VW), gb.reshape(b, t, LANES), cbuf8,
                                _state_to_blockdiag(s_gdn[l]), lw, consts)
        final = l == depth - 1
        xf = _post_call(xf, g_out.reshape(n, GDN_VW), m_out.reshape(n, MLA_OUT), lw,
                        weights[-1]['nfinal'], final)
        ckv_rows.append(c_new.reshape(b, t, KV_RANK))
        kpe_rows.append(kpe_new.reshape(b, t, ROPE_DIM))
        states.append(_blockdiag_to_state(s_bd))
        conv_bufs.append(qkv3[:, t - (CONV_W - 1):, :])
    return (xf.reshape(b, t, D_MODEL), jnp.stack(ckv_rows), jnp.stack(kpe_rows), jnp.stack(states),
            jnp.stack(conv_bufs))


def kernel(x_prompt, x_sample, cache_mla_ckv, cache_mla_krope, state_gdn, state_gdn_conv,
           norm_ffn1, w_ffn1_gate, w_ffn1_up, w_ffn1_down, norm_mix, w_in, gdn_conv_w, gdn_a_log,
           gdn_dt_bias, gdn_norm_w, mla_q_norm, mla_kv_norm, w_uq, w_ukv, w_out, norm_ffn2,
           w_ffn2_gate, w_ffn2_up, w_ffn2_down, norm_final):
    params = dict(norm_ffn1=norm_ffn1, w_ffn1_gate=w_ffn1_gate, w_ffn1_up=w_ffn1_up, w_ffn1_down=w_ffn1_down,
                  norm_mix=norm_mix, w_in=w_in, gdn_conv_w=gdn_conv_w, gdn_a_log=gdn_a_log,
                  gdn_dt_bias=gdn_dt_bias, gdn_norm_w=gdn_norm_w, mla_q_norm=mla_q_norm,
                  mla_kv_norm=mla_kv_norm, w_uq=w_uq, w_ukv=w_ukv, w_out=w_out, norm_ffn2=norm_ffn2,
                  w_ffn2_gate=w_ffn2_gate, w_ffn2_up=w_ffn2_up, w_ffn2_down=w_ffn2_down)
    depth = w_in.shape[0]
    weights = [_layer_weights(l, params) for l in range(depth)]
    weights[-1]['nfinal'] = norm_final.reshape(1, D_MODEL)
    consts = _gdn_consts()
    b_p, t_p = x_prompt.shape[0], x_prompt.shape[1]
    zero_state = jnp.zeros((depth, b_p, GDN_HEADS, GDN_DK, GDN_DV), F32)
    zero_conv = jnp.zeros((depth, b_p, CONV_W - 1, GDN_CONV_DIM), x_prompt.dtype)
    out_p = _run_trunk(x_prompt, jnp.arange(t_p, dtype=jnp.int32), None, None, zero_state, zero_conv,
                       weights, consts)
    pos_s = cache_mla_ckv.shape[2] + jnp.arange(x_sample.shape[1], dtype=jnp.int32)
    out_s = _run_trunk(x_sample, pos_s, cache_mla_ckv, cache_mla_krope, state_gdn, state_gdn_conv,
                       weights, consts)
    return (out_p[0], out_s[0]) + out_p[1:] + out_s[1:]
```

```python
import functools

import numpy as np
import jax
import jax.numpy as jnp
from jax import lax
from jax.experimental import pallas as pl
from jax.experimental.pallas import tpu as pltpu

F32 = jnp.float32
BF16 = jnp.bfloat16

D_MODEL = 1024
D_FF = 2816
NORM_EPS = 1e-6
L2_EPS = 1e-6
CHUNK = 64
GDN_HEADS = 8
GDN_DK = 64
GDN_DV = 64
CONV_W = 4
GDN_QK = GDN_HEADS * GDN_DK
GDN_VW = GDN_HEADS * GDN_DV
GDN_CONV_DIM = 2 * GDN_QK + GDN_VW
MLA_HEADS = 8
Q_RANK = 256
KV_RANK = 128
NOPE_DIM = 64
ROPE_DIM = 32
V_DIM = 64
ROPE_THETA = 10000.0
MLA_OUT = MLA_HEADS * V_DIM
MASK_VALUE = -1e30

LANES = 128
HEAD_PAD = 128
MAIN_COLS = GDN_CONV_DIM + GDN_VW + Q_RANK + KV_RANK
FF_CHUNK = 1408
GROUP = 4
GROUP_W = GROUP * GDN_DK
N_GROUPS = GDN_HEADS // GROUP
G_LANE0 = 32
B_LANE0 = 40
VMEM_LIMIT = 56 * 1024 * 1024
TOK_TILE = 256
KV_TILE = 512
ATT_TILE = 256
ATT_HEADS = 8
GDN_TB = 256
GDN_NB = 2


def _dot(a, b):
    return jnp.dot(a, b, preferred_element_type=F32)


def _dot_nt(a, b):
    return lax.dot_general(a, b, (((1,), (1,)), ((), ())), preferred_element_type=F32)


def _dot_tn(a, b):
    return lax.dot_general(a, b, (((0,), (0,)), ((), ())), preferred_element_type=F32)


def _split_dot(x, w01, passes, left=False):
    acc = None
    r = x
    for i in range(passes):
        part = r.astype(BF16)
        d = _dot(w01, part) if left else _dot(part, w01)
        acc = d if acc is None else acc + d
        if i + 1 < passes:
            r = r - part.astype(F32)
    return acc


def _rms(x, w):
    return x * lax.rsqrt(jnp.mean(x * x, axis=-1, keepdims=True) + NORM_EPS) * w


def _silu(x):
    return x * jax.nn.sigmoid(x)


def _ffn_residual(x, nw, wg_ref, wu_ref, wd_ref):
    xn = _rms(x, nw).astype(BF16)
    acc = None
    for c0 in range(0, D_FF, FF_CHUNK):
        g = _dot(xn, wg_ref[:, c0:c0 + FF_CHUNK])
        u = _dot(xn, wu_ref[:, c0:c0 + FF_CHUNK])
        hid = (_silu(g) * u).astype(BF16)
        d = _dot(hid, wd_ref[c0:c0 + FF_CHUNK, :])
        acc = d if acc is None else acc + d
    return x + 0.5 * acc


def _pre_kernel(x_ref, nf_ref, wg_ref, wu_ref, wd_ref, nm_ref, wmain_ref, wsm_ref, gvec_ref,
                qn_ref, kvn_ref, wq2_ref, rot_ref,
                xo_ref, qkv_ref, z_ref, q_ref, c_ref, kpe_ref, gb_ref):
    x1 = _ffn_residual(x_ref[...], nf_ref[...], wg_ref, wu_ref, wd_ref)
    xo_ref[...] = x1
    h = _rms(x1, nm_ref[...]).astype(BF16)
    main = _dot(h, wmain_ref[...])
    qkv_ref[...] = main[:, :GDN_CONV_DIM]
    z_ref[...] = main[:, GDN_CONV_DIM:GDN_CONV_DIM + GDN_VW]
    c0 = GDN_CONV_DIM + GDN_VW
    cq = main[:, c0:c0 + Q_RANK]
    ckv = main[:, c0 + Q_RANK:]
    rot = rot_ref[...]
    cosq, sinq = rot[:, :LANES], rot[:, LANES:2 * LANES]
    cosk, sink = rot[:, 2 * LANES:3 * LANES], rot[:, 3 * LANES:]
    q2 = _dot(_rms(cq, qn_ref[...]).astype(BF16), wq2_ref[...])
    nq = MLA_HEADS * HEAD_PAD
    cos8 = jnp.concatenate([cosq] * MLA_HEADS, axis=1)
    sin8 = jnp.concatenate([sinq] * MLA_HEADS, axis=1)
    q_ref[...] = (q2[:, :nq] * cos8 + q2[:, nq:] * sin8).astype(BF16)
    c_ref[...] = _rms(ckv, kvn_ref[...])
    small = _dot(h, wsm_ref[...])
    s1, s2 = small[:, :LANES], small[:, LANES:]
    kpe_ref[...] = (s1 * cosk + s2 * sink)[:, :ROPE_DIM]
    gvec = gvec_ref[...]
    xa = s1 + gvec[1:2, :]
    softplus = jnp.maximum(xa, 0.0) + jnp.log1p(jnp.exp(-jnp.abs(xa)))
    gfull = -jnp.exp(gvec[0:1, :]) * softplus
    bfull = jax.nn.sigmoid(s1)
    lane = lax.broadcasted_iota(jnp.int32, s1.shape, 1)
    is_g = (lane >= G_LANE0) & (lane < G_LANE0 + GDN_HEADS)
    is_b = (lane >= B_LANE0) & (lane < B_LANE0 + GDN_HEADS)
    gb_ref[...] = jnp.where(is_g, gfull, jnp.where(is_b, bfull, 0.0))


def _post_kernel(x_ref, g_ref, m_ref, wo_ref, nf_ref, wg_ref, wu_ref, wd_ref, nfin_ref, y_ref, *, final):
    x = x_ref[...] + _dot(g_ref[...], wo_ref[:GDN_VW, :]) + _dot(m_ref[...], wo_ref[GDN_VW:, :])
    x2 = _ffn_residual(x, nf_ref[...], wg_ref, wu_ref, wd_ref)
    y_ref[...] = _rms(x2, nfin_ref[...]) if final else x2


def _const_spec(shape):
    nd = len(shape)
    return pl.BlockSpec(shape, lambda *_: (0,) * nd, pipeline_mode=pl.Buffered(1))


def _params(sem):
    return pltpu.CompilerParams(dimension_semantics=sem, vmem_limit_bytes=VMEM_LIMIT)


def _pre_call(x, lw, rot, seq_len):
    n = x.shape[0]
    tm = TOK_TILE
    rot_blocks = rot.shape[0] // tm
    row = lambda w: pl.BlockSpec((tm, w), lambda i: (i, 0))
    consts = [lw['nf1'], lw['wg1'], lw['wu1'], lw['wd1'], lw['nmix'], lw['wmain'], lw['wsm'], lw['gvec'],
              lw['qnorm'], lw['kvnorm'], lw['wq2']]
    in_specs = [row(D_MODEL)] + [_const_spec(c.shape) for c in consts]
    in_specs.append(pl.BlockSpec((tm, 4 * LANES), lambda i: (i % rot_blocks, 0)))
    out_shape = (
        jax.ShapeDtypeStruct((n, D_MODEL), F32),
        jax.ShapeDtypeStruct((n, GDN_CONV_DIM), F32),
        jax.ShapeDtypeStruct((n, GDN_VW), F32),
        jax.ShapeDtypeStruct((n, MLA_HEADS * HEAD_PAD), BF16),
        jax.ShapeDtypeStruct((n, KV_RANK), F32),
        jax.ShapeDtypeStruct((n, ROPE_DIM), F32),
        jax.ShapeDtypeStruct((n, LANES), F32),
    )
    out_specs = (row(D_MODEL), row(GDN_CONV_DIM), row(GDN_VW), row(MLA_HEADS * HEAD_PAD), row(KV_RANK),
                 row(ROPE_DIM), row(LANES))
    return pl.pallas_call(
        _pre_kernel, grid=(n // tm,), in_specs=in_specs, out_specs=out_specs, out_shape=out_shape,
        compiler_params=_params(("parallel",)), name="ffn_proj",
    )(x, *consts, rot)


def _post_call(x, g_out, m_out, lw, nfin, final):
    n = x.shape[0]
    tm = TOK_TILE
    row = lambda w: pl.BlockSpec((tm, w), lambda i: (i, 0))
    consts = [lw['wo'], lw['nf2'], lw['wg2'], lw['wu2'], lw['wd2'], nfin]
    in_specs = [row(D_MODEL), row(GDN_VW), row(MLA_OUT)] + [_const_spec(c.shape) for c in consts]
    return pl.pallas_call(
        functools.partial(_post_kernel, final=final), grid=(n // tm,), in_specs=in_specs,
        out_specs=row(D_MODEL), out_shape=jax.ShapeDtypeStruct((n, D_MODEL), F32),
        compiler_params=_params(("parallel",)), name="mix_ffn",
    )(x, g_out, m_out, *consts)


def _kv_kernel(c_ref, kpe_ref, wk_ref, we_ref, wv_ref, k_ref, v_ref, *, v_transposed):
    c = c_ref[...].astype(BF16)
    k = _dot(c, wk_ref[...]) + _dot(kpe_ref[...].astype(BF16), we_ref[...])
    k_ref[...] = k.astype(BF16)
    if v_transposed:
        v_ref[0] = _dot_nt(wv_ref[...], c).astype(BF16)
    else:
        v_ref[...] = _dot(c, wv_ref[...]).astype(BF16)


def _kv_call(c, kpe, lw, v_transposed):
    n = c.shape[0]
    tm = ATT_TILE if v_transposed else min(KV_TILE, n)
    row = lambda w: pl.BlockSpec((tm, w), lambda i: (i, 0))
    consts = [lw['wk'], lw['we'], lw['wvt'] if v_transposed else lw['wv']]
    if v_transposed:
        v_spec = pl.BlockSpec((1, MLA_OUT, tm), lambda i: (i, 0, 0))
        v_shape = jax.ShapeDtypeStruct((n // tm, MLA_OUT, tm), BF16)
    else:
        v_spec = row(MLA_OUT)
        v_shape = jax.ShapeDtypeStruct((n, MLA_OUT), BF16)
    return pl.pallas_call(
        functools.partial(_kv_kernel, v_transposed=v_transposed), grid=(n // tm,),
        in_specs=[row(KV_RANK), row(ROPE_DIM)] + [_const_spec(w.shape) for w in consts],
        out_specs=(row(MLA_HEADS * HEAD_PAD), v_spec),
        out_shape=(jax.ShapeDtypeStruct((n, MLA_HEADS * HEAD_PAD), BF16), v_shape),
        compiler_params=_params(("parallel",)), name="kv_up",
    )(c, kpe, *consts)


_ATT_SCALE = float((NOPE_DIM + ROPE_DIM) ** -0.5)
_ATT_SCALE_LOG2 = float(_ATT_SCALE * np.log2(np.e))


def _merge_heads(outs):
    lane = lax.broadcasted_iota(jnp.int32, outs[0].shape, 1)
    return jnp.where(lane < V_DIM, outs[0], outs[1])


def _attn_prompt_kernel(q_ref, k_ref, vt_ref, o_ref, s_scr, p_scr, m_scr, l_scr, acc_scr, *, tile, nh):
    qi = pl.program_id(2)
    heads = tuple(range(nh))
    pair_rows = [slice((h // 2) * 2 * V_DIM, (h // 2 + 1) * 2 * V_DIM) for h in heads]
    hsl = [slice(h * HEAD_PAD, (h + 1) * HEAD_PAD) for h in heads]

    def scores(j, h):
        k0 = pl.multiple_of(j * tile, tile)
        return _dot_nt(k_ref[0, pl.ds(k0, tile), hsl[h]], q_ref[0, :, hsl[h]]) * _ATT_SCALE_LOG2

    def pv(j, h):
        return _dot(vt_ref[j, pair_rows[h], :], p_scr[h])

    def softmax(h, s, acc_in):
        m_prev = m_scr[h]
        m_new = jnp.maximum(m_prev, jnp.max(s, axis=0, keepdims=True))
        alpha = jnp.exp2(m_prev - m_new)
        p = jnp.exp2(s - m_new)
        l_scr[h] = alpha * l_scr[h] + jnp.sum(p, axis=0, keepdims=True)
        acc_scr[h] = alpha * acc_in
        m_scr[h] = m_new
        p_scr[h] = p.astype(BF16)

    for h in heads:
        s_scr[h] = scores(0, h)
        p_scr[h] = jnp.zeros((tile, tile), BF16)
        m_scr[h] = jnp.full((1, tile), MASK_VALUE, F32)
        l_scr[h] = jnp.zeros((1, tile), F32)
        acc_scr[h] = jnp.zeros((2 * V_DIM, tile), F32)

    def body(j, carry):
        jm1 = jnp.maximum(j - 1, 0)
        accs = [acc_scr[h] + pv(jm1, h) for h in heads]
        s_next = [scores(j + 1, h) for h in heads]
        for h in heads:
            softmax(h, s_scr[h], accs[h])
        for h in heads:
            s_scr[h] = s_next[h]
        return carry

    lax.fori_loop(0, qi, body, 0)

    kc = lax.broadcasted_iota(jnp.int32, (tile, tile), 0) // CHUNK
    qc = lax.broadcasted_iota(jnp.int32, (tile, tile), 1) // CHUNK
    jm1 = jnp.maximum(qi - 1, 0)
    accs = [acc_scr[h] + pv(jm1, h) for h in heads]
    for h in heads:
        softmax(h, jnp.where(kc <= qc, s_scr[h], MASK_VALUE), accs[h])
    outs = [(acc_scr[h] + pv(qi, h)) / l_scr[h] for h in heads]
    vrow = lax.broadcasted_iota(jnp.int32, outs[0].shape, 0)
    merged = [jnp.where(vrow < V_DIM, outs[h], outs[h + 1]) for h in heads[::2]]
    o_ref[0] = jnp.concatenate(merged, axis=0).T.astype(BF16)


def _attn_prompt_call(q, k, vt):
    b, t, _ = q.shape
    tile = ATT_TILE
    nh = ATT_HEADS
    groups = MLA_HEADS // nh
    nk = t // tile
    return pl.pallas_call(
        functools.partial(_attn_prompt_kernel, tile=tile, nh=nh), grid=(b, groups, nk),
        in_specs=[pl.BlockSpec((1, tile, nh * HEAD_PAD), lambda bi, p, i: (bi, i, p)),
                  pl.BlockSpec((1, t, nh * HEAD_PAD), lambda bi, p, i: (bi, 0, p)),
                  pl.BlockSpec((nk, nh * V_DIM, tile), lambda bi, p, i: (bi, p, 0))],
        out_specs=pl.BlockSpec((1, tile, nh * V_DIM), lambda bi, p, i: (bi, i, p)),
        out_shape=jax.ShapeDtypeStruct((b, t, MLA_OUT), BF16),
        scratch_shapes=[pltpu.VMEM((nh, tile, tile), F32), pltpu.VMEM((nh, tile, tile), BF16),
                        pltpu.VMEM((nh, 1, tile), F32), pltpu.VMEM((nh, 1, tile), F32),
                        pltpu.VMEM((nh, 2 * V_DIM, tile), F32)],
        compiler_params=_params(("parallel", "parallel", "arbitrary")), name="attn_prompt",
    )(q, k, vt)


def _attn_sample_kernel(q_ref, kp_ref, vp_ref, kn_ref, vn_ref, o_ref, *, tq, past):
    pad = LANES - tq
    q_chunk = (past + lax.broadcasted_iota(jnp.int32, (tq, 1), 0)) // CHUNK
    outs = []
    for h in range(2):
        hs = slice(h * HEAD_PAD, (h + 1) * HEAD_PAD)
        q = q_ref[0, :, hs]
        s_p = _dot_nt(q, kp_ref[0, :, hs]) * _ATT_SCALE
        kc_p = lax.broadcasted_iota(jnp.int32, s_p.shape, 1) // CHUNK
        s_p = jnp.where(kc_p <= q_chunk, s_p, MASK_VALUE)
        kn = jnp.concatenate([kn_ref[0, :, hs], jnp.zeros((pad, HEAD_PAD), BF16)], axis=0)
        vn = jnp.concatenate([vn_ref[0], jnp.zeros((pad, 2 * V_DIM), BF16)], axis=0)
        s_n = _dot_nt(q, kn) * _ATT_SCALE
        col = lax.broadcasted_iota(jnp.int32, s_n.shape, 1)
        ok = (col < tq) & ((past + col) // CHUNK <= q_chunk)
        s_n = jnp.where(ok, s_n, MASK_VALUE)
        m = jnp.maximum(jnp.max(s_p, axis=-1, keepdims=True), jnp.max(s_n, axis=-1, keepdims=True))
        p_p = jnp.exp(s_p - m)
        p_n = jnp.exp(s_n - m)
        l = jnp.sum(p_p, axis=-1, keepdims=True) + jnp.sum(p_n, axis=-1, keepdims=True)
        acc = _dot(p_p.astype(BF16), vp_ref[0]) + _dot(p_n.astype(BF16), vn)
        outs.append(acc / l)
    o_ref[0] = _merge_heads(outs).astype(BF16)


def _attn_sample_call(q, k_past, v_past, k_new, v_new):
    b, tq, _ = q.shape
    past = k_past.shape[1]
    pairs = MLA_HEADS // 2
    blk = lambda rows, w: pl.BlockSpec((1, rows, w), lambda bi, p: (bi, 0, p))
    return pl.pallas_call(
        functools.partial(_attn_sample_kernel, tq=tq, past=past), grid=(b, pairs),
        in_specs=[blk(tq, 2 * HEAD_PAD), blk(past, 2 * HEAD_PAD), blk(past, 2 * V_DIM),
                  blk(tq, 2 * HEAD_PAD), blk(tq, 2 * V_DIM)],
        out_specs=blk(tq, 2 * V_DIM),
        out_shape=jax.ShapeDtypeStruct((b, tq, MLA_OUT), BF16),
        compiler_params=_params(("parallel", "parallel")), name="attn_sample",
    )(q, k_past, v_past, k_new, v_new)


def _gdn_kernel(x_ref, z_ref, gb_ref, cbuf_ref, s0_ref, cw_ref, nw_ref, gsum_ref, eg_ref, eb_ref, bdm_ref,
                o_ref, sout_ref, s_scr, tail_scr, *, nb, tb, nt):
    t = pl.program_id(1)
    rows = max(tb, CHUNK)

    @pl.when(t == 0)
    def _():
        s_scr[...] = s0_ref[...]
        tail_scr[...] = cbuf_ref[...]

    cw = cw_ref[...]
    gsum = gsum_ref[...]
    bdm = bdm_ref[...]
    bdm_f32 = bdm.astype(F32)

    def bd(m):
        mb = m.astype(BF16)
        return jnp.concatenate([mb] * GROUP, axis=0) * bdm

    ri = lax.broadcasted_iota(jnp.int32, (CHUNK, GROUP_W), 0)
    ci = lax.broadcasted_iota(jnp.int32, (CHUNK, GROUP_W), 1) % CHUNK
    causal = ri >= ci
    strict = ri > ci
    eye = jnp.where(ri == ci, 1.0, 0.0).astype(F32)
    tri = jnp.where(lax.broadcasted_iota(jnp.int32, (CHUNK, CHUNK), 0)
                    >= lax.broadcasted_iota(jnp.int32, (CHUNK, CHUNK), 1), 1.0, 0.0).astype(BF16)

    def precompute(units):
        for u in units:
            gcum = _split_dot(u['g'], tri, 3, left=True)
            grow = jnp.sum(jnp.where(ri <= ci, u['g'], 0.0), axis=0, keepdims=True)
            eg = jnp.exp(gcum)
            glast = gcum[CHUNK - 1:CHUNK, :]
            kb = u['k'] * u['b']
            u['qd'] = u['q'] * eg
            u['kd'] = u['k'] * jnp.exp(glast - gcum)
            u['eglast'] = jnp.exp(glast)
            u['vb'] = u['v'] * u['b']
            u['kg'] = kb * eg
            decay = jnp.where(causal, jnp.exp(jnp.where(causal, gcum - grow, 0.0)), 0.0)
            kq = _dot_nt(jnp.concatenate([kb, u['q']], axis=0).astype(BF16), bd(u['k']))
            u['attn'] = kq[CHUNK:] * decay
            u['m'] = -jnp.where(strict, kq[:CHUNK] * decay, 0.0)
            u['inv'] = eye + u['m']
        for u in units:
            u['m'] = _dot(u['m'].astype(BF16), bd(u['m']))
        for _ in range(4):
            for u in units:
                r = _dot(jnp.concatenate([u['m'], u['inv']], axis=0).astype(BF16), bd(u['m']))
                u['m'] = r[:CHUNK]
                u['inv'] = u['inv'] + r[CHUNK:]
        for u in units:
            u['invb'] = (u['inv'] + _dot(u['inv'].astype(BF16), bd(u['m']))).astype(BF16)
        for u in units:
            u['value'] = _dot(u['invb'], bd(u['vb']))
            u['kcum'] = _dot(u['invb'], bd(u['kg']))

    row8 = lax.broadcasted_iota(jnp.int32, (8, GDN_CONV_DIM), 0)
    units = []
    for bi in range(nb):
        mine = []
        x = x_ref[bi]
        tail = tail_scr[bi]
        conv = x * cw[CONV_W - 1:CONV_W, :]
        for sh in range(1, CONV_W):
            xr = pltpu.roll(x, sh, axis=0)
            head = jnp.where(row8 < sh, pltpu.roll(tail, sh, axis=0), xr[:8])
            shifted = jnp.concatenate([head, xr[8:]], axis=0)
            conv = conv + shifted * cw[CONV_W - 1 - sh:CONV_W - sh, :]
        tail_scr[bi] = x[tb - 8:]
        y = _silu(conv)
        gb = gb_ref[bi]
        if rows > tb:
            y = jnp.concatenate([y, jnp.zeros((rows - tb, GDN_CONV_DIM), F32)], axis=0)
            gb = jnp.concatenate([gb, jnp.zeros((rows - tb, LANES), F32)], axis=0)
        q, k, v = y[:, :GDN_QK], y[:, GDN_QK:2 * GDN_QK], y[:, 2 * GDN_QK:]
        qn = q * lax.rsqrt(_split_dot(q * q, gsum, 2) + L2_EPS) * (GDN_DK ** -0.5)
        kn = k * lax.rsqrt(_split_dot(k * k, gsum, 2) + L2_EPS)
        ge = _split_dot(gb, eg_ref[...], 3)
        be = _split_dot(gb, eb_ref[...], 3)
        for c in range(rows // CHUNK):
            rs = slice(c * CHUNK, (c + 1) * CHUNK)
            for p in range(N_GROUPS):
                ls = slice(p * GROUP_W, (p + 1) * GROUP_W)
                mine.append(dict(bi=bi, c=c, p=p, q=qn[rs, ls], k=kn[rs, ls], v=v[rs, ls], g=ge[rs, ls],
                                 b=be[rs, ls]))
        precompute(mine)
        units += mine

    state = {(bi, p): s_scr[bi, p] for bi in range(nb) for p in range(N_GROUPS)}
    outs = {}
    for c in range(rows // CHUNK):
        cur = [u for u in units if u['c'] == c]
        for u in cur:
            s = state[(u['bi'], u['p'])]
            u['r'] = _dot(jnp.concatenate([u['kcum'], u['qd']], axis=0).astype(BF16), s.astype(BF16))
        for u in cur:
            v_new = u['value'] - u['r'][:CHUNK]
            outs[(u['bi'], c, u['p'])] = u['r'][CHUNK:] + _dot(u['attn'].astype(BF16), bd(v_new))
            upd = _dot_tn(u['kd'].astype(BF16), v_new.astype(BF16))
            key = (u['bi'], u['p'])
            state[key] = state[key] * u['eglast'] + upd * bdm_f32
    for (bi, p), s in state.items():
        s_scr[bi, p] = s

    for bi in range(nb):
        o_rows = [jnp.concatenate([outs[(bi, c, p)] for p in range(N_GROUPS)], axis=1)
                  for c in range(rows // CHUNK)]
        o = jnp.concatenate(o_rows, axis=0) if len(o_rows) > 1 else o_rows[0]
        o = o[:tb]
        ms = _split_dot(o * o, gsum, 2) * (1.0 / GDN_DV)
        o_ref[bi] = (o * lax.rsqrt(ms + NORM_EPS) * nw_ref[...] * _silu(z_ref[bi])).astype(BF16)

    @pl.when(t == nt - 1)
    def _():
        sout_ref[...] = s_scr[...]


def _gdn_call(qkv, z, gb, cbuf8, s0_bd, lw, consts):
    b, t, _ = qkv.shape
    tb = min(GDN_TB, t)
    nt = t // tb
    nb = GDN_NB
    blk = lambda w: pl.BlockSpec((nb, tb, w), lambda bi, ti: (bi, ti, 0))
    per_b = lambda shape: pl.BlockSpec((nb,) + shape, lambda bi, ti: (bi,) + (0,) * len(shape))
    cs = [lw['convw'], lw['gnw'], consts['gsum'], consts['eg'], consts['eb'], consts['bdm']]
    return pl.pallas_call(
        functools.partial(_gdn_kernel, nb=nb, tb=tb, nt=nt), grid=(b // nb, nt),
        in_specs=[blk(GDN_CONV_DIM), blk(GDN_VW), blk(LANES), per_b((8, GDN_CONV_DIM)),
                  per_b((N_GROUPS, GROUP_W, GROUP_W))] + [_const_spec(c.shape) for c in cs],
        out_specs=(blk(GDN_VW), per_b((N_GROUPS, GROUP_W, GROUP_W))),
        out_shape=(jax.ShapeDtypeStruct((b, t, GDN_VW), BF16),
                   jax.ShapeDtypeStruct((b, N_GROUPS, GROUP_W, GROUP_W), F32)),
        scratch_shapes=[pltpu.VMEM((nb, N_GROUPS, GROUP_W, GROUP_W), F32),
                        pltpu.VMEM((nb, 8, GDN_CONV_DIM), F32)],
        compiler_params=_params(("parallel", "arbitrary")), name="gdn",
    )(qkv, z, gb, cbuf8, s0_bd, *cs)


def _layer_weights(l, p):
    c1 = GDN_CONV_DIM
    c2 = c1 + GDN_VW
    c3 = c2 + GDN_HEADS
    c4 = c3 + GDN_HEADS
    c5 = c4 + Q_RANK
    c6 = c5 + KV_RANK
    w_in = p['w_in'][l]
    half = ROPE_DIM // 2
    kpe_w = w_in[:, c6:]
    kpe_sw = jnp.concatenate([kpe_w[:, half:], kpe_w[:, :half]], axis=1)
    zcol = lambda n: jnp.zeros((D_MODEL, n), F32)
    wsm = jnp.concatenate([kpe_w, w_in[:, c2:c3], w_in[:, c3:c4], zcol(LANES - B_LANE0 - GDN_HEADS),
                           kpe_sw, zcol(LANES - ROPE_DIM)], axis=1)
    wmain = jnp.concatenate([w_in[:, :c2], w_in[:, c4:c6]], axis=1)
    gvec = jnp.zeros((2, LANES), F32)
    gvec = gvec.at[0, G_LANE0:G_LANE0 + GDN_HEADS].set(p['gdn_a_log'][l])
    gvec = gvec.at[1, G_LANE0:G_LANE0 + GDN_HEADS].set(p['gdn_dt_bias'][l])
    wq = p['w_uq'][l].reshape(Q_RANK, MLA_HEADS, NOPE_DIM + ROPE_DIM)
    nope, rope = wq[..., :NOPE_DIM], wq[..., NOPE_DIM:]
    rope_sw = jnp.concatenate([rope[..., half:], rope[..., :half]], axis=-1)
    zq = lambda n: jnp.zeros((Q_RANK, MLA_HEADS, n), F32)
    tail = HEAD_PAD - NOPE_DIM - ROPE_DIM
    wq_a = jnp.concatenate([nope, rope, zq(tail)], axis=-1).reshape(Q_RANK, MLA_HEADS * HEAD_PAD)
    wq_b = jnp.concatenate([zq(NOPE_DIM), rope_sw, zq(tail)], axis=-1).reshape(Q_RANK, MLA_HEADS * HEAD_PAD)
    wkv = p['w_ukv'][l].reshape(KV_RANK, MLA_HEADS, NOPE_DIM + V_DIM)
    wk = jnp.concatenate([wkv[..., :NOPE_DIM], jnp.zeros((KV_RANK, MLA_HEADS, HEAD_PAD - NOPE_DIM), F32)],
                         axis=-1).reshape(KV_RANK, MLA_HEADS * HEAD_PAD)
    wv = wkv[..., NOPE_DIM:].reshape(KV_RANK, MLA_OUT)
    we = np.zeros((ROPE_DIM, MLA_HEADS, HEAD_PAD), np.float32)
    for r in range(ROPE_DIM):
        we[r, :, NOPE_DIM + r] = 1.0
    bf = lambda a: a.astype(BF16)
    r2 = lambda a: a.reshape(1, -1)
    return dict(
        nf1=r2(p['norm_ffn1'][l]), wg1=bf(p['w_ffn1_gate'][l]), wu1=bf(p['w_ffn1_up'][l]),
        wd1=bf(p['w_ffn1_down'][l]), nmix=r2(p['norm_mix'][l]), wmain=bf(wmain), wsm=bf(wsm), gvec=gvec,
        qnorm=r2(p['mla_q_norm'][l]), kvnorm=r2(p['mla_kv_norm'][l]),
        wq2=bf(jnp.concatenate([wq_a, wq_b], axis=1)), wk=bf(wk), wv=bf(wv), wvt=bf(wv.T),
        we=jnp.asarray(we.reshape(ROPE_DIM, MLA_HEADS * HEAD_PAD), BF16),
        wo=bf(p['w_out'][l]), nf2=r2(p['norm_ffn2'][l]), wg2=bf(p['w_ffn2_gate'][l]),
        wu2=bf(p['w_ffn2_up'][l]), wd2=bf(p['w_ffn2_down'][l]),
        convw=p['gdn_conv_w'][l], gnw=jnp.tile(p['gdn_norm_w'][l], GDN_HEADS).reshape(1, GDN_VW),
    )


def _gdn_consts():
    lane = np.arange(GDN_QK)
    gsum = (lane[:, None] // GDN_DK == lane[None, :] // GDN_DK).astype(np.float32)
    eg = np.zeros((LANES, GDN_QK), np.float32)
    eb = np.zeros((LANES, GDN_QK), np.float32)
    for h in range(GDN_HEADS):
        eg[G_LANE0 + h, h * GDN_DK:(h + 1) * GDN_DK] = 1.0
        eb[B_LANE0 + h, h * GDN_DK:(h + 1) * GDN_DK] = 1.0
    gl = np.arange(GROUP_W)
    bdm = (gl[:, None] // GDN_DK == gl[None, :] // GDN_DK).astype(np.float32)
    return dict(gsum=jnp.asarray(gsum, BF16), eg=jnp.asarray(eg, BF16), eb=jnp.asarray(eb, BF16),
                bdm=jnp.asarray(bdm, BF16))


def _rotary_table(pos, rows):
    half = ROPE_DIM // 2
    inv_freq = 1.0 / (ROPE_THETA ** (jnp.arange(half, dtype=F32) / half))
    ang = pos.astype(F32)[:, None] * inv_freq[None, :]
    cos, sin = jnp.cos(ang), jnp.sin(ang)
    t = pos.shape[0]
    cos2 = jnp.concatenate([cos, cos], axis=1)
    sin2 = jnp.concatenate([-sin, sin], axis=1)
    z = lambda n: jnp.zeros((t, n), F32)
    tail = HEAD_PAD - NOPE_DIM - ROPE_DIM
    tab = jnp.concatenate([jnp.ones((t, NOPE_DIM), F32), cos2, z(tail), z(NOPE_DIM), sin2, z(tail),
                           cos2, z(LANES - ROPE_DIM), sin2, z(LANES - ROPE_DIM)], axis=1)
    return jnp.tile(tab, (rows // t, 1)) if rows > t else tab


def _state_to_blockdiag(s):
    b = s.shape[0]
    s = s.reshape(b, N_GROUPS, GROUP, GDN_DK, GDN_DV)
    eye = jnp.eye(GROUP, dtype=s.dtype)
    bdiag = jnp.einsum('bghij,hk->bghikj', s, eye)
    return bdiag.reshape(b, N_GROUPS, GROUP_W, GROUP_W)


def _blockdiag_to_state(sb):
    b = sb.shape[0]
    s6 = sb.reshape(b, N_GROUPS, GROUP, GDN_DK, GROUP, GDN_DV)
    heads = [s6[:, :, h, :, h, :] for h in range(GROUP)]
    return jnp.stack(heads, axis=2).reshape(b, GDN_HEADS, GDN_DK, GDN_DV)


def _run_trunk(x, pos, past_ckv, past_kpe, s_gdn, s_conv, weights, consts):
    b, t, _ = x.shape
    n = b * t
    rot = _rotary_table(pos, max(t, TOK_TILE))
    xf = x.reshape(n, D_MODEL)
    ckv_rows, kpe_rows, states, conv_bufs = [], [], [], []
    depth = len(weights)
    for l, lw in enumerate(weights):
        xf, qkv, z, q, c_new, kpe_new, gb = _pre_call(xf, lw, rot, t)
        k_new, v_new = _kv_call(c_new, kpe_new, lw, v_transposed=past_ckv is None)
        q3 = q.reshape(b, t, -1)
        k3 = k_new.reshape(b, t, -1)
        if past_ckv is None:
            m_out = _attn_prompt_call(q3, k3, v_new)
        else:
            pl_len = past_ckv.shape[2]
            k_past, v_past = _kv_call(past_ckv[l].reshape(b * pl_len, KV_RANK),
                                      past_kpe[l].reshape(b * pl_len, ROPE_DIM), lw, v_transposed=False)
            m_out = _attn_sample_call(q3, k_past.reshape(b, pl_len, -1), v_past.reshape(b, pl_len, -1), k3,
                                      v_new.reshape(b, t, -1))
        qkv3 = qkv.reshape(b, t, GDN_CONV_DIM)
        cbuf8 = jnp.pad(s_conv[l], ((0, 0), (8 - (CONV_W - 1), 0), (0, 0)))
        g_out, s_bd = _gdn_call(qkv3, z.reshape(b, t, GDN_VW), gb.reshape(b, t, LANES), cbuf8,
                                _state_to_blockdiag(s_gdn[l]), lw, consts)
        final = l == depth - 1
        xf = _post_call(xf, g_out.reshape(n, GDN_VW), m_out.reshape(n, MLA_OUT), lw,
                        weights[-1]['nfinal'], final)
        ckv_rows.append(c_new.reshape(b, t, KV_RANK))
        kpe_rows.append(kpe_new.reshape(b, t, ROPE_DIM))
        states.append(_blockdiag_to_state(s_bd))
        conv_bufs.append(qkv3[:, t - (CONV_W - 1):, :])
    return (xf.reshape(b, t, D_MODEL), jnp.stack(ckv_rows), jnp.stack(kpe_rows), jnp.stack(states),
            jnp.stack(conv_bufs))


def kernel(x_prompt, x_sample, cache_mla_ckv, cache_mla_krope, state_gdn, state_gdn_conv,
           norm_ffn1, w_ffn1_gate, w_ffn1_up, w_ffn1_down, norm_mix, w_in, gdn_conv_w, gdn_a_log,
           gdn_dt_bias, gdn_norm_w, mla_q_norm, mla_kv_norm, w_uq, w_ukv, w_out, norm_ffn2,
           w_ffn2_gate, w_ffn2_up, w_ffn2_down, norm_final):
    params = dict(norm_ffn1=norm_ffn1, w_ffn1_gate=w_ffn1_gate, w_ffn1_up=w_ffn1_up, w_ffn1_down=w_ffn1_down,
                  norm_mix=norm_mix, w_in=w_in, gdn_conv_w=gdn_conv_w, gdn_a_log=gdn_a_log,
                  gdn_dt_bias=gdn_dt_bias, gdn_norm_w=gdn_norm_w, mla_q_norm=mla_q_norm,
                  mla_kv_norm=mla_kv_norm, w_uq=w_uq, w_ukv=w_ukv, w_out=w_out, norm_ffn2=norm_ffn2,
                  w_ffn2_gate=w_ffn2_gate, w_ffn2_up=w_ffn2_up, w_ffn2_down=w_ffn2_down)
    depth = w_in.shape[0]
    weights = [_layer_weights(l, params) for l in range(depth)]
    weights[-1]['nfinal'] = norm_final.reshape(1, D_MODEL)
    consts = _gdn_consts()
    b_p, t_p = x_prompt.shape[0], x_prompt.shape[1]
    zero_state = jnp.zeros((depth, b_p, GDN_HEADS, GDN_DK, GDN_DV), F32)
    zero_conv = jnp.zeros((depth, b_p, CONV_W - 1, GDN_CONV_DIM), x_prompt.dtype)
    out_p = _run_trunk(x_prompt, jnp.arange(t_p, dtype=jnp.int32), None, None, zero_state, zero_conv,
                       weights, consts)
    pos_s = cache_mla_ckv.shape[2] + jnp.arange(x_sample.shape[1], dtype=jnp.int32)
    out_s = _run_trunk(x_sample, pos_s, cache_mla_ckv, cache_mla_krope, state_gdn, state_gdn_conv,
                       weights, consts)
    return (out_p[0], out_s[0]) + out_p[1:] + out_s[1:]
```

```python
import functools

import numpy as np
import jax
import jax.numpy as jnp
from jax import lax
from jax.experimental import pallas as pl
from jax.experimental.pallas import tpu as pltpu

F32 = jnp.float32
BF16 = jnp.bfloat16

D_MODEL = 1024
D_FF = 2816
NORM_EPS = 1e-6
L2_EPS = 1e-6
CHUNK = 64
GDN_HEADS = 8
GDN_DK = 64
GDN_DV = 64
CONV_W = 4
GDN_QK = GDN_HEADS * GDN_DK
GDN_VW = GDN_HEADS * GDN_DV
GDN_CONV_DIM = 2 * GDN_QK + GDN_VW
MLA_HEADS = 8
Q_RANK = 256
KV_RANK = 128
NOPE_DIM = 64
ROPE_DIM = 32
V_DIM = 64
ROPE_THETA = 10000.0
MLA_OUT = MLA_HEADS * V_DIM
MASK_VALUE = -1e30

LANES = 128
HEAD_PAD = 128
MAIN_COLS = GDN_CONV_DIM + GDN_VW + Q_RANK + KV_RANK
FF_CHUNK = 512
GROUP = 4
GROUP_W = GROUP * GDN_DK
N_GROUPS = GDN_HEADS // GROUP
G_LANE0 = 32
B_LANE0 = 40
VMEM_LIMIT = 56 * 1024 * 1024
TOK_TILE = 512
POST_TILE = 512
KV_TILE = 1024
ATT_TILE = 256
ATT_HEADS = 8
GDN_TB = 256
GDN_NB = 2


def _dot(a, b):
    return jnp.dot(a, b, preferred_element_type=F32)


def _dot_nt(a, b):
    return lax.dot_general(a, b, (((1,), (1,)), ((), ())), preferred_element_type=F32)


def _dot_tn(a, b):
    return lax.dot_general(a, b, (((0,), (0,)), ((), ())), preferred_element_type=F32)


def _split_dot(x, w01, passes, left=False):
    acc = None
    r = x
    for i in range(passes):
        part = r.astype(BF16)
        d = _dot(w01, part) if left else _dot(part, w01)
        acc = d if acc is None else acc + d
        if i + 1 < passes:
            r = r - part.astype(F32)
    return acc


def _rms(x, w):
    return x * lax.rsqrt(jnp.mean(x * x, axis=-1, keepdims=True) + NORM_EPS) * w


def _silu(x):
    return x * jax.nn.sigmoid(x)


def _ffn_residual(x, nw, wg_ref, wu_ref, wd_ref):
    xn = _rms(x, nw).astype(BF16)
    acc = None
    for c0 in range(0, D_FF, FF_CHUNK):
        c1 = min(c0 + FF_CHUNK, D_FF)
        g = _dot(xn, wg_ref[:, c0:c1])
        u = _dot(xn, wu_ref[:, c0:c1])
        hid = (_silu(g) * u).astype(BF16)
        d = _dot(hid, wd_ref[c0:c1, :])
        acc = d if acc is None else acc + d
    return x + 0.5 * acc


def _pre_kernel(x_ref, nf_ref, wg_ref, wu_ref, wd_ref, nm_ref, wmain_ref, wsm_ref, gvec_ref,
                qn_ref, kvn_ref, wq2_ref, rot_ref,
                xo_ref, qkv_ref, z_ref, q_ref, c_ref, kpe_ref, gb_ref):
    x1 = _ffn_residual(x_ref[...], nf_ref[...], wg_ref, wu_ref, wd_ref)
    xo_ref[...] = x1
    h = _rms(x1, nm_ref[...]).astype(BF16)
    qkv_ref[...] = _dot(h, wmain_ref[:, :GDN_CONV_DIM])
    c0 = GDN_CONV_DIM + GDN_VW
    z_ref[...] = _dot(h, wmain_ref[:, GDN_CONV_DIM:c0])
    lat = _dot(h, wmain_ref[:, c0:])
    cq = lat[:, :Q_RANK]
    ckv = lat[:, Q_RANK:]
    rot = rot_ref[...]
    cosq, sinq = rot[:, :LANES], rot[:, LANES:2 * LANES]
    cosk, sink = rot[:, 2 * LANES:3 * LANES], rot[:, 3 * LANES:]
    cqn = _rms(cq, qn_ref[...]).astype(BF16)
    nq = MLA_HEADS * HEAD_PAD
    cos8 = jnp.concatenate([cosq] * MLA_HEADS, axis=1)
    sin8 = jnp.concatenate([sinq] * MLA_HEADS, axis=1)
    q_ref[...] = (_dot(cqn, wq2_ref[:, :nq]) * cos8 + _dot(cqn, wq2_ref[:, nq:]) * sin8).astype(BF16)
    c_ref[...] = _rms(ckv, kvn_ref[...])
    small = _dot(h, wsm_ref[...])
    s1, s2 = small[:, :LANES], small[:, LANES:]
    kpe_ref[...] = (s1 * cosk + s2 * sink)[:, :ROPE_DIM]
    gvec = gvec_ref[...]
    xa = s1 + gvec[1:2, :]
    softplus = jnp.maximum(xa, 0.0) + jnp.log1p(jnp.exp(-jnp.abs(xa)))
    gfull = -jnp.exp(gvec[0:1, :]) * softplus
    bfull = jax.nn.sigmoid(s1)
    lane = lax.broadcasted_iota(jnp.int32, s1.shape, 1)
    is_g = (lane >= G_LANE0) & (lane < G_LANE0 + GDN_HEADS)
    is_b = (lane >= B_LANE0) & (lane < B_LANE0 + GDN_HEADS)
    gb_ref[...] = jnp.where(is_g, gfull, jnp.where(is_b, bfull, 0.0))


def _post_kernel(x_ref, g_ref, m_ref, wo_ref, nf_ref, wg_ref, wu_ref, wd_ref, nfin_ref, y_ref, *, final):
    x = x_ref[...] + _dot(g_ref[...], wo_ref[:GDN_VW, :]) + _dot(m_ref[...], wo_ref[GDN_VW:, :])
    x2 = _ffn_residual(x, nf_ref[...], wg_ref, wu_ref, wd_ref)
    y_ref[...] = _rms(x2, nfin_ref[...]) if final else x2


def _const_spec(shape):
    nd = len(shape)
    return pl.BlockSpec(shape, lambda *_: (0,) * nd, pipeline_mode=pl.Buffered(1))


def _params(sem):
    return pltpu.CompilerParams(dimension_semantics=sem, vmem_limit_bytes=VMEM_LIMIT)


def _pre_call(x, lw, rot, seq_len):
    n = x.shape[0]
    tm = min(TOK_TILE, n)
    rot_blocks = rot.shape[0] // tm
    row = lambda w: pl.BlockSpec((tm, w), lambda i: (i, 0))
    consts = [lw['nf1'], lw['wg1'], lw['wu1'], lw['wd1'], lw['nmix'], lw['wmain'], lw['wsm'], lw['gvec'],
              lw['qnorm'], lw['kvnorm'], lw['wq2']]
    in_specs = [row(D_MODEL)] + [_const_spec(c.shape) for c in consts]
    in_specs.append(pl.BlockSpec((tm, 4 * LANES), lambda i: (i % rot_blocks, 0)))
    out_shape = (
        jax.ShapeDtypeStruct((n, D_MODEL), F32),
        jax.ShapeDtypeStruct((n, GDN_CONV_DIM), F32),
        jax.ShapeDtypeStruct((n, GDN_VW), F32),
        jax.ShapeDtypeStruct((n, MLA_HEADS * HEAD_PAD), BF16),
        jax.ShapeDtypeStruct((n, KV_RANK), F32),
        jax.ShapeDtypeStruct((n, ROPE_DIM), F32),
        jax.ShapeDtypeStruct((n, LANES), F32),
    )
    out_specs = (row(D_MODEL), row(GDN_CONV_DIM), row(GDN_VW), row(MLA_HEADS * HEAD_PAD), row(KV_RANK),
                 row(ROPE_DIM), row(LANES))
    return pl.pallas_call(
        _pre_kernel, grid=(n // tm,), in_specs=in_specs, out_specs=out_specs, out_shape=out_shape,
        compiler_params=_params(("parallel",)), name="ffn_proj",
    )(x, *consts, rot)


def _post_call(x, g_out, m_out, lw, nfin, final):
    n = x.shape[0]
    tm = min(POST_TILE, n)
    row = lambda w: pl.BlockSpec((tm, w), lambda i: (i, 0))
    consts = [lw['wo'], lw['nf2'], lw['wg2'], lw['wu2'], lw['wd2'], nfin]
    in_specs = [row(D_MODEL), row(GDN_VW), row(MLA_OUT)] + [_const_spec(c.shape) for c in consts]
    return pl.pallas_call(
        functools.partial(_post_kernel, final=final), grid=(n // tm,), in_specs=in_specs,
        out_specs=row(D_MODEL), out_shape=jax.ShapeDtypeStruct((n, D_MODEL), F32),
        compiler_params=_params(("parallel",)), name="mix_ffn",
    )(x, g_out, m_out, *consts)


def _kv_kernel(c_ref, kpe_ref, wk_ref, we_ref, wv_ref, k_ref, v_ref, *, v_transposed):
    c = c_ref[...].astype(BF16)
    k = _dot(c, wk_ref[...]) + _dot(kpe_ref[...].astype(BF16), we_ref[...])
    k_ref[...] = k.astype(BF16)
    if v_transposed:
        for i in range(v_ref.shape[0]):
            v_ref[i] = _dot_nt(wv_ref[...], c[i * ATT_TILE:(i + 1) * ATT_TILE]).astype(BF16)
    else:
        v_ref[...] = _dot(c, wv_ref[...]).astype(BF16)


def _kv_call(c, kpe, lw, v_transposed):
    n = c.shape[0]
    tm = min(KV_TILE, n)
    row = lambda w: pl.BlockSpec((tm, w), lambda i: (i, 0))
    consts = [lw['wk'], lw['we'], lw['wvt'] if v_transposed else lw['wv']]
    if v_transposed:
        v_spec = pl.BlockSpec((tm // ATT_TILE, MLA_OUT, ATT_TILE), lambda i: (i, 0, 0))
        v_shape = jax.ShapeDtypeStruct((n // ATT_TILE, MLA_OUT, ATT_TILE), BF16)
    else:
        v_spec = row(MLA_OUT)
        v_shape = jax.ShapeDtypeStruct((n, MLA_OUT), BF16)
    return pl.pallas_call(
        functools.partial(_kv_kernel, v_transposed=v_transposed), grid=(n // tm,),
        in_specs=[row(KV_RANK), row(ROPE_DIM)] + [_const_spec(w.shape) for w in consts],
        out_specs=(row(MLA_HEADS * HEAD_PAD), v_spec),
        out_shape=(jax.ShapeDtypeStruct((n, MLA_HEADS * HEAD_PAD), BF16), v_shape),
        compiler_params=_params(("parallel",)), name="kv_up",
    )(c, kpe, *consts)


_ATT_SCALE = float((NOPE_DIM + ROPE_DIM) ** -0.5)
_ATT_SCALE_LOG2 = float(_ATT_SCALE * np.log2(np.e))


def _merge_heads(outs):
    lane = lax.broadcasted_iota(jnp.int32, outs[0].shape, 1)
    return jnp.where(lane < V_DIM, outs[0], outs[1])


def _attn_prompt_kernel(q_ref, k_ref, vt_ref, o_ref, s_scr, p_scr, m_scr, l_scr, acc_scr, *, tile, nh):
    qi = pl.program_id(2)
    heads = tuple(range(nh))
    pair_rows = [slice((h // 2) * 2 * V_DIM, (h // 2 + 1) * 2 * V_DIM) for h in heads]
    hsl = [slice(h * HEAD_PAD, (h + 1) * HEAD_PAD) for h in heads]

    def scores(j, h):
        k0 = pl.multiple_of(j * tile, tile)
        return _dot_nt(k_ref[0, pl.ds(k0, tile), hsl[h]], q_ref[0, :, hsl[h]]) * _ATT_SCALE_LOG2

    def pv(j, h):
        return _dot(vt_ref[j, pair_rows[h], :], p_scr[h])

    def softmax(h, s, acc_in):
        m_prev = m_scr[h]
        m_new = jnp.maximum(m_prev, jnp.max(s, axis=0, keepdims=True))
        alpha = jnp.exp2(m_prev - m_new)
        p = jnp.exp2(s - m_new)
        l_scr[h] = alpha * l_scr[h] + jnp.sum(p, axis=0, keepdims=True)
        acc_scr[h] = alpha * acc_in
        m_scr[h] = m_new
        p_scr[h] = p.astype(BF16)

    for h in heads:
        s_scr[h] = scores(0, h)
        p_scr[h] = jnp.zeros((tile, tile), BF16)
        m_scr[h] = jnp.full((1, tile), MASK_VALUE, F32)
        l_scr[h] = jnp.zeros((1, tile), F32)
        acc_scr[h] = jnp.zeros((2 * V_DIM, tile), F32)

    def body(j, carry):
        jm1 = jnp.maximum(j - 1, 0)
        accs = [acc_scr[h] + pv(jm1, h) for h in heads]
        s_next = [scores(j + 1, h) for h in heads]
        for h in heads:
            softmax(h, s_scr[h], accs[h])
        for h in heads:
            s_scr[h] = s_next[h]
        return carry

    lax.fori_loop(0, qi, body, 0)

    kc = lax.broadcasted_iota(jnp.int32, (tile, tile), 0) // CHUNK
    qc = lax.broadcasted_iota(jnp.int32, (tile, tile), 1) // CHUNK
    jm1 = jnp.maximum(qi - 1, 0)
    accs = [acc_scr[h] + pv(jm1, h) for h in heads]
    for h in heads:
        softmax(h, jnp.where(kc <= qc, s_scr[h], MASK_VALUE), accs[h])
    outs = [(acc_scr[h] + pv(qi, h)) / l_scr[h] for h in heads]
    vrow = lax.broadcasted_iota(jnp.int32, outs[0].shape, 0)
    merged = [jnp.where(vrow < V_DIM, outs[h], outs[h + 1]) for h in heads[::2]]
    o_ref[0] = jnp.concatenate(merged, axis=0).T.astype(BF16)


def _attn_prompt_call(q, k, vt):
    b, t, _ = q.shape
    tile = ATT_TILE
    nh = ATT_HEADS
    groups = MLA_HEADS // nh
    nk = t // tile
    return pl.pallas_call(
        functools.partial(_attn_prompt_kernel, tile=tile, nh=nh), grid=(b, groups, nk),
        in_specs=[pl.BlockSpec((1, tile, nh * HEAD_PAD), lambda bi, p, i: (bi, i, p)),
                  pl.BlockSpec((1, t, nh * HEAD_PAD), lambda bi, p, i: (bi, 0, p)),
                  pl.BlockSpec((nk, nh * V_DIM, tile), lambda bi, p, i: (bi, p, 0))],
        out_specs=pl.BlockSpec((1, tile, nh * V_DIM), lambda bi, p, i: (bi, i, p)),
        out_shape=jax.ShapeDtypeStruct((b, t, MLA_OUT), BF16),
        scratch_shapes=[pltpu.VMEM((nh, tile, tile), F32), pltpu.VMEM((nh, tile, tile), BF16),
                        pltpu.VMEM((nh, 1, tile), F32), pltpu.VMEM((nh, 1, tile), F32),
                        pltpu.VMEM((nh, 2 * V_DIM, tile), F32)],
        compiler_params=_params(("parallel", "parallel", "arbitrary")), name="attn_prompt",
    )(q, k, vt)


def _attn_sample_kernel(q_ref, kp_ref, vp_ref, kn_ref, vn_ref, o_ref, *, tq, past):
    pad = LANES - tq
    q_chunk = (past + lax.broadcasted_iota(jnp.int32, (tq, 1), 0)) // CHUNK
    outs = []
    for h in range(2):
        hs = slice(h * HEAD_PAD, (h + 1) * HEAD_PAD)
        q = q_ref[0, :, hs]
        s_p = _dot_nt(q, kp_ref[0, :, hs]) * _ATT_SCALE
        kc_p = lax.broadcasted_iota(jnp.int32, s_p.shape, 1) // CHUNK
        s_p = jnp.where(kc_p <= q_chunk, s_p, MASK_VALUE)
        kn = jnp.concatenate([kn_ref[0, :, hs], jnp.zeros((pad, HEAD_PAD), BF16)], axis=0)
        vn = jnp.concatenate([vn_ref[0], jnp.zeros((pad, 2 * V_DIM), BF16)], axis=0)
        s_n = _dot_nt(q, kn) * _ATT_SCALE
        col = lax.broadcasted_iota(jnp.int32, s_n.shape, 1)
        ok = (col < tq) & ((past + col) // CHUNK <= q_chunk)
        s_n = jnp.where(ok, s_n, MASK_VALUE)
        m = jnp.maximum(jnp.max(s_p, axis=-1, keepdims=True), jnp.max(s_n, axis=-1, keepdims=True))
        p_p = jnp.exp(s_p - m)
        p_n = jnp.exp(s_n - m)
        l = jnp.sum(p_p, axis=-1, keepdims=True) + jnp.sum(p_n, axis=-1, keepdims=True)
        acc = _dot(p_p.astype(BF16), vp_ref[0]) + _dot(p_n.astype(BF16), vn)
        outs.append(acc / l)
    o_ref[0] = _merge_heads(outs).astype(BF16)


def _attn_sample_call(q, k_past, v_past, k_new, v_new):
    b, tq, _ = q.shape
    past = k_past.shape[1]
    pairs = MLA_HEADS // 2
    blk = lambda rows, w: pl.BlockSpec((1, rows, w), lambda bi, p: (bi, 0, p))
    return pl.pallas_call(
        functools.partial(_attn_sample_kernel, tq=tq, past=past), grid=(b, pairs),
        in_specs=[blk(tq, 2 * HEAD_PAD), blk(past, 2 * HEAD_PAD), blk(past, 2 * V_DIM),
                  blk(tq, 2 * HEAD_PAD), blk(tq, 2 * V_DIM)],
        out_specs=blk(tq, 2 * V_DIM),
        out_shape=jax.ShapeDtypeStruct((b, tq, MLA_OUT), BF16),
        compiler_params=_params(("parallel", "parallel")), name="attn_sample",
    )(q, k_past, v_past, k_new, v_new)


def _gdn_kernel(x_ref, z_ref, gb_ref, cbuf_ref, s0_ref, cw_ref, nw_ref, gsum_ref, eg_ref, eb_ref, bdm_ref,
                o_ref, sout_ref, s_scr, tail_scr, *, nb, tb, nt):
    t = pl.program_id(1)
    rows = max(tb, CHUNK)

    @pl.when(t == 0)
    def _():
        s_scr[...] = s0_ref[...]
        tail_scr[...] = cbuf_ref[...]

    cw = cw_ref[...]
    gsum = gsum_ref[...]
    bdm = bdm_ref[...]
    bdm_f32 = bdm.astype(F32)

    def bd(m):
        mb = m.astype(BF16)
        return jnp.concatenate([mb] * GROUP, axis=0) * bdm

    ri = lax.broadcasted_iota(jnp.int32, (CHUNK, GROUP_W), 0)
    ci = lax.broadcasted_iota(jnp.int32, (CHUNK, GROUP_W), 1) % CHUNK
    causal = ri >= ci
    strict = ri > ci
    eye = jnp.where(ri == ci, 1.0, 0.0).astype(F32)
    tri = jnp.where(lax.broadcasted_iota(jnp.int32, (CHUNK, CHUNK), 0)
                    >= lax.broadcasted_iota(jnp.int32, (CHUNK, CHUNK), 1), 1.0, 0.0).astype(BF16)

    def precompute(units):
        def decay_and_products():
            for u in units:
                gcum = _split_dot(u['g'], tri, 3, left=True)
                grow = jnp.sum(jnp.where(ri <= ci, u['g'], 0.0), axis=0, keepdims=True)
                eg = jnp.exp(gcum)
                glast = gcum[CHUNK - 1:CHUNK, :]
                kb = u['k'] * u['b']
                u['qd'] = u['q'] * eg
                u['kd'] = u['k'] * jnp.exp(glast - gcum)
                u['eglast'] = jnp.exp(glast)
                u['vb'] = u['v'] * u['b']
                u['kg'] = kb * eg
                decay = jnp.where(causal, jnp.exp(jnp.where(causal, gcum - grow, 0.0)), 0.0)
                kq = _dot_nt(jnp.concatenate([kb, u['q']], axis=0).astype(BF16), bd(u['k']))
                u['attn'] = kq[CHUNK:] * decay
                u['m'] = -jnp.where(strict, kq[:CHUNK] * decay, 0.0)
                u['inv'] = eye + u['m']

        def square():
            for u in units:
                u['m'] = _dot(u['m'].astype(BF16), bd(u['m']))

        def square_and_extend():
            for u in units:
                r = _dot(jnp.concatenate([u['m'], u['inv']], axis=0).astype(BF16), bd(u['m']))
                u['m'] = r[:CHUNK]
                u['inv'] = u['inv'] + r[CHUNK:]

        def extend():
            for u in units:
                u['invb'] = (u['inv'] + _dot(u['inv'].astype(BF16), bd(u['m']))).astype(BF16)

        def solve():
            for u in units:
                u['value'] = _dot(u['invb'], bd(u['vb']))
                u['kcum'] = _dot(u['invb'], bd(u['kg']))

        return [decay_and_products, square] + [square_and_extend] * 4 + [extend, solve]

    row8 = lax.broadcasted_iota(jnp.int32, (8, GDN_CONV_DIM), 0)
    n_chunks = rows // CHUNK
    units = [[] for _ in range(nb)]

    def front(bi):
        x = x_ref[bi]
        tail = tail_scr[bi]
        conv = x * cw[CONV_W - 1:CONV_W, :]
        for sh in range(1, CONV_W):
            xr = pltpu.roll(x, sh, axis=0)
            head = jnp.where(row8 < sh, pltpu.roll(tail, sh, axis=0), xr[:8])
            shifted = jnp.concatenate([head, xr[8:]], axis=0)
            conv = conv + shifted * cw[CONV_W - 1 - sh:CONV_W - sh, :]
        tail_scr[bi] = x[tb - 8:]
        y = _silu(conv)
        gb = gb_ref[bi]
        if rows > tb:
            y = jnp.concatenate([y, jnp.zeros((rows - tb, GDN_CONV_DIM), F32)], axis=0)
            gb = jnp.concatenate([gb, jnp.zeros((rows - tb, LANES), F32)], axis=0)
        q, k, v = y[:, :GDN_QK], y[:, GDN_QK:2 * GDN_QK], y[:, 2 * GDN_QK:]
        qn = q * lax.rsqrt(_split_dot(q * q, gsum, 2) + L2_EPS) * (GDN_DK ** -0.5)
        kn = k * lax.rsqrt(_split_dot(k * k, gsum, 2) + L2_EPS)
        ge = _split_dot(gb, eg_ref[...], 3)
        be = _split_dot(gb, eb_ref[...], 3)
        for c in range(n_chunks):
            rs = slice(c * CHUNK, (c + 1) * CHUNK)
            for p in range(N_GROUPS):
                ls = slice(p * GROUP_W, (p + 1) * GROUP_W)
                units[bi].append(dict(bi=bi, c=c, p=p, q=qn[rs, ls], k=kn[rs, ls], v=v[rs, ls], g=ge[rs, ls],
                                      b=be[rs, ls]))

    state = {}
    outs = {}

    def rec_read(bi, c):
        for u in units[bi]:
            if u['c'] == c:
                s = state.setdefault((bi, u['p']), s_scr[bi, u['p']]) if c == 0 else state[(bi, u['p'])]
                u['r'] = _dot(jnp.concatenate([u['kcum'], u['qd']], axis=0).astype(BF16), s.astype(BF16))

    def rec_update(bi, c):
        for u in units[bi]:
            if u['c'] == c:
                v_new = u['value'] - u['r'][:CHUNK]
                outs[(bi, c, u['p'])] = u['r'][CHUNK:] + _dot(u['attn'].astype(BF16), bd(v_new))
                upd = _dot_tn(u['kd'].astype(BF16), v_new.astype(BF16))
                key = (bi, u['p'])
                state[key] = state[key] * u['eglast'] + upd * bdm_f32

    def finish(bi):
        for p in range(N_GROUPS):
            s_scr[bi, p] = state[(bi, p)]
        o_rows = [jnp.concatenate([outs[(bi, c, p)] for p in range(N_GROUPS)], axis=1) for c in range(n_chunks)]
        o = jnp.concatenate(o_rows, axis=0) if n_chunks > 1 else o_rows[0]
        o = o[:tb]
        ms = _split_dot(o * o, gsum, 2) * (1.0 / GDN_DV)
        o_ref[bi] = (o * lax.rsqrt(ms + NORM_EPS) * nw_ref[...] * _silu(z_ref[bi])).astype(BF16)

    def bind(fn, *args):
        return lambda: fn(*args)

    fronts = [[bind(front, bi)] for bi in range(nb)]
    pres = [precompute(units[bi]) for bi in range(nb)]
    recs = [[f for c in range(n_chunks) for f in (bind(rec_read, bi, c), bind(rec_update, bi, c))]
            + [bind(finish, bi)] for bi in range(nb)]

    def interleave(*seqs):
        seqs = [s for s in seqs if s]
        n = max(len(s) for s in seqs)
        keyed = [((i + 0.5) * n / len(s), k, f) for k, s in enumerate(seqs) for i, f in enumerate(s)]
        return [f for _, _, f in sorted(keyed, key=lambda e: (e[0], e[1]))]

    program = []
    for bi in range(nb):
        program += fronts[bi] + pres[bi]
    program += interleave(*recs)
    for thunk in program:
        thunk()

    @pl.when(t == nt - 1)
    def _():
        sout_ref[...] = s_scr[...]


def _gdn_call(qkv, z, gb, cbuf8, s0_bd, lw, consts):
    b, t, _ = qkv.shape
    tb = min(GDN_TB, t)
    nt = t // tb
    nb = GDN_NB
    blk = lambda w: pl.BlockSpec((nb, tb, w), lambda bi, ti: (bi, ti, 0))
    per_b = lambda shape: pl.BlockSpec((nb,) + shape, lambda bi, ti: (bi,) + (0,) * len(shape))
    cs = [lw['convw'], lw['gnw'], consts['gsum'], consts['eg'], consts['eb'], consts['bdm']]
    return pl.pallas_call(
        functools.partial(_gdn_kernel, nb=nb, tb=tb, nt=nt), grid=(b // nb, nt),
        in_specs=[blk(GDN_CONV_DIM), blk(GDN_VW), blk(LANES), per_b((8, GDN_CONV_DIM)),
                  per_b((N_GROUPS, GROUP_W, GROUP_W))] + [_const_spec(c.shape) for c in cs],
        out_specs=(blk(GDN_VW), per_b((N_GROUPS, GROUP_W, GROUP_W))),
        out_shape=(jax.ShapeDtypeStruct((b, t, GDN_VW), BF16),
                   jax.ShapeDtypeStruct((b, N_GROUPS, GROUP_W, GROUP_W), F32)),
        scratch_shapes=[pltpu.VMEM((nb, N_GROUPS, GROUP_W, GROUP_W), F32),
                        pltpu.VMEM((nb, 8, GDN_CONV_DIM), F32)],
        compiler_params=_params(("parallel", "arbitrary")), name="gdn",
    )(qkv, z, gb, cbuf8, s0_bd, *cs)


def _layer_weights(l, p):
    c1 = GDN_CONV_DIM
    c2 = c1 + GDN_VW
    c3 = c2 + GDN_HEADS
    c4 = c3 + GDN_HEADS
    c5 = c4 + Q_RANK
    c6 = c5 + KV_RANK
    w_in = p['w_in'][l]
    half = ROPE_DIM // 2
    kpe_w = w_in[:, c6:]
    kpe_sw = jnp.concatenate([kpe_w[:, half:], kpe_w[:, :half]], axis=1)
    zcol = lambda n: jnp.zeros((D_MODEL, n), F32)
    wsm = jnp.concatenate([kpe_w, w_in[:, c2:c3], w_in[:, c3:c4], zcol(LANES - B_LANE0 - GDN_HEADS),
                           kpe_sw, zcol(LANES - ROPE_DIM)], axis=1)
    wmain = jnp.concatenate([w_in[:, :c2], w_in[:, c4:c6]], axis=1)
    gvec = jnp.zeros((2, LANES), F32)
    gvec = gvec.at[0, G_LANE0:G_LANE0 + GDN_HEADS].set(p['gdn_a_log'][l])
    gvec = gvec.at[1, G_LANE0:G_LANE0 + GDN_HEADS].set(p['gdn_dt_bias'][l])
    wq = p['w_uq'][l].reshape(Q_RANK, MLA_HEADS, NOPE_DIM + ROPE_DIM)
    nope, rope = wq[..., :NOPE_DIM], wq[..., NOPE_DIM:]
    rope_sw = jnp.concatenate([rope[..., half:], rope[..., :half]], axis=-1)
    zq = lambda n: jnp.zeros((Q_RANK, MLA_HEADS, n), F32)
    tail = HEAD_PAD - NOPE_DIM - ROPE_DIM
    wq_a = jnp.concatenate([nope, rope, zq(tail)], axis=-1).reshape(Q_RANK, MLA_HEADS * HEAD_PAD)
    wq_b = jnp.concatenate([zq(NOPE_DIM), rope_sw, zq(tail)], axis=-1).reshape(Q_RANK, MLA_HEADS * HEAD_PAD)
    wkv = p['w_ukv'][l].reshape(KV_RANK, MLA_HEADS, NOPE_DIM + V_DIM)
    wk = jnp.concatenate([wkv[..., :NOPE_DIM], jnp.zeros((KV_RANK, MLA_HEADS, HEAD_PAD - NOPE_DIM), F32)],
                         axis=-1).reshape(KV_RANK, MLA_HEADS * HEAD_PAD)
    wv = wkv[..., NOPE_DIM:].reshape(KV_RANK, MLA_OUT)
    we = np.zeros((ROPE_DIM, MLA_HEADS, HEAD_PAD), np.float32)
    for r in range(ROPE_DIM):
        we[r, :, NOPE_DIM + r] = 1.0
    bf = lambda a: a.astype(BF16)
    r2 = lambda a: a.reshape(1, -1)
    return dict(
        nf1=r2(p['norm_ffn1'][l]), wg1=bf(p['w_ffn1_gate'][l]), wu1=bf(p['w_ffn1_up'][l]),
        wd1=bf(p['w_ffn1_down'][l]), nmix=r2(p['norm_mix'][l]), wmain=bf(wmain), wsm=bf(wsm), gvec=gvec,
        qnorm=r2(p['mla_q_norm'][l]), kvnorm=r2(p['mla_kv_norm'][l]),
        wq2=bf(jnp.concatenate([wq_a, wq_b], axis=1)), wk=bf(wk), wv=bf(wv), wvt=bf(wv.T),
        we=jnp.asarray(we.reshape(ROPE_DIM, MLA_HEADS * HEAD_PAD), BF16),
        wo=bf(p['w_out'][l]), nf2=r2(p['norm_ffn2'][l]), wg2=bf(p['w_ffn2_gate'][l]),
        wu2=bf(p['w_ffn2_up'][l]), wd2=bf(p['w_ffn2_down'][l]),
        convw=p['gdn_conv_w'][l], gnw=jnp.tile(p['gdn_norm_w'][l], GDN_HEADS).reshape(1, GDN_VW),
    )


def _gdn_consts():
    lane = np.arange(GDN_QK)
    gsum = (lane[:, None] // GDN_DK == lane[None, :] // GDN_DK).astype(np.float32)
    eg = np.zeros((LANES, GDN_QK), np.float32)
    eb = np.zeros((LANES, GDN_QK), np.float32)
    for h in range(GDN_HEADS):
        eg[G_LANE0 + h, h * GDN_DK:(h + 1) * GDN_DK] = 1.0
        eb[B_LANE0 + h, h * GDN_DK:(h + 1) * GDN_DK] = 1.0
    gl = np.arange(GROUP_W)
    bdm = (gl[:, None] // GDN_DK == gl[None, :] // GDN_DK).astype(np.float32)
    return dict(gsum=jnp.asarray(gsum, BF16), eg=jnp.asarray(eg, BF16), eb=jnp.asarray(eb, BF16),
                bdm=jnp.asarray(bdm, BF16))


def _rotary_table(pos, rows):
    half = ROPE_DIM // 2
    inv_freq = 1.0 / (ROPE_THETA ** (jnp.arange(half, dtype=F32) / half))
    ang = pos.astype(F32)[:, None] * inv_freq[None, :]
    cos, sin = jnp.cos(ang), jnp.sin(ang)
    t = pos.shape[0]
    cos2 = jnp.concatenate([cos, cos], axis=1)
    sin2 = jnp.concatenate([-sin, sin], axis=1)
    z = lambda n: jnp.zeros((t, n), F32)
    tail = HEAD_PAD - NOPE_DIM - ROPE_DIM
    tab = jnp.concatenate([jnp.ones((t, NOPE_DIM), F32), cos2, z(tail), z(NOPE_DIM), sin2, z(tail),
                           cos2, z(LANES - ROPE_DIM), sin2, z(LANES - ROPE_DIM)], axis=1)
    return jnp.tile(tab, (rows // t, 1)) if rows > t else tab


def _state_to_blockdiag(s):
    b = s.shape[0]
    s = s.reshape(b, N_GROUPS, GROUP, GDN_DK, GDN_DV)
    eye = jnp.eye(GROUP, dtype=s.dtype)
    bdiag = jnp.einsum('bghij,hk->bghikj', s, eye)
    return bdiag.reshape(b, N_GROUPS, GROUP_W, GROUP_W)


def _blockdiag_to_state(sb):
    b = sb.shape[0]
    s6 = sb.reshape(b, N_GROUPS, GROUP, GDN_DK, GROUP, GDN_DV)
    heads = [s6[:, :, h, :, h, :] for h in range(GROUP)]
    return jnp.stack(heads, axis=2).reshape(b, GDN_HEADS, GDN_DK, GDN_DV)


def _run_trunk(x, pos, past_ckv, past_kpe, s_gdn, s_conv, weights, consts):
    b, t, _ = x.shape
    n = b * t
    rot = _rotary_table(pos, max(t, min(TOK_TILE, n)))
    xf = x.reshape(n, D_MODEL)
    ckv_rows, kpe_rows, states, conv_bufs = [], [], [], []
    depth = len(weights)
    for l, lw in enumerate(weights):
        xf, qkv, z, q, c_new, kpe_new, gb = _pre_call(xf, lw, rot, t)
        k_new, v_new = _kv_call(c_new, kpe_new, lw, v_transposed=past_ckv is None)
        q3 = q.reshape(b, t, -1)
        k3 = k_new.reshape(b, t, -1)
        if past_ckv is None:
            m_out = _attn_prompt_call(q3, k3, v_new)
        else:
            pl_len = past_ckv.shape[2]
            k_past, v_past = _kv_call(past_ckv[l].reshape(b * pl_len, KV_RANK),
                                      past_kpe[l].reshape(b * pl_len, ROPE_DIM), lw, v_transposed=False)
            m_out = _attn_sample_call(q3, k_past.reshape(b, pl_len, -1), v_past.reshape(b, pl_len, -1), k3,
                                      v_new.reshape(b, t, -1))
        qkv3 = qkv.reshape(b, t, GDN_CONV_DIM)
        cbuf8 = jnp.pad(s_conv[l], ((0, 0), (8 - (CONV_W - 1), 0), (0, 0)))
        g_out, s_bd = _gdn_call(qkv3, z.reshape(b, t, GDN_VW), gb.reshape(b, t, LANES), cbuf8,
                                _state_to_blockdiag(s_gdn[l]), lw, consts)
        final = l == depth - 1
        xf = _post_call(xf, g_out.reshape(n, GDN_VW), m_out.reshape(n, MLA_OUT), lw,
                        weights[-1]['nfinal'], final)
        ckv_rows.append(c_new.reshape(b, t, KV_RANK))
        kpe_rows.append(kpe_new.reshape(b, t, ROPE_DIM))
        states.append(_blockdiag_to_state(s_bd))
        conv_bufs.append(qkv3[:, t - (CONV_W - 1):, :])
    return (xf.reshape(b, t, D_MODEL), jnp.stack(ckv_rows), jnp.stack(kpe_rows), jnp.stack(states),
            jnp.stack(conv_bufs))


def kernel(x_prompt, x_sample, cache_mla_ckv, cache_mla_krope, state_gdn, state_gdn_conv,
           norm_ffn1, w_ffn1_gate, w_ffn1_up, w_ffn1_down, norm_mix, w_in, gdn_conv_w, gdn_a_log,
           gdn_dt_bias, gdn_norm_w, mla_q_norm, mla_kv_norm, w_uq, w_ukv, w_out, norm_ffn2,
           w_ffn2_gate, w_ffn2_up, w_ffn2_down, norm_final):
    params = dict(norm_ffn1=norm_ffn1, w_ffn1_gate=w_ffn1_gate, w_ffn1_up=w_ffn1_up, w_ffn1_down=w_ffn1_down,
                  norm_mix=norm_mix, w_in=w_in, gdn_conv_w=gdn_conv_w, gdn_a_log=gdn_a_log,
                  gdn_dt_bias=gdn_dt_bias, gdn_norm_w=gdn_norm_w, mla_q_norm=mla_q_norm,
                  mla_kv_norm=mla_kv_norm, w_uq=w_uq, w_ukv=w_ukv, w_out=w_out, norm_ffn2=norm_ffn2,
                  w_ffn2_gate=w_ffn2_gate, w_ffn2_up=w_ffn2_up, w_ffn2_down=w_ffn2_down)
    depth = w_in.shape[0]
    weights = [_layer_weights(l, params) for l in range(depth)]
    weights[-1]['nfinal'] = norm_final.reshape(1, D_MODEL)
    consts = _gdn_consts()
    b_p, t_p = x_prompt.shape[0], x_prompt.shape[1]
    zero_state = jnp.zeros((depth, b_p, GDN_HEADS, GDN_DK, GDN_DV), F32)
    zero_conv = jnp.zeros((depth, b_p, CONV_W - 1, GDN_CONV_DIM), x_prompt.dtype)
    out_p = _run_trunk(x_prompt, jnp.arange(t_p, dtype=jnp.int32), None, None, zero_state, zero_conv,
                       weights, consts)
    pos_s = cache_mla_ckv.shape[2] + jnp.arange(x_sample.shape[1], dtype=jnp.int32)
    out_s = _run_trunk(x_sample, pos_s, cache_mla_ckv, cache_mla_krope, state_gdn, state_gdn_conv,
                       weights, consts)
    return (out_p[0], out_s[0]) + out_p[1:] + out_s[1:]
```

```python
import functools

import numpy as np
import jax
import jax.numpy as jnp
from jax import lax
from jax.experimental import pallas as pl
from jax.experimental.pallas import tpu as pltpu

F32 = jnp.float32
BF16 = jnp.bfloat16

D_MODEL = 1024
D_FF = 2816
NORM_EPS = 1e-6
L2_EPS = 1e-6
CHUNK = 64
GDN_HEADS = 8
GDN_DK = 64
GDN_DV = 64
CONV_W = 4
GDN_QK = GDN_HEADS * GDN_DK
GDN_VW = GDN_HEADS * GDN_DV
GDN_CONV_DIM = 2 * GDN_QK + GDN_VW
MLA_HEADS = 8
Q_RANK = 256
KV_RANK = 128
NOPE_DIM = 64
ROPE_DIM = 32
V_DIM = 64
ROPE_THETA = 10000.0
MLA_OUT = MLA_HEADS * V_DIM
MASK_VALUE = -1e30

LANES = 128
HEAD_PAD = 128
MAIN_COLS = GDN_CONV_DIM + GDN_VW + Q_RANK + KV_RANK
FF_CHUNK = 512
GROUP = 4
GROUP_W = GROUP * GDN_DK
N_GROUPS = GDN_HEADS // GROUP
G_LANE0 = 32
B_LANE0 = 40
VMEM_LIMIT = 56 * 1024 * 1024
TOK_TILE = 512
POST_TILE = 512
KV_TILE = 1024
ATT_TILE = 256
ATT_HEADS = 8
GDN_TB = 256
GDN_NB = 2


def _dot(a, b):
    return jnp.dot(a, b, preferred_element_type=F32)


def _dot_nt(a, b):
    return lax.dot_general(a, b, (((1,), (1,)), ((), ())), preferred_element_type=F32)


def _dot_tn(a, b):
    return lax.dot_general(a, b, (((0,), (0,)), ((), ())), preferred_element_type=F32)


def _split_dot(x, w01, passes, left=False):
    acc = None
    r = x
    for i in range(passes):
        part = r.astype(BF16)
        d = _dot(w01, part) if left else _dot(part, w01)
        acc = d if acc is None else acc + d
        if i + 1 < passes:
            r = r - part.astype(F32)
    return acc


def _rms(x, w):
    return x * lax.rsqrt(jnp.mean(x * x, axis=-1, keepdims=True) + NORM_EPS) * w


def _silu(x):
    return x * jax.nn.sigmoid(x)


def _ffn_residual(x, nw, wg_ref, wu_ref, wd_ref):
    xn = _rms(x, nw).astype(BF16)
    acc = None
    for c0 in range(0, D_FF, FF_CHUNK):
        c1 = min(c0 + FF_CHUNK, D_FF)
        g = _dot(xn, wg_ref[:, c0:c1])
        u = _dot(xn, wu_ref[:, c0:c1])
        hid = (_silu(g) * u).astype(BF16)
        d = _dot(hid, wd_ref[c0:c1, :])
        acc = d if acc is None else acc + d
    return x + 0.5 * acc


def _causal_conv(x, tail, cw):
    row8 = lax.broadcasted_iota(jnp.int32, (8, x.shape[1]), 0)
    conv = x * cw[CONV_W - 1:CONV_W, :]
    for sh in range(1, CONV_W):
        xr = pltpu.roll(x, sh, axis=0)
        head = jnp.where(row8 < sh, pltpu.roll(tail, sh, axis=0), xr[:8])
        conv = conv + jnp.concatenate([head, xr[8:]], axis=0) * cw[CONV_W - 1 - sh:CONV_W - sh, :]
    return conv


def _pre_kernel(x_ref, nf_ref, wg_ref, wu_ref, wd_ref, nm_ref, wmain_ref, wsm_ref, gvec_ref,
                qn_ref, kvn_ref, wq2_ref, rot_ref,
                xo_ref, qkv_ref, z_ref, q_ref, c_ref, kpe_ref, gb_ref):
    x1 = _ffn_residual(x_ref[...], nf_ref[...], wg_ref, wu_ref, wd_ref)
    xo_ref[...] = x1
    h = _rms(x1, nm_ref[...]).astype(BF16)
    qkv_ref[...] = _dot(h, wmain_ref[:, :GDN_CONV_DIM])
    c0 = GDN_CONV_DIM + GDN_VW
    z_ref[...] = _dot(h, wmain_ref[:, GDN_CONV_DIM:c0])
    lat = _dot(h, wmain_ref[:, c0:])
    cq = lat[:, :Q_RANK]
    ckv = lat[:, Q_RANK:]
    rot = rot_ref[...]
    cosq, sinq = rot[:, :LANES], rot[:, LANES:2 * LANES]
    cosk, sink = rot[:, 2 * LANES:3 * LANES], rot[:, 3 * LANES:]
    cqn = _rms(cq, qn_ref[...]).astype(BF16)
    nq = MLA_HEADS * HEAD_PAD
    cos8 = jnp.concatenate([cosq] * MLA_HEADS, axis=1)
    sin8 = jnp.concatenate([sinq] * MLA_HEADS, axis=1)
    q_ref[...] = (_dot(cqn, wq2_ref[:, :nq]) * cos8 + _dot(cqn, wq2_ref[:, nq:]) * sin8).astype(BF16)
    c_ref[...] = _rms(ckv, kvn_ref[...])
    small = _dot(h, wsm_ref[...])
    s1, s2 = small[:, :LANES], small[:, LANES:]
    kpe_ref[...] = (s1 * cosk + s2 * sink)[:, :ROPE_DIM]
    gvec = gvec_ref[...]
    xa = s1 + gvec[1:2, :]
    softplus = jnp.maximum(xa, 0.0) + jnp.log1p(jnp.exp(-jnp.abs(xa)))
    gfull = -jnp.exp(gvec[0:1, :]) * softplus
    bfull = jax.nn.sigmoid(s1)
    lane = lax.broadcasted_iota(jnp.int32, s1.shape, 1)
    is_g = (lane >= G_LANE0) & (lane < G_LANE0 + GDN_HEADS)
    is_b = (lane >= B_LANE0) & (lane < B_LANE0 + GDN_HEADS)
    gb_ref[...] = jnp.where(is_g, gfull, jnp.where(is_b, bfull, 0.0))


def _post_kernel(x_ref, g_ref, m_ref, wo_ref, nf_ref, wg_ref, wu_ref, wd_ref, nfin_ref, y_ref, *, final):
    x = x_ref[...] + _dot(g_ref[...], wo_ref[:GDN_VW, :]) + _dot(m_ref[...], wo_ref[GDN_VW:, :])
    x2 = _ffn_residual(x, nf_ref[...], wg_ref, wu_ref, wd_ref)
    y_ref[...] = _rms(x2, nfin_ref[...]) if final else x2


def _const_spec(shape):
    nd = len(shape)
    return pl.BlockSpec(shape, lambda *_: (0,) * nd, pipeline_mode=pl.Buffered(1))


def _params(sem):
    return pltpu.CompilerParams(dimension_semantics=sem, vmem_limit_bytes=VMEM_LIMIT)


def _pre_call(x, lw, rot, seq_len):
    n = x.shape[0]
    tm = min(TOK_TILE, n)
    rot_blocks = rot.shape[0] // tm
    row = lambda w: pl.BlockSpec((tm, w), lambda i: (i, 0))
    consts = [lw['nf1'], lw['wg1'], lw['wu1'], lw['wd1'], lw['nmix'], lw['wmain'], lw['wsm'], lw['gvec'],
              lw['qnorm'], lw['kvnorm'], lw['wq2']]
    in_specs = [row(D_MODEL)] + [_const_spec(c.shape) for c in consts]
    in_specs.append(pl.BlockSpec((tm, 4 * LANES), lambda i: (i % rot_blocks, 0)))
    out_shape = (
        jax.ShapeDtypeStruct((n, D_MODEL), F32),
        jax.ShapeDtypeStruct((n, GDN_CONV_DIM), F32),
        jax.ShapeDtypeStruct((n, GDN_VW), F32),
        jax.ShapeDtypeStruct((n, MLA_HEADS * HEAD_PAD), BF16),
        jax.ShapeDtypeStruct((n, KV_RANK), F32),
        jax.ShapeDtypeStruct((n, ROPE_DIM), F32),
        jax.ShapeDtypeStruct((n, LANES), F32),
    )
    out_specs = (row(D_MODEL), row(GDN_CONV_DIM), row(GDN_VW), row(MLA_HEADS * HEAD_PAD), row(KV_RANK),
                 row(ROPE_DIM), row(LANES))
    return pl.pallas_call(
        _pre_kernel, grid=(n // tm,), in_specs=in_specs, out_specs=out_specs, out_shape=out_shape,
        compiler_params=_params(("parallel",)), name="ffn_proj",
    )(x, *consts, rot)


def _post_call(x, g_out, m_out, lw, nfin, final):
    n = x.shape[0]
    tm = min(POST_TILE, n)
    row = lambda w: pl.BlockSpec((tm, w), lambda i: (i, 0))
    consts = [lw['wo'], lw['nf2'], lw['wg2'], lw['wu2'], lw['wd2'], nfin]
    in_specs = [row(D_MODEL), row(GDN_VW), row(MLA_OUT)] + [_const_spec(c.shape) for c in consts]
    return pl.pallas_call(
        functools.partial(_post_kernel, final=final), grid=(n // tm,), in_specs=in_specs,
        out_specs=row(D_MODEL), out_shape=jax.ShapeDtypeStruct((n, D_MODEL), F32),
        compiler_params=_params(("parallel",)), name="mix_ffn",
    )(x, g_out, m_out, *consts)


def _kv_kernel(c_ref, kpe_ref, wk_ref, wv_ref, k_ref, v_ref, *, v_transposed):
    c = c_ref[...].astype(BF16)
    ck = jnp.concatenate([c, kpe_ref[...].astype(BF16)], axis=1)
    k_ref[...] = _dot(ck, wk_ref[...]).astype(BF16)
    if v_transposed:
        for i in range(v_ref.shape[0]):
            v_ref[i] = _dot_nt(wv_ref[...], c[i * ATT_TILE:(i + 1) * ATT_TILE]).astype(BF16)
    else:
        v_ref[...] = _dot(c, wv_ref[...]).astype(BF16)


def _kv_call(c, kpe, lw, v_transposed):
    n = c.shape[0]
    tm = min(KV_TILE, n)
    row = lambda w: pl.BlockSpec((tm, w), lambda i: (i, 0))
    consts = [lw['wke'], lw['wvt'] if v_transposed else lw['wv']]
    if v_transposed:
        v_spec = pl.BlockSpec((tm // ATT_TILE, MLA_OUT, ATT_TILE), lambda i: (i, 0, 0))
        v_shape = jax.ShapeDtypeStruct((n // ATT_TILE, MLA_OUT, ATT_TILE), BF16)
    else:
        v_spec = row(MLA_OUT)
        v_shape = jax.ShapeDtypeStruct((n, MLA_OUT), BF16)
    return pl.pallas_call(
        functools.partial(_kv_kernel, v_transposed=v_transposed), grid=(n // tm,),
        in_specs=[row(KV_RANK), row(ROPE_DIM)] + [_const_spec(w.shape) for w in consts],
        out_specs=(row(MLA_HEADS * HEAD_PAD), v_spec),
        out_shape=(jax.ShapeDtypeStruct((n, MLA_HEADS * HEAD_PAD), BF16), v_shape),
        compiler_params=_params(("parallel",)), name="kv_up",
    )(c, kpe, *consts)


_ATT_SCALE = float((NOPE_DIM + ROPE_DIM) ** -0.5)
_ATT_SCALE_LOG2 = float(_ATT_SCALE * np.log2(np.e))


def _merge_heads(outs):
    lane = lax.broadcasted_iota(jnp.int32, outs[0].shape, 1)
    return jnp.where(lane < V_DIM, outs[0], outs[1])


def _attn_prompt_kernel(q_ref, k_ref, vt_ref, o_ref, s_scr, p_scr, m_scr, l_scr, acc_scr, *, tile, nh):
    qi = pl.program_id(2)
    heads = tuple(range(nh))
    pair_rows = [slice((h // 2) * 2 * V_DIM, (h // 2 + 1) * 2 * V_DIM) for h in heads]
    hsl = [slice(h * HEAD_PAD, (h + 1) * HEAD_PAD) for h in heads]

    def scores(j, h):
        k0 = pl.multiple_of(j * tile, tile)
        return _dot_nt(k_ref[0, pl.ds(k0, tile), hsl[h]], q_ref[0, :, hsl[h]]) * _ATT_SCALE_LOG2

    def pv(j, h):
        return _dot(vt_ref[j, pair_rows[h], :], p_scr[h])

    def softmax(h, s, acc_in):
        m_prev = m_scr[h]
        m_new = jnp.maximum(m_prev, jnp.max(s, axis=0, keepdims=True))
        alpha = jnp.exp2(m_prev - m_new)
        p = jnp.exp2(s - m_new)
        l_scr[h] = alpha * l_scr[h] + jnp.sum(p, axis=0, keepdims=True)
        acc_scr[h] = alpha * acc_in
        m_scr[h] = m_new
        p_scr[h] = p.astype(BF16)

    for h in heads:
        s_scr[h] = scores(0, h)
        p_scr[h] = jnp.zeros((tile, tile), BF16)
        m_scr[h] = jnp.full((1, tile), MASK_VALUE, F32)
        l_scr[h] = jnp.zeros((1, tile), F32)
        acc_scr[h] = jnp.zeros((2 * V_DIM, tile), F32)

    def body(j, carry):
        jm1 = jnp.maximum(j - 1, 0)
        accs = [acc_scr[h] + pv(jm1, h) for h in heads]
        s_next = [scores(j + 1, h) for h in heads]
        for h in heads:
            softmax(h, s_scr[h], accs[h])
        for h in heads:
            s_scr[h] = s_next[h]
        return carry

    lax.fori_loop(0, qi, body, 0)

    kc = lax.broadcasted_iota(jnp.int32, (tile, tile), 0) // CHUNK
    qc = lax.broadcasted_iota(jnp.int32, (tile, tile), 1) // CHUNK
    jm1 = jnp.maximum(qi - 1, 0)
    accs = [acc_scr[h] + pv(jm1, h) for h in heads]
    for h in heads:
        softmax(h, jnp.where(kc <= qc, s_scr[h], MASK_VALUE), accs[h])
    outs = [(acc_scr[h] + pv(qi, h)) / l_scr[h] for h in heads]
    vrow = lax.broadcasted_iota(jnp.int32, outs[0].shape, 0)
    merged = [jnp.where(vrow < V_DIM, outs[h], outs[h + 1]) for h in heads[::2]]
    o_ref[0] = jnp.concatenate(merged, axis=0).T.astype(BF16)


def _attn_prompt_call(q, k, vt):
    b, t, _ = q.shape
    tile = ATT_TILE
    nh = ATT_HEADS
    groups = MLA_HEADS // nh
    nk = t // tile
    return pl.pallas_call(
        functools.partial(_attn_prompt_kernel, tile=tile, nh=nh), grid=(b, groups, nk),
        in_specs=[pl.BlockSpec((1, tile, nh * HEAD_PAD), lambda bi, p, i: (bi, i, p)),
                  pl.BlockSpec((1, t, nh * HEAD_PAD), lambda bi, p, i: (bi, 0, p)),
                  pl.BlockSpec((nk, nh * V_DIM, tile), lambda bi, p, i: (bi, p, 0))],
        out_specs=pl.BlockSpec((1, tile, nh * V_DIM), lambda bi, p, i: (bi, i, p)),
        out_shape=jax.ShapeDtypeStruct((b, t, MLA_OUT), BF16),
        scratch_shapes=[pltpu.VMEM((nh, tile, tile), F32), pltpu.VMEM((nh, tile, tile), BF16),
                        pltpu.VMEM((nh, 1, tile), F32), pltpu.VMEM((nh, 1, tile), F32),
                        pltpu.VMEM((nh, 2 * V_DIM, tile), F32)],
        compiler_params=_params(("parallel", "parallel", "arbitrary")), name="attn_prompt",
    )(q, k, vt)


def _attn_sample_kernel(q_ref, kp_ref, vp_ref, kn_ref, vn_ref, o_ref, *, tq, past):
    pad = LANES - tq
    q_chunk = (past + lax.broadcasted_iota(jnp.int32, (tq, 1), 0)) // CHUNK
    pairs = []
    for pr in range(MLA_HEADS // 2):
        vs = slice(pr * 2 * V_DIM, (pr + 1) * 2 * V_DIM)
        vn = jnp.concatenate([vn_ref[0, :, vs], jnp.zeros((pad, 2 * V_DIM), BF16)], axis=0)
        outs = []
        for h in (2 * pr, 2 * pr + 1):
            hs = slice(h * HEAD_PAD, (h + 1) * HEAD_PAD)
            q = q_ref[0, :, hs]
            s_p = _dot_nt(q, kp_ref[0, :, hs]) * _ATT_SCALE
            kc_p = lax.broadcasted_iota(jnp.int32, s_p.shape, 1) // CHUNK
            s_p = jnp.where(kc_p <= q_chunk, s_p, MASK_VALUE)
            kn = jnp.concatenate([kn_ref[0, :, hs], jnp.zeros((pad, HEAD_PAD), BF16)], axis=0)
            s_n = _dot_nt(q, kn) * _ATT_SCALE
            col = lax.broadcasted_iota(jnp.int32, s_n.shape, 1)
            ok = (col < tq) & ((past + col) // CHUNK <= q_chunk)
            s_n = jnp.where(ok, s_n, MASK_VALUE)
            m = jnp.maximum(jnp.max(s_p, axis=-1, keepdims=True), jnp.max(s_n, axis=-1, keepdims=True))
            p_p = jnp.exp(s_p - m)
            p_n = jnp.exp(s_n - m)
            l = jnp.sum(p_p, axis=-1, keepdims=True) + jnp.sum(p_n, axis=-1, keepdims=True)
            acc = _dot(p_p.astype(BF16), vp_ref[0, :, vs]) + _dot(p_n.astype(BF16), vn)
            outs.append(acc / l)
        pairs.append(_merge_heads(outs))
    o_ref[0] = jnp.concatenate(pairs, axis=1).astype(BF16)


def _attn_sample_call(q, k_past, v_past, k_new, v_new):
    b, tq, _ = q.shape
    past = k_past.shape[1]
    blk = lambda rows, w: pl.BlockSpec((1, rows, w), lambda bi: (bi, 0, 0))
    return pl.pallas_call(
        functools.partial(_attn_sample_kernel, tq=tq, past=past), grid=(b,),
        in_specs=[blk(tq, MLA_HEADS * HEAD_PAD), blk(past, MLA_HEADS * HEAD_PAD), blk(past, MLA_OUT),
                  blk(tq, MLA_HEADS * HEAD_PAD), blk(tq, MLA_OUT)],
        out_specs=blk(tq, MLA_OUT),
        out_shape=jax.ShapeDtypeStruct((b, tq, MLA_OUT), BF16),
        compiler_params=_params(("parallel",)), name="attn_sample",
    )(q, k_past, v_past, k_new, v_new)


def _gdn_kernel(x_ref, z_ref, gb_ref, cbuf_ref, s0_ref, cw_ref, nw_ref, gsum_ref, eg_ref, eb_ref, bdm_ref,
                o_ref, sout_ref, s_scr, tail_scr, *, nb, tb, nt):
    t = pl.program_id(1)
    rows = max(tb, CHUNK)

    @pl.when(t == 0)
    def _():
        s_scr[...] = s0_ref[...]
        tail_scr[...] = cbuf_ref[...]

    cw = cw_ref[...]
    gsum = gsum_ref[...]
    bdm = bdm_ref[...]
    bdm_f32 = bdm.astype(F32)

    def bd(m):
        mb = m.astype(BF16)
        return jnp.concatenate([mb] * GROUP, axis=0) * bdm

    ri = lax.broadcasted_iota(jnp.int32, (CHUNK, GROUP_W), 0)
    ci = lax.broadcasted_iota(jnp.int32, (CHUNK, GROUP_W), 1) % CHUNK
    causal = ri >= ci
    strict = ri > ci
    eye = jnp.where(ri == ci, 1.0, 0.0).astype(F32)
    tri = jnp.where(lax.broadcasted_iota(jnp.int32, (CHUNK, CHUNK), 0)
                    >= lax.broadcasted_iota(jnp.int32, (CHUNK, CHUNK), 1), 1.0, 0.0).astype(BF16)

    def precompute(units):
        def decay_and_products():
            for u in units:
                gcum = _split_dot(u['g'], tri, 3, left=True)
                grow = jnp.sum(jnp.where(ri <= ci, u['g'], 0.0), axis=0, keepdims=True)
                eg = jnp.exp(gcum)
                glast = gcum[CHUNK - 1:CHUNK, :]
                kb = u['k'] * u['b']
                u['qd'] = u['q'] * eg
                u['kd'] = u['k'] * jnp.exp(glast - gcum)
                u['eglast'] = jnp.exp(glast)
                u['vb'] = u['v'] * u['b']
                u['kg'] = kb * eg
                decay = jnp.where(causal, jnp.exp(jnp.where(causal, gcum - grow, 0.0)), 0.0)
                kq = _dot_nt(jnp.concatenate([kb, u['q']], axis=0).astype(BF16), bd(u['k']))
                u['attn'] = kq[CHUNK:] * decay
                u['m'] = -jnp.where(strict, kq[:CHUNK] * decay, 0.0)
                u['inv'] = eye + u['m']

        def square():
            for u in units:
                u['m'] = _dot(u['m'].astype(BF16), bd(u['m']))

        def square_and_extend():
            for u in units:
                r = _dot(jnp.concatenate([u['m'], u['inv']], axis=0).astype(BF16), bd(u['m']))
                u['m'] = r[:CHUNK]
                u['inv'] = u['inv'] + r[CHUNK:]

        def extend():
            for u in units:
                u['invb'] = (u['inv'] + _dot(u['inv'].astype(BF16), bd(u['m']))).astype(BF16)

        def solve():
            for u in units:
                u['value'] = _dot(u['invb'], bd(u['vb']))
                u['kcum'] = _dot(u['invb'], bd(u['kg']))

        return [decay_and_products, square] + [square_and_extend] * 4 + [extend, solve]

    n_chunks = rows // CHUNK
    units = [[] for _ in range(nb)]

    def front(bi):
        x = x_ref[bi]
        y = _silu(_causal_conv(x, tail_scr[bi], cw))
        tail_scr[bi] = x[tb - 8:]
        gb = gb_ref[bi]
        if rows > tb:
            y = jnp.concatenate([y, jnp.zeros((rows - tb, GDN_CONV_DIM), F32)], axis=0)
            gb = jnp.concatenate([gb, jnp.zeros((rows - tb, LANES), F32)], axis=0)
        q, k, v = y[:, :GDN_QK], y[:, GDN_QK:2 * GDN_QK], y[:, 2 * GDN_QK:]
        qn = q * lax.rsqrt(_split_dot(q * q, gsum, 2) + L2_EPS) * (GDN_DK ** -0.5)
        kn = k * lax.rsqrt(_split_dot(k * k, gsum, 2) + L2_EPS)
        ge = _split_dot(gb, eg_ref[...], 3)
        be = _split_dot(gb, eb_ref[...], 3)
        for c in range(n_chunks):
            rs = slice(c * CHUNK, (c + 1) * CHUNK)
            for p in range(N_GROUPS):
                ls = slice(p * GROUP_W, (p + 1) * GROUP_W)
                units[bi].append(dict(bi=bi, c=c, p=p, q=qn[rs, ls], k=kn[rs, ls], v=v[rs, ls], g=ge[rs, ls],
                                      b=be[rs, ls]))

    state = {}
    outs = {}

    def rec_read(bi, c):
        for u in units[bi]:
            if u['c'] == c:
                s = state.setdefault((bi, u['p']), s_scr[bi, u['p']]) if c == 0 else state[(bi, u['p'])]
                u['r'] = _dot(jnp.concatenate([u['kcum'], u['qd']], axis=0).astype(BF16), s.astype(BF16))

    def rec_update(bi, c):
        for u in units[bi]:
            if u['c'] == c:
                v_new = u['value'] - u['r'][:CHUNK]
                outs[(bi, c, u['p'])] = u['r'][CHUNK:] + _dot(u['attn'].astype(BF16), bd(v_new))
                upd = _dot_tn(u['kd'].astype(BF16), v_new.astype(BF16))
                key = (bi, u['p'])
                state[key] = state[key] * u['eglast'] + upd * bdm_f32

    def finish(bi):
        for p in range(N_GROUPS):
            s_scr[bi, p] = state[(bi, p)]
        o_rows = [jnp.concatenate([outs[(bi, c, p)] for p in range(N_GROUPS)], axis=1) for c in range(n_chunks)]
        o = jnp.concatenate(o_rows, axis=0) if n_chunks > 1 else o_rows[0]
        o = o[:tb]
        ms = _split_dot(o * o, gsum, 2) * (1.0 / GDN_DV)
        o_ref[bi] = (o * lax.rsqrt(ms + NORM_EPS) * nw_ref[...] * _silu(z_ref[bi])).astype(BF16)

    def bind(fn, *args):
        return lambda: fn(*args)

    fronts = [[bind(front, bi)] for bi in range(nb)]
    pres = [precompute(units[bi]) for bi in range(nb)]
    recs = [[f for c in range(n_chunks) for f in (bind(rec_read, bi, c), bind(rec_update, bi, c))]
            + [bind(finish, bi)] for bi in range(nb)]

    def interleave(*seqs):
        seqs = [s for s in seqs if s]
        n = max(len(s) for s in seqs)
        keyed = [((i + 0.5) * n / len(s), k, f) for k, s in enumerate(seqs) for i, f in enumerate(s)]
        return [f for _, _, f in sorted(keyed, key=lambda e: (e[0], e[1]))]

    program = []
    for bi in range(nb):
        program += fronts[bi] + pres[bi]
    program += interleave(*recs)
    for thunk in program:
        thunk()

    @pl.when(t == nt - 1)
    def _():
        sout_ref[...] = s_scr[...]


def _gdn_call(qkv, z, gb, cbuf8, s0_bd, lw, consts):
    b, t, _ = qkv.shape
    tb = min(GDN_TB, t)
    nt = t // tb
    nb = GDN_NB
    blk = lambda w: pl.BlockSpec((nb, tb, w), lambda bi, ti: (bi, ti, 0))
    per_b = lambda shape: pl.BlockSpec((nb,) + shape, lambda bi, ti: (bi,) + (0,) * len(shape))
    cs = [lw['convw'], lw['gnw'], consts['gsum'], consts['eg'], consts['eb'], consts['bdm']]
    return pl.pallas_call(
        functools.partial(_gdn_kernel, nb=nb, tb=tb, nt=nt), grid=(b // nb, nt),
        in_specs=[blk(GDN_CONV_DIM), blk(GDN_VW), blk(LANES), per_b((8, GDN_CONV_DIM)),
                  per_b((N_GROUPS, GROUP_W, GROUP_W))] + [_const_spec(c.shape) for c in cs],
        out_specs=(blk(GDN_VW), per_b((N_GROUPS, GROUP_W, GROUP_W))),
        out_shape=(jax.ShapeDtypeStruct((b, t, GDN_VW), BF16),
                   jax.ShapeDtypeStruct((b, N_GROUPS, GROUP_W, GROUP_W), F32)),
        scratch_shapes=[pltpu.VMEM((nb, N_GROUPS, GROUP_W, GROUP_W), F32),
                        pltpu.VMEM((nb, 8, GDN_CONV_DIM), F32)],
        compiler_params=_params(("parallel", "arbitrary")), name="gdn",
    )(qkv, z, gb, cbuf8, s0_bd, *cs)


def _layer_weights(l, p):
    c1 = GDN_CONV_DIM
    c2 = c1 + GDN_VW
    c3 = c2 + GDN_HEADS
    c4 = c3 + GDN_HEADS
    c5 = c4 + Q_RANK
    c6 = c5 + KV_RANK
    w_in = p['w_in'][l]
    half = ROPE_DIM // 2
    kpe_w = w_in[:, c6:]
    kpe_sw = jnp.concatenate([kpe_w[:, half:], kpe_w[:, :half]], axis=1)
    zcol = lambda n: jnp.zeros((D_MODEL, n), F32)
    wsm = jnp.concatenate([kpe_w, w_in[:, c2:c3], w_in[:, c3:c4], zcol(LANES - B_LANE0 - GDN_HEADS),
                           kpe_sw, zcol(LANES - ROPE_DIM)], axis=1)
    wmain = jnp.concatenate([w_in[:, :c2], w_in[:, c4:c6]], axis=1)
    gvec = jnp.zeros((2, LANES), F32)
    gvec = gvec.at[0, G_LANE0:G_LANE0 + GDN_HEADS].set(p['gdn_a_log'][l])
    gvec = gvec.at[1, G_LANE0:G_LANE0 + GDN_HEADS].set(p['gdn_dt_bias'][l])
    wq = p['w_uq'][l].reshape(Q_RANK, MLA_HEADS, NOPE_DIM + ROPE_DIM)
    nope, rope = wq[..., :NOPE_DIM], wq[..., NOPE_DIM:]
    rope_sw = jnp.concatenate([rope[..., half:], rope[..., :half]], axis=-1)
    zq = lambda n: jnp.zeros((Q_RANK, MLA_HEADS, n), F32)
    tail = HEAD_PAD - NOPE_DIM - ROPE_DIM
    wq_a = jnp.concatenate([nope, rope, zq(tail)], axis=-1).reshape(Q_RANK, MLA_HEADS * HEAD_PAD)
    wq_b = jnp.concatenate([zq(NOPE_DIM), rope_sw, zq(tail)], axis=-1).reshape(Q_RANK, MLA_HEADS * HEAD_PAD)
    wkv = p['w_ukv'][l].reshape(KV_RANK, MLA_HEADS, NOPE_DIM + V_DIM)
    wk = jnp.concatenate([wkv[..., :NOPE_DIM], jnp.zeros((KV_RANK, MLA_HEADS, HEAD_PAD - NOPE_DIM), F32)],
                         axis=-1).reshape(KV_RANK, MLA_HEADS * HEAD_PAD)
    wv = wkv[..., NOPE_DIM:].reshape(KV_RANK, MLA_OUT)
    we = np.zeros((ROPE_DIM, MLA_HEADS, HEAD_PAD), np.float32)
    for r in range(ROPE_DIM):
        we[r, :, NOPE_DIM + r] = 1.0
    bf = lambda a: a.astype(BF16)
    r2 = lambda a: a.reshape(1, -1)
    return dict(
        nf1=r2(p['norm_ffn1'][l]), wg1=bf(p['w_ffn1_gate'][l]), wu1=bf(p['w_ffn1_up'][l]),
        wd1=bf(p['w_ffn1_down'][l]), nmix=r2(p['norm_mix'][l]), wmain=bf(wmain), wsm=bf(wsm), gvec=gvec,
        qnorm=r2(p['mla_q_norm'][l]), kvnorm=r2(p['mla_kv_norm'][l]),
        wq2=bf(jnp.concatenate([wq_a, wq_b], axis=1)), wv=bf(wv), wvt=bf(wv.T),
        wke=bf(jnp.concatenate([wk, jnp.asarray(we.reshape(ROPE_DIM, MLA_HEADS * HEAD_PAD))], axis=0)),
        wo=bf(p['w_out'][l]), nf2=r2(p['norm_ffn2'][l]), wg2=bf(p['w_ffn2_gate'][l]),
        wu2=bf(p['w_ffn2_up'][l]), wd2=bf(p['w_ffn2_down'][l]),
        convw=p['gdn_conv_w'][l], gnw=jnp.tile(p['gdn_norm_w'][l], GDN_HEADS).reshape(1, GDN_VW),
    )


def _gdn_consts():
    lane = np.arange(GDN_QK)
    gsum = (lane[:, None] // GDN_DK == lane[None, :] // GDN_DK).astype(np.float32)
    eg = np.zeros((LANES, GDN_QK), np.float32)
    eb = np.zeros((LANES, GDN_QK), np.float32)
    for h in range(GDN_HEADS):
        eg[G_LANE0 + h, h * GDN_DK:(h + 1) * GDN_DK] = 1.0
        eb[B_LANE0 + h, h * GDN_DK:(h + 1) * GDN_DK] = 1.0
    gl = np.arange(GROUP_W)
    bdm = (gl[:, None] // GDN_DK == gl[None, :] // GDN_DK).astype(np.float32)
    return dict(gsum=jnp.asarray(gsum, BF16), eg=jnp.asarray(eg, BF16), eb=jnp.asarray(eb, BF16),
                bdm=jnp.asarray(bdm, BF16))


def _rotary_table(pos, rows):
    half = ROPE_DIM // 2
    inv_freq = 1.0 / (ROPE_THETA ** (jnp.arange(half, dtype=F32) / half))
    ang = pos.astype(F32)[:, None] * inv_freq[None, :]
    cos, sin = jnp.cos(ang), jnp.sin(ang)
    t = pos.shape[0]
    cos2 = jnp.concatenate([cos, cos], axis=1)
    sin2 = jnp.concatenate([-sin, sin], axis=1)
    z = lambda n: jnp.zeros((t, n), F32)
    tail = HEAD_PAD - NOPE_DIM - ROPE_DIM
    tab = jnp.concatenate([jnp.ones((t, NOPE_DIM), F32), cos2, z(tail), z(NOPE_DIM), sin2, z(tail),
                           cos2, z(LANES - ROPE_DIM), sin2, z(LANES - ROPE_DIM)], axis=1)
    return jnp.tile(tab, (rows // t, 1)) if rows > t else tab


def _state_to_blockdiag(s):
    b = s.shape[0]
    s = s.reshape(b, N_GROUPS, GROUP, GDN_DK, GDN_DV)
    eye = jnp.eye(GROUP, dtype=s.dtype)
    bdiag = jnp.einsum('bghij,hk->bghikj', s, eye)
    return bdiag.reshape(b, N_GROUPS, GROUP_W, GROUP_W)


def _blockdiag_to_state(sb):
    b = sb.shape[0]
    s6 = sb.reshape(b, N_GROUPS, GROUP, GDN_DK, GROUP, GDN_DV)
    heads = [s6[:, :, h, :, h, :] for h in range(GROUP)]
    return jnp.stack(heads, axis=2).reshape(b, GDN_HEADS, GDN_DK, GDN_DV)


def _run_trunk(x, pos, past_ckv, past_kpe, s_gdn, s_conv, weights, consts):
    b, t, _ = x.shape
    n = b * t
    rot = _rotary_table(pos, max(t, min(TOK_TILE, n)))
    xf = x.reshape(n, D_MODEL)
    ckv_rows, kpe_rows, states, conv_bufs = [], [], [], []
    depth = len(weights)
    for l, lw in enumerate(weights):
        if s_gdn is None:
            cbuf8 = jnp.zeros((b, 8, GDN_CONV_DIM), F32)
            s0_bd = jnp.zeros((b, N_GROUPS, GROUP_W, GROUP_W), F32)
        else:
            cbuf8 = jnp.pad(s_conv[l], ((0, 0), (8 - (CONV_W - 1), 0), (0, 0)))
            s0_bd = _state_to_blockdiag(s_gdn[l])
        xf, qkv, z, q, c_new, kpe_new, gb = _pre_call(xf, lw, rot, t)
        k_new, v_new = _kv_call(c_new, kpe_new, lw, v_transposed=past_ckv is None)
        q3 = q.reshape(b, t, -1)
        k3 = k_new.reshape(b, t, -1)
        if past_ckv is None:
            m_out = _attn_prompt_call(q3, k3, v_new)
        else:
            pl_len = past_ckv.shape[2]
            k_past, v_past = _kv_call(past_ckv[l].reshape(b * pl_len, KV_RANK),
                                      past_kpe[l].reshape(b * pl_len, ROPE_DIM), lw, v_transposed=False)
            m_out = _attn_sample_call(q3, k_past.reshape(b, pl_len, -1), v_past.reshape(b, pl_len, -1), k3,
                                      v_new.reshape(b, t, -1))
        qkv3 = qkv.reshape(b, t, GDN_CONV_DIM)
        g_out, s_bd = _gdn_call(qkv3, z.reshape(b, t, GDN_VW), gb.reshape(b, t, LANES), cbuf8, s0_bd, lw,
                                consts)
        final = l == depth - 1
        xf = _post_call(xf, g_out.reshape(n, GDN_VW), m_out.reshape(n, MLA_OUT), lw,
                        weights[-1]['nfinal'], final)
        ckv_rows.append(c_new.reshape(b, t, KV_RANK))
        kpe_rows.append(kpe_new.reshape(b, t, ROPE_DIM))
        states.append(_blockdiag_to_state(s_bd))
        conv_bufs.append(qkv3[:, t - (CONV_W - 1):, :])
    return (xf.reshape(b, t, D_MODEL), jnp.stack(ckv_rows), jnp.stack(kpe_rows), jnp.stack(states),
            jnp.stack(conv_bufs))


def kernel(x_prompt, x_sample, cache_mla_ckv, cache_mla_krope, state_gdn, state_gdn_conv,
           norm_ffn1, w_ffn1_gate, w_ffn1_up, w_ffn1_down, norm_mix, w_in, gdn_conv_w, gdn_a_log,
           gdn_dt_bias, gdn_norm_w, mla_q_norm, mla_kv_norm, w_uq, w_ukv, w_out, norm_ffn2,
           w_ffn2_gate, w_ffn2_up, w_ffn2_down, norm_final):
    params = dict(norm_ffn1=norm_ffn1, w_ffn1_gate=w_ffn1_gate, w_ffn1_up=w_ffn1_up, w_ffn1_down=w_ffn1_down,
                  norm_mix=norm_mix, w_in=w_in, gdn_conv_w=gdn_conv_w, gdn_a_log=gdn_a_log,
                  gdn_dt_bias=gdn_dt_bias, gdn_norm_w=gdn_norm_w, mla_q_norm=mla_q_norm,
                  mla_kv_norm=mla_kv_norm, w_uq=w_uq, w_ukv=w_ukv, w_out=w_out, norm_ffn2=norm_ffn2,
                  w_ffn2_gate=w_ffn2_gate, w_ffn2_up=w_ffn2_up, w_ffn2_down=w_ffn2_down)
    depth = w_in.shape[0]
    weights = [_layer_weights(l, params) for l in range(depth)]
    weights[-1]['nfinal'] = norm_final.reshape(1, D_MODEL)
    consts = _gdn_consts()
    t_p = x_prompt.shape[1]
    out_p = _run_trunk(x_prompt, jnp.arange(t_p, dtype=jnp.int32), None, None, None, None, weights, consts)
    pos_s = cache_mla_ckv.shape[2] + jnp.arange(x_sample.shape[1], dtype=jnp.int32)
    out_s = _run_trunk(x_sample, pos_s, cache_mla_ckv, cache_mla_krope, state_gdn, state_gdn_conv,
                       weights, consts)
    return (out_p[0], out_s[0]) + out_p[1:] + out_s[1:]
```

```python
import functools

import numpy as np
import jax
import jax.numpy as jnp
from jax import lax
from jax.experimental import pallas as pl
from jax.experimental.pallas import tpu as pltpu

F32 = jnp.float32
BF16 = jnp.bfloat16

D_MODEL = 1024
D_FF = 2816
NORM_EPS = 1e-6
L2_EPS = 1e-6
CHUNK = 64
GDN_HEADS = 8
GDN_DK = 64
GDN_DV = 64
CONV_W = 4
GDN_QK = GDN_HEADS * GDN_DK
GDN_VW = GDN_HEADS * GDN_DV
GDN_CONV_DIM = 2 * GDN_QK + GDN_VW
MLA_HEADS = 8
Q_RANK = 256
KV_RANK = 128
NOPE_DIM = 64
ROPE_DIM = 32
V_DIM = 64
ROPE_THETA = 10000.0
MLA_OUT = MLA_HEADS * V_DIM
MASK_VALUE = -1e30

LANES = 128
HEAD_PAD = 128
MAIN_COLS = GDN_CONV_DIM + GDN_VW + Q_RANK + KV_RANK
FF_CHUNK = 512
GROUP = 4
GROUP_W = GROUP * GDN_DK
N_GROUPS = GDN_HEADS // GROUP
G_LANE0 = 32
B_LANE0 = 40
VMEM_LIMIT = 56 * 1024 * 1024
TOK_TILE = 512
POST_TILE = 512
KV_TILE = 1024
ATT_TILE = 256
ATT_HEADS = 8
GDN_TB = 256
GDN_NB = 2
INV_BASE = 8


def _dot(a, b):
    return jnp.dot(a, b, preferred_element_type=F32)


def _dot_nt(a, b):
    return lax.dot_general(a, b, (((1,), (1,)), ((), ())), preferred_element_type=F32)


def _dot_tn(a, b):
    return lax.dot_general(a, b, (((0,), (0,)), ((), ())), preferred_element_type=F32)


def _split_dot(x, w01, passes, left=False):
    acc = None
    r = x
    for i in range(passes):
        part = r.astype(BF16)
        d = _dot(w01, part) if left else _dot(part, w01)
        acc = d if acc is None else acc + d
        if i + 1 < passes:
            r = r - part.astype(F32)
    return acc


def _rms(x, w):
    return x * lax.rsqrt(jnp.mean(x * x, axis=-1, keepdims=True) + NORM_EPS) * w


def _silu(x):
    return x * jax.nn.sigmoid(x)


def _ffn_residual(x, nw, wg_ref, wu_ref, wd_ref):
    xn = _rms(x, nw).astype(BF16)
    acc = None
    for c0 in range(0, D_FF, FF_CHUNK):
        c1 = min(c0 + FF_CHUNK, D_FF)
        g = _dot(xn, wg_ref[:, c0:c1])
        u = _dot(xn, wu_ref[:, c0:c1])
        hid = (_silu(g) * u).astype(BF16)
        d = _dot(hid, wd_ref[c0:c1, :])
        acc = d if acc is None else acc + d
    return x + 0.5 * acc


def _causal_conv(x, tail, cw):
    row8 = lax.broadcasted_iota(jnp.int32, (8, x.shape[1]), 0)
    conv = x * cw[CONV_W - 1:CONV_W, :]
    for sh in range(1, CONV_W):
        xr = pltpu.roll(x, sh, axis=0)
        head = jnp.where(row8 < sh, pltpu.roll(tail, sh, axis=0), xr[:8])
        conv = conv + jnp.concatenate([head, xr[8:]], axis=0) * cw[CONV_W - 1 - sh:CONV_W - sh, :]
    return conv


def _pre_kernel(x_ref, nf_ref, wg_ref, wu_ref, wd_ref, nm_ref, wmain_ref, wsm_ref, gvec_ref,
                qn_ref, kvn_ref, wq2_ref, rot_ref,
                xo_ref, qkv_ref, z_ref, q_ref, c_ref, kpe_ref, gb_ref):
    x1 = _ffn_residual(x_ref[...], nf_ref[...], wg_ref, wu_ref, wd_ref)
    xo_ref[...] = x1
    h = _rms(x1, nm_ref[...]).astype(BF16)
    qkv_ref[...] = _dot(h, wmain_ref[:, :GDN_CONV_DIM])
    c0 = GDN_CONV_DIM + GDN_VW
    z_ref[...] = _dot(h, wmain_ref[:, GDN_CONV_DIM:c0])
    lat = _dot(h, wmain_ref[:, c0:])
    cq = lat[:, :Q_RANK]
    ckv = lat[:, Q_RANK:]
    rot = rot_ref[...]
    cosq, sinq = rot[:, :LANES], rot[:, LANES:2 * LANES]
    cosk, sink = rot[:, 2 * LANES:3 * LANES], rot[:, 3 * LANES:]
    cqn = _rms(cq, qn_ref[...]).astype(BF16)
    nq = MLA_HEADS * HEAD_PAD
    cos8 = jnp.concatenate([cosq] * MLA_HEADS, axis=1)
    sin8 = jnp.concatenate([sinq] * MLA_HEADS, axis=1)
    q_ref[...] = (_dot(cqn, wq2_ref[:, :nq]) * cos8 + _dot(cqn, wq2_ref[:, nq:]) * sin8).astype(BF16)
    c_ref[...] = _rms(ckv, kvn_ref[...])
    small = _dot(h, wsm_ref[...])
    s1, s2 = small[:, :LANES], small[:, LANES:]
    kpe_ref[...] = (s1 * cosk + s2 * sink)[:, :ROPE_DIM]
    gvec = gvec_ref[...]
    xa = s1 + gvec[1:2, :]
    softplus = jnp.maximum(xa, 0.0) + jnp.log1p(jnp.exp(-jnp.abs(xa)))
    gfull = -jnp.exp(gvec[0:1, :]) * softplus
    bfull = jax.nn.sigmoid(s1)
    lane = lax.broadcasted_iota(jnp.int32, s1.shape, 1)
    is_g = (lane >= G_LANE0) & (lane < G_LANE0 + GDN_HEADS)
    is_b = (lane >= B_LANE0) & (lane < B_LANE0 + GDN_HEADS)
    gb_ref[...] = jnp.where(is_g, gfull, jnp.where(is_b, bfull, 0.0))


def _post_kernel(x_ref, g_ref, m_ref, wo_ref, nf_ref, wg_ref, wu_ref, wd_ref, nfin_ref, y_ref, *, final):
    x = x_ref[...] + _dot(g_ref[...], wo_ref[:GDN_VW, :]) + _dot(m_ref[...], wo_ref[GDN_VW:, :])
    x2 = _ffn_residual(x, nf_ref[...], wg_ref, wu_ref, wd_ref)
    y_ref[...] = _rms(x2, nfin_ref[...]) if final else x2


def _const_spec(shape):
    nd = len(shape)
    return pl.BlockSpec(shape, lambda *_: (0,) * nd, pipeline_mode=pl.Buffered(1))


def _params(sem):
    return pltpu.CompilerParams(dimension_semantics=sem, vmem_limit_bytes=VMEM_LIMIT)


def _pre_call(x, lw, rot, seq_len):
    n = x.shape[0]
    tm = min(TOK_TILE, n)
    rot_blocks = rot.shape[0] // tm
    row = lambda w: pl.BlockSpec((tm, w), lambda i: (i, 0))
    consts = [lw['nf1'], lw['wg1'], lw['wu1'], lw['wd1'], lw['nmix'], lw['wmain'], lw['wsm'], lw['gvec'],
              lw['qnorm'], lw['kvnorm'], lw['wq2']]
    in_specs = [row(D_MODEL)] + [_const_spec(c.shape) for c in consts]
    in_specs.append(pl.BlockSpec((tm, 4 * LANES), lambda i: (i % rot_blocks, 0)))
    out_shape = (
        jax.ShapeDtypeStruct((n, D_MODEL), F32),
        jax.ShapeDtypeStruct((n, GDN_CONV_DIM), F32),
        jax.ShapeDtypeStruct((n, GDN_VW), F32),
        jax.ShapeDtypeStruct((n, MLA_HEADS * HEAD_PAD), BF16),
        jax.ShapeDtypeStruct((n, KV_RANK), F32),
        jax.ShapeDtypeStruct((n, ROPE_DIM), F32),
        jax.ShapeDtypeStruct((n, LANES), F32),
    )
    out_specs = (row(D_MODEL), row(GDN_CONV_DIM), row(GDN_VW), row(MLA_HEADS * HEAD_PAD), row(KV_RANK),
                 row(ROPE_DIM), row(LANES))
    return pl.pallas_call(
        _pre_kernel, grid=(n // tm,), in_specs=in_specs, out_specs=out_specs, out_shape=out_shape,
        compiler_params=_params(("parallel",)), name="ffn_proj",
    )(x, *consts, rot)


def _post_call(x, g_out, m_out, lw, nfin, final):
    n = x.shape[0]
    tm = min(POST_TILE, n)
    row = lambda w: pl.BlockSpec((tm, w), lambda i: (i, 0))
    consts = [lw['wo'], lw['nf2'], lw['wg2'], lw['wu2'], lw['wd2'], nfin]
    in_specs = [row(D_MODEL), row(GDN_VW), row(MLA_OUT)] + [_const_spec(c.shape) for c in consts]
    return pl.pallas_call(
        functools.partial(_post_kernel, final=final), grid=(n // tm,), in_specs=in_specs,
        out_specs=row(D_MODEL), out_shape=jax.ShapeDtypeStruct((n, D_MODEL), F32),
        compiler_params=_params(("parallel",)), name="mix_ffn",
    )(x, g_out, m_out, *consts)


def _kv_kernel(c_ref, kpe_ref, wk_ref, wv_ref, k_ref, v_ref, *, v_transposed):
    c = c_ref[...].astype(BF16)
    ck = jnp.concatenate([c, kpe_ref[...].astype(BF16)], axis=1)
    k_ref[...] = _dot(ck, wk_ref[...]).astype(BF16)
    if v_transposed:
        for i in range(v_ref.shape[0]):
            v_ref[i] = _dot_nt(wv_ref[...], c[i * ATT_TILE:(i + 1) * ATT_TILE]).astype(BF16)
    else:
        v_ref[...] = _dot(c, wv_ref[...]).astype(BF16)


def _kv_call(c, kpe, lw, v_transposed):
    n = c.shape[0]
    tm = min(KV_TILE, n)
    row = lambda w: pl.BlockSpec((tm, w), lambda i: (i, 0))
    consts = [lw['wke'], lw['wvt'] if v_transposed else lw['wv']]
    if v_transposed:
        v_spec = pl.BlockSpec((tm // ATT_TILE, MLA_OUT, ATT_TILE), lambda i: (i, 0, 0))
        v_shape = jax.ShapeDtypeStruct((n // ATT_TILE, MLA_OUT, ATT_TILE), BF16)
    else:
        v_spec = row(MLA_OUT)
        v_shape = jax.ShapeDtypeStruct((n, MLA_OUT), BF16)
    return pl.pallas_call(
        functools.partial(_kv_kernel, v_transposed=v_transposed), grid=(n // tm,),
        in_specs=[row(KV_RANK), row(ROPE_DIM)] + [_const_spec(w.shape) for w in consts],
        out_specs=(row(MLA_HEADS * HEAD_PAD), v_spec),
        out_shape=(jax.ShapeDtypeStruct((n, MLA_HEADS * HEAD_PAD), BF16), v_shape),
        compiler_params=_params(("parallel",)), name="kv_up",
    )(c, kpe, *consts)


_ATT_SCALE = float((NOPE_DIM + ROPE_DIM) ** -0.5)
_ATT_SCALE_LOG2 = float(_ATT_SCALE * np.log2(np.e))


def _merge_heads(outs):
    lane = lax.broadcasted_iota(jnp.int32, outs[0].shape, 1)
    return jnp.where(lane < V_DIM, outs[0], outs[1])


def _attn_prompt_kernel(q_ref, k_ref, vt_ref, o_ref, s_scr, p_scr, m_scr, l_scr, acc_scr, *, tile, nh):
    qi = pl.program_id(2)
    heads = tuple(range(nh))
    pair_rows = [slice((h // 2) * 2 * V_DIM, (h // 2 + 1) * 2 * V_DIM) for h in heads]
    hsl = [slice(h * HEAD_PAD, (h + 1) * HEAD_PAD) for h in heads]

    def scores(j, h):
        k0 = pl.multiple_of(j * tile, tile)
        return _dot_nt(k_ref[0, pl.ds(k0, tile), hsl[h]], q_ref[0, :, hsl[h]]) * _ATT_SCALE_LOG2

    def pv(j, h):
        return _dot(vt_ref[j, pair_rows[h], :], p_scr[h])

    def softmax(h, s, acc_in):
        m_prev = m_scr[h]
        m_new = jnp.maximum(m_prev, jnp.max(s, axis=0, keepdims=True))
        alpha = jnp.exp2(m_prev - m_new)
        p = jnp.exp2(s - m_new)
        l_scr[h] = alpha * l_scr[h] + jnp.sum(p, axis=0, keepdims=True)
        acc_scr[h] = alpha * acc_in
        m_scr[h] = m_new
        p_scr[h] = p.astype(BF16)

    for h in heads:
        s_scr[h] = scores(0, h)
        p_scr[h] = jnp.zeros((tile, tile), BF16)
        m_scr[h] = jnp.full((1, tile), MASK_VALUE, F32)
        l_scr[h] = jnp.zeros((1, tile), F32)
        acc_scr[h] = jnp.zeros((2 * V_DIM, tile), F32)

    def body(j, carry):
        jm1 = jnp.maximum(j - 1, 0)
        accs = [acc_scr[h] + pv(jm1, h) for h in heads]
        s_next = [scores(j + 1, h) for h in heads]
        for h in heads:
            softmax(h, s_scr[h], accs[h])
        for h in heads:
            s_scr[h] = s_next[h]
        return carry

    lax.fori_loop(0, qi, body, 0)

    kc = lax.broadcasted_iota(jnp.int32, (tile, tile), 0) // CHUNK
    qc = lax.broadcasted_iota(jnp.int32, (tile, tile), 1) // CHUNK
    jm1 = jnp.maximum(qi - 1, 0)
    accs = [acc_scr[h] + pv(jm1, h) for h in heads]
    for h in heads:
        softmax(h, jnp.where(kc <= qc, s_scr[h], MASK_VALUE), accs[h])
    outs = [(acc_scr[h] + pv(qi, h)) / l_scr[h] for h in heads]
    vrow = lax.broadcasted_iota(jnp.int32, outs[0].shape, 0)
    merged = [jnp.where(vrow < V_DIM, outs[h], outs[h + 1]) for h in heads[::2]]
    o_ref[0] = jnp.concatenate(merged, axis=0).T.astype(BF16)


def _attn_prompt_call(q, k, vt):
    b, t, _ = q.shape
    tile = ATT_TILE
    nh = ATT_HEADS
    groups = MLA_HEADS // nh
    nk = t // tile
    return pl.pallas_call(
        functools.partial(_attn_prompt_kernel, tile=tile, nh=nh), grid=(b, groups, nk),
        in_specs=[pl.BlockSpec((1, tile, nh * HEAD_PAD), lambda bi, p, i: (bi, i, p)),
                  pl.BlockSpec((1, t, nh * HEAD_PAD), lambda bi, p, i: (bi, 0, p)),
                  pl.BlockSpec((nk, nh * V_DIM, tile), lambda bi, p, i: (bi, p, 0))],
        out_specs=pl.BlockSpec((1, tile, nh * V_DIM), lambda bi, p, i: (bi, i, p)),
        out_shape=jax.ShapeDtypeStruct((b, t, MLA_OUT), BF16),
        scratch_shapes=[pltpu.VMEM((nh, tile, tile), F32), pltpu.VMEM((nh, tile, tile), BF16),
                        pltpu.VMEM((nh, 1, tile), F32), pltpu.VMEM((nh, 1, tile), F32),
                        pltpu.VMEM((nh, 2 * V_DIM, tile), F32)],
        compiler_params=_params(("parallel", "parallel", "arbitrary")), name="attn_prompt",
    )(q, k, vt)


def _attn_sample_kernel(q_ref, kp_ref, vp_ref, kn_ref, vn_ref, o_ref, *, tq, past):
    pad = LANES - tq
    q_chunk = (past + lax.broadcasted_iota(jnp.int32, (tq, 1), 0)) // CHUNK
    pairs = []
    for pr in range(MLA_HEADS // 2):
        vs = slice(pr * 2 * V_DIM, (pr + 1) * 2 * V_DIM)
        vn = jnp.concatenate([vn_ref[0, :, vs], jnp.zeros((pad, 2 * V_DIM), BF16)], axis=0)
        outs = []
        for h in (2 * pr, 2 * pr + 1):
            hs = slice(h * HEAD_PAD, (h + 1) * HEAD_PAD)
            q = q_ref[0, :, hs]
            s_p = _dot_nt(q, kp_ref[0, :, hs]) * _ATT_SCALE
            kc_p = lax.broadcasted_iota(jnp.int32, s_p.shape, 1) // CHUNK
            s_p = jnp.where(kc_p <= q_chunk, s_p, MASK_VALUE)
            kn = jnp.concatenate([kn_ref[0, :, hs], jnp.zeros((pad, HEAD_PAD), BF16)], axis=0)
            s_n = _dot_nt(q, kn) * _ATT_SCALE
            col = lax.broadcasted_iota(jnp.int32, s_n.shape, 1)
            ok = (col < tq) & ((past + col) // CHUNK <= q_chunk)
            s_n = jnp.where(ok, s_n, MASK_VALUE)
            m = jnp.maximum(jnp.max(s_p, axis=-1, keepdims=True), jnp.max(s_n, axis=-1, keepdims=True))
            p_p = jnp.exp(s_p - m)
            p_n = jnp.exp(s_n - m)
            l = jnp.sum(p_p, axis=-1, keepdims=True) + jnp.sum(p_n, axis=-1, keepdims=True)
            acc = _dot(p_p.astype(BF16), vp_ref[0, :, vs]) + _dot(p_n.astype(BF16), vn)
            outs.append(acc / l)
        pairs.append(_merge_heads(outs))
    o_ref[0] = jnp.concatenate(pairs, axis=1).astype(BF16)


def _attn_sample_call(q, k_past, v_past, k_new, v_new):
    b, tq, _ = q.shape
    past = k_past.shape[1]
    blk = lambda rows, w: pl.BlockSpec((1, rows, w), lambda bi: (bi, 0, 0))
    return pl.pallas_call(
        functools.partial(_attn_sample_kernel, tq=tq, past=past), grid=(b,),
        in_specs=[blk(tq, MLA_HEADS * HEAD_PAD), blk(past, MLA_HEADS * HEAD_PAD), blk(past, MLA_OUT),
                  blk(tq, MLA_HEADS * HEAD_PAD), blk(tq, MLA_OUT)],
        out_specs=blk(tq, MLA_OUT),
        out_shape=jax.ShapeDtypeStruct((b, tq, MLA_OUT), BF16),
        compiler_params=_params(("parallel",)), name="attn_sample",
    )(q, k_past, v_past, k_new, v_new)


def _gdn_kernel(x_ref, z_ref, gb_ref, cbuf_ref, s0_ref, cw_ref, nw_ref, gsum_ref, eg_ref, eb_ref, bdm_ref,
                o_ref, sout_ref, s_scr, tail_scr, *, nb, tb, nt):
    t = pl.program_id(1)
    rows = max(tb, CHUNK)

    @pl.when(t == 0)
    def _():
        s_scr[...] = s0_ref[...]
        tail_scr[...] = cbuf_ref[...]

    cw = cw_ref[...]
    gsum = gsum_ref[...]
    bdm = bdm_ref[...]
    bdm_f32 = bdm.astype(F32)

    def bd(m):
        mb = m.astype(BF16)
        return jnp.concatenate([mb] * GROUP, axis=0) * bdm

    ri = lax.broadcasted_iota(jnp.int32, (CHUNK, GROUP_W), 0)
    ci = lax.broadcasted_iota(jnp.int32, (CHUNK, GROUP_W), 1) % CHUNK
    causal = ri >= ci
    strict = ri > ci
    eye = jnp.where(ri == ci, 1.0, 0.0).astype(F32)
    tri = jnp.where(lax.broadcasted_iota(jnp.int32, (CHUNK, CHUNK), 0)
                    >= lax.broadcasted_iota(jnp.int32, (CHUNK, CHUNK), 1), 1.0, 0.0).astype(BF16)

    def same_block(size):
        return (ri // size) == (ci // size)

    def precompute(units):
        def decay_and_products():
            for u in units:
                gcum = _split_dot(u['g'], tri, 3, left=True)
                grow = jnp.sum(jnp.where(ri <= ci, u['g'], 0.0), axis=0, keepdims=True)
                eg = jnp.exp(gcum)
                glast = gcum[CHUNK - 1:CHUNK, :]
                kb = u['k'] * u['b']
                u['qd'] = u['q'] * eg
                u['kd'] = u['k'] * jnp.exp(glast - gcum)
                u['eglast'] = jnp.exp(glast)
                u['vb'] = u['v'] * u['b']
                u['kg'] = kb * eg
                decay = jnp.where(causal, jnp.exp(jnp.where(causal, gcum - grow, 0.0)), 0.0)
                kq = _dot_nt(jnp.concatenate([kb, u['q']], axis=0).astype(BF16), bd(u['k']))
                u['attn'] = kq[CHUNK:] * decay
                u['lower'] = jnp.where(strict, kq[:CHUNK] * decay, 0.0)
                u['m'] = -jnp.where(same_block(INV_BASE), u['lower'], 0.0)
                u['inv'] = eye + u['m']

        def square():
            for u in units:
                u['m'] = _dot(u['m'].astype(BF16), bd(u['m']))

        def square_and_extend():
            for u in units:
                r = _dot(jnp.concatenate([u['m'], u['inv']], axis=0).astype(BF16), bd(u['m']))
                u['m'] = r[:CHUNK]
                u['inv'] = u['inv'] + r[CHUNK:]

        def extend():
            for u in units:
                u['inv'] = u['inv'] + _dot(u['inv'].astype(BF16), bd(u['m']))

        def couple(size):
            def left():
                for u in units:
                    off = jnp.where(same_block(2 * size) & ~same_block(size), u['lower'], 0.0)
                    u['m'] = _dot(off.astype(BF16), bd(u['inv']))

            def right():
                for u in units:
                    u['inv'] = u['inv'] - _dot(u['inv'].astype(BF16), bd(u['m']))
            return [left, right]

        def solve():
            for u in units:
                invb = u['inv'].astype(BF16)
                u['value'] = _dot(invb, bd(u['vb']))
                u['kcum'] = _dot(invb, bd(u['kg']))

        stages = [decay_and_products, square, square_and_extend, extend]
        size = INV_BASE
        while size < CHUNK:
            stages += couple(size)
            size *= 2
        return stages + [solve]

    n_chunks = rows // CHUNK
    units = [[] for _ in range(nb)]

    conv_rows = {}

    def conv_piece(bi, c):
        r0 = c * CHUNK
        r1 = min(r0 + CHUNK, tb)
        if c == 0:
            tail = tail_scr[bi]
            tail_scr[bi] = x_ref[bi, tb - 8:tb, :]
        else:
            tail = x_ref[bi, r0 - 8:r0, :]
        conv_rows[(bi, c)] = _silu(_causal_conv(x_ref[bi, r0:r1, :], tail, cw))

    def front(bi):
        pieces = [conv_rows[(bi, c)] for c in range(n_chunks)]
        y = jnp.concatenate(pieces, axis=0) if n_chunks > 1 else pieces[0]
        gb = gb_ref[bi]
        if rows > tb:
            y = jnp.concatenate([y, jnp.zeros((rows - tb, GDN_CONV_DIM), F32)], axis=0)
            gb = jnp.concatenate([gb, jnp.zeros((rows - tb, LANES), F32)], axis=0)
        q, k, v = y[:, :GDN_QK], y[:, GDN_QK:2 * GDN_QK], y[:, 2 * GDN_QK:]
        qn = q * lax.rsqrt(_split_dot(q * q, gsum, 2) + L2_EPS) * (GDN_DK ** -0.5)
        kn = k * lax.rsqrt(_split_dot(k * k, gsum, 2) + L2_EPS)
        ge = _split_dot(gb, eg_ref[...], 3)
        be = _split_dot(gb, eb_ref[...], 3)
        for c in range(n_chunks):
            rs = slice(c * CHUNK, (c + 1) * CHUNK)
            for p in range(N_GROUPS):
                ls = slice(p * GROUP_W, (p + 1) * GROUP_W)
                units[bi].append(dict(bi=bi, c=c, p=p, q=qn[rs, ls], k=kn[rs, ls], v=v[rs, ls], g=ge[rs, ls],
                                      b=be[rs, ls]))

    state = {}
    outs = {}

    def rec_read(bi, c):
        for u in units[bi]:
            if u['c'] == c:
                s = state.setdefault((bi, u['p']), s_scr[bi, u['p']]) if c == 0 else state[(bi, u['p'])]
                u['r'] = _dot(jnp.concatenate([u['kcum'], u['qd']], axis=0).astype(BF16), s.astype(BF16))

    def rec_update(bi, c):
        for u in units[bi]:
            if u['c'] == c:
                v_new = u['value'] - u['r'][:CHUNK]
                outs[(bi, c, u['p'])] = u['r'][CHUNK:] + _dot(u['attn'].astype(BF16), bd(v_new))
                upd = _dot_tn(u['kd'].astype(BF16), v_new.astype(BF16))
                key = (bi, u['p'])
                state[key] = state[key] * u['eglast'] + upd * bdm_f32

    def finish(bi):
        for p in range(N_GROUPS):
            s_scr[bi, p] = state[(bi, p)]
        o_rows = [jnp.concatenate([outs[(bi, c, p)] for p in range(N_GROUPS)], axis=1) for c in range(n_chunks)]
        o = jnp.concatenate(o_rows, axis=0) if n_chunks > 1 else o_rows[0]
        o = o[:tb]
        ms = _split_dot(o * o, gsum, 2) * (1.0 / GDN_DV)
        o_ref[bi] = (o * lax.rsqrt(ms + NORM_EPS) * nw_ref[...] * _silu(z_ref[bi])).astype(BF16)

    def bind(fn, *args):
        return lambda: fn(*args)

    fronts = [[bind(front, bi)] for bi in range(nb)]
    pres = [precompute(units[bi]) for bi in range(nb)]
    recs = [[f for c in range(n_chunks) for f in (bind(rec_read, bi, c), bind(rec_update, bi, c))]
            + [bind(finish, bi)] for bi in range(nb)]

    def interleave(*seqs):
        seqs = [s for s in seqs if s]
        n = max(len(s) for s in seqs)
        keyed = [((i + 0.5) * n / len(s), k, f) for k, s in enumerate(seqs) for i, f in enumerate(s)]
        return [f for _, _, f in sorted(keyed, key=lambda e: (e[0], e[1]))]

    convs = [[bind(conv_piece, bi, c) for c in range(n_chunks)] for bi in range(nb)]
    program = list(convs[0])
    for bi in range(nb):
        program += fronts[bi] + interleave(pres[bi], convs[bi + 1] if bi + 1 < nb else [])
    program += interleave(*recs)
    for thunk in program:
        thunk()

    @pl.when(t == nt - 1)
    def _():
        sout_ref[...] = s_scr[...]


def _gdn_call(qkv, z, gb, cbuf8, s0_bd, lw, consts):
    b, t, _ = qkv.shape
    tb = min(GDN_TB, t)
    nt = t // tb
    nb = GDN_NB
    blk = lambda w: pl.BlockSpec((nb, tb, w), lambda bi, ti: (bi, ti, 0))
    per_b = lambda shape: pl.BlockSpec((nb,) + shape, lambda bi, ti: (bi,) + (0,) * len(shape))
    cs = [lw['convw'], lw['gnw'], consts['gsum'], consts['eg'], consts['eb'], consts['bdm']]
    return pl.pallas_call(
        functools.partial(_gdn_kernel, nb=nb, tb=tb, nt=nt), grid=(b // nb, nt),
        in_specs=[blk(GDN_CONV_DIM), blk(GDN_VW), blk(LANES), per_b((8, GDN_CONV_DIM)),
                  per_b((N_GROUPS, GROUP_W, GROUP_W))] + [_const_spec(c.shape) for c in cs],
        out_specs=(blk(GDN_VW), per_b((N_GROUPS, GROUP_W, GROUP_W))),
        out_shape=(jax.ShapeDtypeStruct((b, t, GDN_VW), BF16),
                   jax.ShapeDtypeStruct((b, N_GROUPS, GROUP_W, GROUP_W), F32)),
        scratch_shapes=[pltpu.VMEM((nb, N_GROUPS, GROUP_W, GROUP_W), F32),
                        pltpu.VMEM((nb, 8, GDN_CONV_DIM), F32)],
        compiler_params=_params(("parallel", "arbitrary")), name="gdn",
    )(qkv, z, gb, cbuf8, s0_bd, *cs)


def _layer_weights(l, p):
    c1 = GDN_CONV_DIM
    c2 = c1 + GDN_VW
    c3 = c2 + GDN_HEADS
    c4 = c3 + GDN_HEADS
    c5 = c4 + Q_RANK
    c6 = c5 + KV_RANK
    w_in = p['w_in'][l]
    half = ROPE_DIM // 2
    kpe_w = w_in[:, c6:]
    kpe_sw = jnp.concatenate([kpe_w[:, half:], kpe_w[:, :half]], axis=1)
    zcol = lambda n: jnp.zeros((D_MODEL, n), F32)
    wsm = jnp.concatenate([kpe_w, w_in[:, c2:c3], w_in[:, c3:c4], zcol(LANES - B_LANE0 - GDN_HEADS),
                           kpe_sw, zcol(LANES - ROPE_DIM)], axis=1)
    wmain = jnp.concatenate([w_in[:, :c2], w_in[:, c4:c6]], axis=1)
    gvec = jnp.zeros((2, LANES), F32)
    gvec = gvec.at[0, G_LANE0:G_LANE0 + GDN_HEADS].set(p['gdn_a_log'][l])
    gvec = gvec.at[1, G_LANE0:G_LANE0 + GDN_HEADS].set(p['gdn_dt_bias'][l])
    wq = p['w_uq'][l].reshape(Q_RANK, MLA_HEADS, NOPE_DIM + ROPE_DIM)
    nope, rope = wq[..., :NOPE_DIM], wq[..., NOPE_DIM:]
    rope_sw = jnp.concatenate([rope[..., half:], rope[..., :half]], axis=-1)
    zq = lambda n: jnp.zeros((Q_RANK, MLA_HEADS, n), F32)
    tail = HEAD_PAD - NOPE_DIM - ROPE_DIM
    wq_a = jnp.concatenate([nope, rope, zq(tail)], axis=-1).reshape(Q_RANK, MLA_HEADS * HEAD_PAD)
    wq_b = jnp.concatenate([zq(NOPE_DIM), rope_sw, zq(tail)], axis=-1).reshape(Q_RANK, MLA_HEADS * HEAD_PAD)
    wkv = p['w_ukv'][l].reshape(KV_RANK, MLA_HEADS, NOPE_DIM + V_DIM)
    wk = jnp.concatenate([wkv[..., :NOPE_DIM], jnp.zeros((KV_RANK, MLA_HEADS, HEAD_PAD - NOPE_DIM), F32)],
                         axis=-1).reshape(KV_RANK, MLA_HEADS * HEAD_PAD)
    wv = wkv[..., NOPE_DIM:].reshape(KV_RANK, MLA_OUT)
    we = np.zeros((ROPE_DIM, MLA_HEADS, HEAD_PAD), np.float32)
    for r in range(ROPE_DIM):
        we[r, :, NOPE_DIM + r] = 1.0
    bf = lambda a: a.astype(BF16)
    r2 = lambda a: a.reshape(1, -1)
    return dict(
        nf1=r2(p['norm_ffn1'][l]), wg1=bf(p['w_ffn1_gate'][l]), wu1=bf(p['w_ffn1_up'][l]),
        wd1=bf(p['w_ffn1_down'][l]), nmix=r2(p['norm_mix'][l]), wmain=bf(wmain), wsm=bf(wsm), gvec=gvec,
        qnorm=r2(p['mla_q_norm'][l]), kvnorm=r2(p['mla_kv_norm'][l]),
        wq2=bf(jnp.concatenate([wq_a, wq_b], axis=1)), wv=bf(wv), wvt=bf(wv.T),
        wke=bf(jnp.concatenate([wk, jnp.asarray(we.reshape(ROPE_DIM, MLA_HEADS * HEAD_PAD))], axis=0)),
        wo=bf(p['w_out'][l]), nf2=r2(p['norm_ffn2'][l]), wg2=bf(p['w_ffn2_gate'][l]),
        wu2=bf(p['w_ffn2_up'][l]), wd2=bf(p['w_ffn2_down'][l]),
        convw=p['gdn_conv_w'][l], gnw=jnp.tile(p['gdn_norm_w'][l], GDN_HEADS).reshape(1, GDN_VW),
    )


def _gdn_consts():
    lane = np.arange(GDN_QK)
    gsum = (lane[:, None] // GDN_DK == lane[None, :] // GDN_DK).astype(np.float32)
    eg = np.zeros((LANES, GDN_QK), np.float32)
    eb = np.zeros((LANES, GDN_QK), np.float32)
    for h in range(GDN_HEADS):
        eg[G_LANE0 + h, h * GDN_DK:(h + 1) * GDN_DK] = 1.0
        eb[B_LANE0 + h, h * GDN_DK:(h + 1) * GDN_DK] = 1.0
    gl = np.arange(GROUP_W)
    bdm = (gl[:, None] // GDN_DK == gl[None, :] // GDN_DK).astype(np.float32)
    return dict(gsum=jnp.asarray(gsum, BF16), eg=jnp.asarray(eg, BF16), eb=jnp.asarray(eb, BF16),
                bdm=jnp.asarray(bdm, BF16))


def _rotary_table(pos, rows):
    half = ROPE_DIM // 2
    inv_freq = 1.0 / (ROPE_THETA ** (jnp.arange(half, dtype=F32) / half))
    ang = pos.astype(F32)[:, None] * inv_freq[None, :]
    cos, sin = jnp.cos(ang), jnp.sin(ang)
    t = pos.shape[0]
    cos2 = jnp.concatenate([cos, cos], axis=1)
    sin2 = jnp.concatenate([-sin, sin], axis=1)
    z = lambda n: jnp.zeros((t, n), F32)
    tail = HEAD_PAD - NOPE_DIM - ROPE_DIM
    tab = jnp.concatenate([jnp.ones((t, NOPE_DIM), F32), cos2, z(tail), z(NOPE_DIM), sin2, z(tail),
                           cos2, z(LANES - ROPE_DIM), sin2, z(LANES - ROPE_DIM)], axis=1)
    return jnp.tile(tab, (rows // t, 1)) if rows > t else tab


def _state_to_blockdiag(s):
    b = s.shape[0]
    s = s.reshape(b, N_GROUPS, GROUP, GDN_DK, GDN_DV)
    eye = jnp.eye(GROUP, dtype=s.dtype)
    bdiag = jnp.einsum('bghij,hk->bghikj', s, eye)
    return bdiag.reshape(b, N_GROUPS, GROUP_W, GROUP_W)


def _blockdiag_to_state(sb):
    b = sb.shape[0]
    s6 = sb.reshape(b, N_GROUPS, GROUP, GDN_DK, GROUP, GDN_DV)
    heads = [s6[:, :, h, :, h, :] for h in range(GROUP)]
    return jnp.stack(heads, axis=2).reshape(b, GDN_HEADS, GDN_DK, GDN_DV)


def _run_trunk(x, pos, past_ckv, past_kpe, s_gdn, s_conv, weights, consts):
    b, t, _ = x.shape
    n = b * t
    rot = _rotary_table(pos, max(t, min(TOK_TILE, n)))
    xf = x.reshape(n, D_MODEL)
    ckv_rows, kpe_rows, states, conv_bufs = [], [], [], []
    depth = len(weights)
    for l, lw in enumerate(weights):
        if s_gdn is None:
            cbuf8 = jnp.zeros((b, 8, GDN_CONV_DIM), F32)
            s0_bd = jnp.zeros((b, N_GROUPS, GROUP_W, GROUP_W), F32)
        else:
            cbuf8 = jnp.pad(s_conv[l], ((0, 0), (8 - (CONV_W - 1), 0), (0, 0)))
            s0_bd = _state_to_blockdiag(s_gdn[l])
        xf, qkv, z, q, c_new, kpe_new, gb = _pre_call(xf, lw, rot, t)
        k_new, v_new = _kv_call(c_new, kpe_new, lw, v_transposed=past_ckv is None)
        q3 = q.reshape(b, t, -1)
        k3 = k_new.reshape(b, t, -1)
        if past_ckv is None:
            m_out = _attn_prompt_call(q3, k3, v_new)
        else:
            pl_len = past_ckv.shape[2]
            k_past, v_past = _kv_call(past_ckv[l].reshape(b * pl_len, KV_RANK),
                                      past_kpe[l].reshape(b * pl_len, ROPE_DIM), lw, v_transposed=False)
            m_out = _attn_sample_call(q3, k_past.reshape(b, pl_len, -1), v_past.reshape(b, pl_len, -1), k3,
                                      v_new.reshape(b, t, -1))
        qkv3 = qkv.reshape(b, t, GDN_CONV_DIM)
        g_out, s_bd = _gdn_call(qkv3, z.reshape(b, t, GDN_VW), gb.reshape(b, t, LANES), cbuf8, s0_bd, lw,
                                consts)
        final = l == depth - 1
        xf = _post_call(xf, g_out.reshape(n, GDN_VW), m_out.reshape(n, MLA_OUT), lw,
                        weights[-1]['nfinal'], final)
        ckv_rows.append(c_new.reshape(b, t, KV_RANK))
        kpe_rows.append(kpe_new.reshape(b, t, ROPE_DIM))
        states.append(_blockdiag_to_state(s_bd))
        conv_bufs.append(qkv3[:, t - (CONV_W - 1):, :])
    return (xf.reshape(b, t, D_MODEL), jnp.stack(ckv_rows), jnp.stack(kpe_rows), jnp.stack(states),
            jnp.stack(conv_bufs))


def kernel(x_prompt, x_sample, cache_mla_ckv, cache_mla_krope, state_gdn, state_gdn_conv,
           norm_ffn1, w_ffn1_gate, w_ffn1_up, w_ffn1_down, norm_mix, w_in, gdn_conv_w, gdn_a_log,
           gdn_dt_bias, gdn_norm_w, mla_q_norm, mla_kv_norm, w_uq, w_ukv, w_out, norm_ffn2,
           w_ffn2_gate, w_ffn2_up, w_ffn2_down, norm_final):
    params = dict(norm_ffn1=norm_ffn1, w_ffn1_gate=w_ffn1_gate, w_ffn1_up=w_ffn1_up, w_ffn1_down=w_ffn1_down,
                  norm_mix=norm_mix, w_in=w_in, gdn_conv_w=gdn_conv_w, gdn_a_log=gdn_a_log,
                  gdn_dt_bias=gdn_dt_bias, gdn_norm_w=gdn_norm_w, mla_q_norm=mla_q_norm,
                  mla_kv_norm=mla_kv_norm, w_uq=w_uq, w_ukv=w_ukv, w_out=w_out, norm_ffn2=norm_ffn2,
                  w_ffn2_gate=w_ffn2_gate, w_ffn2_up=w_ffn2_up, w_ffn2_down=w_ffn2_down)
    depth = w_in.shape[0]
    weights = [_layer_weights(l, params) for l in range(depth)]
    weights[-1]['nfinal'] = norm_final.reshape(1, D_MODEL)
    consts = _gdn_consts()
    t_p = x_prompt.shape[1]
    out_p = _run_trunk(x_prompt, jnp.arange(t_p, dtype=jnp.int32), None, None, None, None, weights, consts)
    pos_s = cache_mla_ckv.shape[2] + jnp.arange(x_sample.shape[1], dtype=jnp.int32)
    out_s = _run_trunk(x_sample, pos_s, cache_mla_ckv, cache_mla_krope, state_gdn, state_gdn_conv,
                       weights, consts)
    return (out_p[0], out_s[0]) + out_p[1:] + out_s[1:]
```

```python
import functools

import numpy as np
import jax
import jax.numpy as jnp
from jax import lax
from jax.experimental import pallas as pl
from jax.experimental.pallas import tpu as pltpu

F32 = jnp.float32
BF16 = jnp.bfloat16

D_MODEL = 1024
D_FF = 2816
NORM_EPS = 1e-6
L2_EPS = 1e-6
CHUNK = 64
GDN_HEADS = 8
GDN_DK = 64
GDN_DV = 64
CONV_W = 4
GDN_QK = GDN_HEADS * GDN_DK
GDN_VW = GDN_HEADS * GDN_DV
GDN_CONV_DIM = 2 * GDN_QK + GDN_VW
MLA_HEADS = 8
Q_RANK = 256
KV_RANK = 128
NOPE_DIM = 64
ROPE_DIM = 32
V_DIM = 64
ROPE_THETA = 10000.0
MLA_OUT = MLA_HEADS * V_DIM
MASK_VALUE = -1e30

LANES = 128
HEAD_PAD = 128
MAIN_COLS = GDN_CONV_DIM + GDN_VW + Q_RANK + KV_RANK
FF_CHUNK = 512
GROUP = 2
GROUP_W = GROUP * GDN_DK
N_GROUPS = GDN_HEADS // GROUP
G_LANE0 = 32
B_LANE0 = 40
VMEM_LIMIT = 56 * 1024 * 1024
TOK_TILE = 512
POST_TILE = 512
KV_TILE = 1024
ATT_TILE = 256
ATT_HEADS = 8
GDN_TB = 256
GDN_NB = 2
INV_BASE = 8


def _dot(a, b):
    return jnp.dot(a, b, preferred_element_type=F32)


def _dot_nt(a, b):
    return lax.dot_general(a, b, (((1,), (1,)), ((), ())), preferred_element_type=F32)


def _dot_tn(a, b):
    return lax.dot_general(a, b, (((0,), (0,)), ((), ())), preferred_element_type=F32)


def _split_dot(x, w01, passes, left=False):
    acc = None
    r = x
    for i in range(passes):
        part = r.astype(BF16)
        d = _dot(w01, part) if left else _dot(part, w01)
        acc = d if acc is None else acc + d
        if i + 1 < passes:
            r = r - part.astype(F32)
    return acc


def _rms(x, w):
    return x * lax.rsqrt(jnp.mean(x * x, axis=-1, keepdims=True) + NORM_EPS) * w


def _silu(x):
    return x * jax.nn.sigmoid(x)


def _ffn_residual(x, nw, wg_ref, wu_ref, wd_ref):
    xn = _rms(x, nw).astype(BF16)
    acc = None
    for c0 in range(0, D_FF, FF_CHUNK):
        c1 = min(c0 + FF_CHUNK, D_FF)
        g = _dot(xn, wg_ref[:, c0:c1])
        u = _dot(xn, wu_ref[:, c0:c1])
        hid = (_silu(g) * u).astype(BF16)
        d = _dot(hid, wd_ref[c0:c1, :])
        acc = d if acc is None else acc + d
    return x + 0.5 * acc


def _causal_conv(x, tail, cw):
    row8 = lax.broadcasted_iota(jnp.int32, (8, x.shape[1]), 0)
    conv = x * cw[CONV_W - 1:CONV_W, :]
    for sh in range(1, CONV_W):
        xr = pltpu.roll(x, sh, axis=0)
        head = jnp.where(row8 < sh, pltpu.roll(tail, sh, axis=0), xr[:8])
        conv = conv + jnp.concatenate([head, xr[8:]], axis=0) * cw[CONV_W - 1 - sh:CONV_W - sh, :]
    return conv


def _pre_kernel(x_ref, nf_ref, wg_ref, wu_ref, wd_ref, nm_ref, wmain_ref, wsm_ref, gvec_ref,
                qn_ref, kvn_ref, wq2_ref, rot_ref,
                xo_ref, qkv_ref, z_ref, q_ref, c_ref, kpe_ref, gb_ref):
    x1 = _ffn_residual(x_ref[...], nf_ref[...], wg_ref, wu_ref, wd_ref)
    xo_ref[...] = x1
    h = _rms(x1, nm_ref[...]).astype(BF16)
    qkv_ref[...] = _dot(h, wmain_ref[:, :GDN_CONV_DIM])
    c0 = GDN_CONV_DIM + GDN_VW
    z_ref[...] = _dot(h, wmain_ref[:, GDN_CONV_DIM:c0])
    lat = _dot(h, wmain_ref[:, c0:])
    cq = lat[:, :Q_RANK]
    ckv = lat[:, Q_RANK:]
    rot = rot_ref[...]
    cosq, sinq = rot[:, :LANES], rot[:, LANES:2 * LANES]
    cosk, sink = rot[:, 2 * LANES:3 * LANES], rot[:, 3 * LANES:]
    cqn = _rms(cq, qn_ref[...]).astype(BF16)
    nq = MLA_HEADS * HEAD_PAD
    cos8 = jnp.concatenate([cosq] * MLA_HEADS, axis=1)
    sin8 = jnp.concatenate([sinq] * MLA_HEADS, axis=1)
    q_ref[...] = (_dot(cqn, wq2_ref[:, :nq]) * cos8 + _dot(cqn, wq2_ref[:, nq:]) * sin8).astype(BF16)
    c_ref[...] = _rms(ckv, kvn_ref[...])
    small = _dot(h, wsm_ref[...])
    s1, s2 = small[:, :LANES], small[:, LANES:]
    kpe_ref[...] = (s1 * cosk + s2 * sink)[:, :ROPE_DIM]
    gvec = gvec_ref[...]
    xa = s1 + gvec[1:2, :]
    softplus = jnp.maximum(xa, 0.0) + jnp.log1p(jnp.exp(-jnp.abs(xa)))
    gfull = -jnp.exp(gvec[0:1, :]) * softplus
    bfull = jax.nn.sigmoid(s1)
    lane = lax.broadcasted_iota(jnp.int32, s1.shape, 1)
    is_g = (lane >= G_LANE0) & (lane < G_LANE0 + GDN_HEADS)
    is_b = (lane >= B_LANE0) & (lane < B_LANE0 + GDN_HEADS)
    gb_ref[...] = jnp.where(is_g, gfull, jnp.where(is_b, bfull, 0.0))


def _post_kernel(x_ref, g_ref, m_ref, wo_ref, nf_ref, wg_ref, wu_ref, wd_ref, nfin_ref, y_ref, *, final):
    x = x_ref[...] + _dot(g_ref[...], wo_ref[:GDN_VW, :]) + _dot(m_ref[...], wo_ref[GDN_VW:, :])
    x2 = _ffn_residual(x, nf_ref[...], wg_ref, wu_ref, wd_ref)
    y_ref[...] = _rms(x2, nfin_ref[...]) if final else x2


def _const_spec(shape):
    nd = len(shape)
    return pl.BlockSpec(shape, lambda *_: (0,) * nd, pipeline_mode=pl.Buffered(1))


def _params(sem):
    return pltpu.CompilerParams(dimension_semantics=sem, vmem_limit_bytes=VMEM_LIMIT)


def _pre_call(x, lw, rot, seq_len):
    n = x.shape[0]
    tm = min(TOK_TILE, n)
    rot_blocks = rot.shape[0] // tm
    row = lambda w: pl.BlockSpec((tm, w), lambda i: (i, 0))
    consts = [lw['nf1'], lw['wg1'], lw['wu1'], lw['wd1'], lw['nmix'], lw['wmain'], lw['wsm'], lw['gvec'],
              lw['qnorm'], lw['kvnorm'], lw['wq2']]
    in_specs = [row(D_MODEL)] + [_const_spec(c.shape) for c in consts]
    in_specs.append(pl.BlockSpec((tm, 4 * LANES), lambda i: (i % rot_blocks, 0)))
    out_shape = (
        jax.ShapeDtypeStruct((n, D_MODEL), F32),
        jax.ShapeDtypeStruct((n, GDN_CONV_DIM), F32),
        jax.ShapeDtypeStruct((n, GDN_VW), F32),
        jax.ShapeDtypeStruct((n, MLA_HEADS * HEAD_PAD), BF16),
        jax.ShapeDtypeStruct((n, KV_RANK), F32),
        jax.ShapeDtypeStruct((n, ROPE_DIM), F32),
        jax.ShapeDtypeStruct((n, LANES), F32),
    )
    out_specs = (row(D_MODEL), row(GDN_CONV_DIM), row(GDN_VW), row(MLA_HEADS * HEAD_PAD), row(KV_RANK),
                 row(ROPE_DIM), row(LANES))
    return pl.pallas_call(
        _pre_kernel, grid=(n // tm,), in_specs=in_specs, out_specs=out_specs, out_shape=out_shape,
        compiler_params=_params(("parallel",)), name="ffn_proj",
    )(x, *consts, rot)


def _post_call(x, g_out, m_out, lw, nfin, final):
    n = x.shape[0]
    tm = min(POST_TILE, n)
    row = lambda w: pl.BlockSpec((tm, w), lambda i: (i, 0))
    consts = [lw['wo'], lw['nf2'], lw['wg2'], lw['wu2'], lw['wd2'], nfin]
    in_specs = [row(D_MODEL), row(GDN_VW), row(MLA_OUT)] + [_const_spec(c.shape) for c in consts]
    return pl.pallas_call(
        functools.partial(_post_kernel, final=final), grid=(n // tm,), in_specs=in_specs,
        out_specs=row(D_MODEL), out_shape=jax.ShapeDtypeStruct((n, D_MODEL), F32),
        compiler_params=_params(("parallel",)), name="mix_ffn",
    )(x, g_out, m_out, *consts)


def _kv_kernel(c_ref, kpe_ref, wk_ref, wv_ref, k_ref, v_ref, *, v_transposed):
    c = c_ref[...].astype(BF16)
    ck = jnp.concatenate([c, kpe_ref[...].astype(BF16)], axis=1)
    k_ref[...] = _dot(ck, wk_ref[...]).astype(BF16)
    if v_transposed:
        for i in range(v_ref.shape[0]):
            v_ref[i] = _dot_nt(wv_ref[...], c[i * ATT_TILE:(i + 1) * ATT_TILE]).astype(BF16)
    else:
        v_ref[...] = _dot(c, wv_ref[...]).astype(BF16)


def _kv_call(c, kpe, lw, v_transposed):
    n = c.shape[0]
    tm = min(KV_TILE, n)
    row = lambda w: pl.BlockSpec((tm, w), lambda i: (i, 0))
    consts = [lw['wke'], lw['wvt'] if v_transposed else lw['wv']]
    if v_transposed:
        v_spec = pl.BlockSpec((tm // ATT_TILE, MLA_OUT, ATT_TILE), lambda i: (i, 0, 0))
        v_shape = jax.ShapeDtypeStruct((n // ATT_TILE, MLA_OUT, ATT_TILE), BF16)
    else:
        v_spec = row(MLA_OUT)
        v_shape = jax.ShapeDtypeStruct((n, MLA_OUT), BF16)
    return pl.pallas_call(
        functools.partial(_kv_kernel, v_transposed=v_transposed), grid=(n // tm,),
        in_specs=[row(KV_RANK), row(ROPE_DIM)] + [_const_spec(w.shape) for w in consts],
        out_specs=(row(MLA_HEADS * HEAD_PAD), v_spec),
        out_shape=(jax.ShapeDtypeStruct((n, MLA_HEADS * HEAD_PAD), BF16), v_shape),
        compiler_params=_params(("parallel",)), name="kv_up",
    )(c, kpe, *consts)


_ATT_SCALE = float((NOPE_DIM + ROPE_DIM) ** -0.5)
_ATT_SCALE_LOG2 = float(_ATT_SCALE * np.log2(np.e))


def _merge_heads(outs):
    lane = lax.broadcasted_iota(jnp.int32, outs[0].shape, 1)
    return jnp.where(lane < V_DIM, outs[0], outs[1])


def _attn_prompt_kernel(q_ref, k_ref, vt_ref, o_ref, s_scr, p_scr, m_scr, l_scr, acc_scr, *, tile, nh):
    qi = pl.program_id(2)
    heads = tuple(range(nh))
    pair_rows = [slice((h // 2) * 2 * V_DIM, (h // 2 + 1) * 2 * V_DIM) for h in heads]
    hsl = [slice(h * HEAD_PAD, (h + 1) * HEAD_PAD) for h in heads]

    def scores(j, h):
        k0 = pl.multiple_of(j * tile, tile)
        return _dot_nt(k_ref[0, pl.ds(k0, tile), hsl[h]], q_ref[0, :, hsl[h]]) * _ATT_SCALE_LOG2

    def pv(j, h):
        return _dot(vt_ref[j, pair_rows[h], :], p_scr[h])

    def softmax(h, s, acc_in):
        m_prev = m_scr[h]
        m_new = jnp.maximum(m_prev, jnp.max(s, axis=0, keepdims=True))
        alpha = jnp.exp2(m_prev - m_new)
        p = jnp.exp2(s - m_new)
        l_scr[h] = alpha * l_scr[h] + jnp.sum(p, axis=0, keepdims=True)
        acc_scr[h] = alpha * acc_in
        m_scr[h] = m_new
        p_scr[h] = p.astype(BF16)

    for h in heads:
        s_scr[h] = scores(0, h)
        p_scr[h] = jnp.zeros((tile, tile), BF16)
        m_scr[h] = jnp.full((1, tile), MASK_VALUE, F32)
        l_scr[h] = jnp.zeros((1, tile), F32)
        acc_scr[h] = jnp.zeros((2 * V_DIM, tile), F32)

    def body(j, carry):
        jm1 = jnp.maximum(j - 1, 0)
        accs = [acc_scr[h] + pv(jm1, h) for h in heads]
        s_next = [scores(j + 1, h) for h in heads]
        for h in heads:
            softmax(h, s_scr[h], accs[h])
        for h in heads:
            s_scr[h] = s_next[h]
        return carry

    lax.fori_loop(0, qi, body, 0)

    kc = lax.broadcasted_iota(jnp.int32, (tile, tile), 0) // CHUNK
    qc = lax.broadcasted_iota(jnp.int32, (tile, tile), 1) // CHUNK
    jm1 = jnp.maximum(qi - 1, 0)
    accs = [acc_scr[h] + pv(jm1, h) for h in heads]
    for h in heads:
        softmax(h, jnp.where(kc <= qc, s_scr[h], MASK_VALUE), accs[h])
    outs = [(acc_scr[h] + pv(qi, h)) / l_scr[h] for h in heads]
    vrow = lax.broadcasted_iota(jnp.int32, outs[0].shape, 0)
    merged = [jnp.where(vrow < V_DIM, outs[h], outs[h + 1]) for h in heads[::2]]
    o_ref[0] = jnp.concatenate(merged, axis=0).T.astype(BF16)


def _attn_prompt_call(q, k, vt):
    b, t, _ = q.shape
    tile = ATT_TILE
    nh = ATT_HEADS
    groups = MLA_HEADS // nh
    nk = t // tile
    return pl.pallas_call(
        functools.partial(_attn_prompt_kernel, tile=tile, nh=nh), grid=(b, groups, nk),
        in_specs=[pl.BlockSpec((1, tile, nh * HEAD_PAD), lambda bi, p, i: (bi, i, p)),
                  pl.BlockSpec((1, t, nh * HEAD_PAD), lambda bi, p, i: (bi, 0, p)),
                  pl.BlockSpec((nk, nh * V_DIM, tile), lambda bi, p, i: (bi, p, 0))],
        out_specs=pl.BlockSpec((1, tile, nh * V_DIM), lambda bi, p, i: (bi, i, p)),
        out_shape=jax.ShapeDtypeStruct((b, t, MLA_OUT), BF16),
        scratch_shapes=[pltpu.VMEM((nh, tile, tile), F32), pltpu.VMEM((nh, tile, tile), BF16),
                        pltpu.VMEM((nh, 1, tile), F32), pltpu.VMEM((nh, 1, tile), F32),
                        pltpu.VMEM((nh, 2 * V_DIM, tile), F32)],
        compiler_params=_params(("parallel", "parallel", "arbitrary")), name="attn_prompt",
    )(q, k, vt)


def _attn_sample_kernel(q_ref, kp_ref, vp_ref, kn_ref, vn_ref, o_ref, *, tq, past):
    pad = LANES - tq
    q_chunk = (past + lax.broadcasted_iota(jnp.int32, (tq, 1), 0)) // CHUNK
    pairs = []
    for pr in range(MLA_HEADS // 2):
        vs = slice(pr * 2 * V_DIM, (pr + 1) * 2 * V_DIM)
        vn = jnp.concatenate([vn_ref[0, :, vs], jnp.zeros((pad, 2 * V_DIM), BF16)], axis=0)
        outs = []
        for h in (2 * pr, 2 * pr + 1):
            hs = slice(h * HEAD_PAD, (h + 1) * HEAD_PAD)
            q = q_ref[0, :, hs]
            s_p = _dot_nt(q, kp_ref[0, :, hs]) * _ATT_SCALE
            kc_p = lax.broadcasted_iota(jnp.int32, s_p.shape, 1) // CHUNK
            s_p = jnp.where(kc_p <= q_chunk, s_p, MASK_VALUE)
            kn = jnp.concatenate([kn_ref[0, :, hs], jnp.zeros((pad, HEAD_PAD), BF16)], axis=0)
            s_n = _dot_nt(q, kn) * _ATT_SCALE
            col = lax.broadcasted_iota(jnp.int32, s_n.shape, 1)
            ok = (col < tq) & ((past + col) // CHUNK <= q_chunk)
            s_n = jnp.where(ok, s_n, MASK_VALUE)
            m = jnp.maximum(jnp.max(s_p, axis=-1, keepdims=True), jnp.max(s_n, axis=-1, keepdims=True))
            p_p = jnp.exp(s_p - m)
            p_n = jnp.exp(s_n - m)
            l = jnp.sum(p_p, axis=-1, keepdims=True) + jnp.sum(p_n, axis=-1, keepdims=True)
            acc = _dot(p_p.astype(BF16), vp_ref[0, :, vs]) + _dot(p_n.astype(BF16), vn)
            outs.append(acc / l)
        pairs.append(_merge_heads(outs))
    o_ref[0] = jnp.concatenate(pairs, axis=1).astype(BF16)


def _attn_sample_call(q, k_past, v_past, k_new, v_new):
    b, tq, _ = q.shape
    past = k_past.shape[1]
    blk = lambda rows, w: pl.BlockSpec((1, rows, w), lambda bi: (bi, 0, 0))
    return pl.pallas_call(
        functools.partial(_attn_sample_kernel, tq=tq, past=past), grid=(b,),
        in_specs=[blk(tq, MLA_HEADS * HEAD_PAD), blk(past, MLA_HEADS * HEAD_PAD), blk(past, MLA_OUT),
                  blk(tq, MLA_HEADS * HEAD_PAD), blk(tq, MLA_OUT)],
        out_specs=blk(tq, MLA_OUT),
        out_shape=jax.ShapeDtypeStruct((b, tq, MLA_OUT), BF16),
        compiler_params=_params(("parallel",)), name="attn_sample",
    )(q, k_past, v_past, k_new, v_new)


def _gdn_kernel(x_ref, z_ref, gb_ref, cbuf_ref, s0_ref, cw_ref, nw_ref, gsum_ref, eg_ref, eb_ref, bdm_ref,
                o_ref, sout_ref, s_scr, tail_scr, *, nb, tb, nt):
    t = pl.program_id(1)
    rows = max(tb, CHUNK)

    @pl.when(t == 0)
    def _():
        s_scr[...] = s0_ref[...]
        tail_scr[...] = cbuf_ref[...]

    cw = cw_ref[...]
    gsum = gsum_ref[...]
    bdm = bdm_ref[...]
    bdm_f32 = bdm.astype(F32)

    def bd(m):
        mb = m.astype(BF16)
        return jnp.concatenate([mb] * GROUP, axis=0) * bdm

    ri = lax.broadcasted_iota(jnp.int32, (CHUNK, GROUP_W), 0)
    ci = lax.broadcasted_iota(jnp.int32, (CHUNK, GROUP_W), 1) % CHUNK
    causal = ri >= ci
    strict = ri > ci
    eye = jnp.where(ri == ci, 1.0, 0.0).astype(F32)
    tri = jnp.where(lax.broadcasted_iota(jnp.int32, (CHUNK, CHUNK), 0)
                    >= lax.broadcasted_iota(jnp.int32, (CHUNK, CHUNK), 1), 1.0, 0.0).astype(BF16)

    def same_block(size):
        return (ri // size) == (ci // size)

    def precompute(units):
        def decay_and_products():
            for u in units:
                gcum = _split_dot(u['g'], tri, 3, left=True)
                grow = jnp.sum(jnp.where(ri <= ci, u['g'], 0.0), axis=0, keepdims=True)
                eg = jnp.exp(gcum)
                glast = gcum[CHUNK - 1:CHUNK, :]
                kb = u['k'] * u['b']
                u['qd'] = u['q'] * eg
                u['kd'] = u['k'] * jnp.exp(glast - gcum)
                u['eglast'] = jnp.exp(glast)
                u['vb'] = u['v'] * u['b']
                u['kg'] = kb * eg
                decay = jnp.where(causal, jnp.exp(jnp.where(causal, gcum - grow, 0.0)), 0.0)
                kq = _dot_nt(jnp.concatenate([kb, u['q']], axis=0).astype(BF16), bd(u['k']))
                u['attn'] = kq[CHUNK:] * decay
                u['lower'] = jnp.where(strict, kq[:CHUNK] * decay, 0.0)
                u['m'] = -jnp.where(same_block(INV_BASE), u['lower'], 0.0)
                u['inv'] = eye + u['m']

        def square():
            for u in units:
                u['m'] = _dot(u['m'].astype(BF16), bd(u['m']))

        def square_and_extend():
            for u in units:
                r = _dot(jnp.concatenate([u['m'], u['inv']], axis=0).astype(BF16), bd(u['m']))
                u['m'] = r[:CHUNK]
                u['inv'] = u['inv'] + r[CHUNK:]

        def extend():
            for u in units:
                u['inv'] = u['inv'] + _dot(u['inv'].astype(BF16), bd(u['m']))

        def couple(size):
            def left():
                for u in units:
                    off = jnp.where(same_block(2 * size) & ~same_block(size), u['lower'], 0.0)
                    u['m'] = _dot(off.astype(BF16), bd(u['inv']))

            def right():
                for u in units:
                    u['inv'] = u['inv'] - _dot(u['inv'].astype(BF16), bd(u['m']))
            return [left, right]

        def solve():
            for u in units:
                invb = u['inv'].astype(BF16)
                u['value'] = _dot(invb, bd(u['vb']))
                u['kcum'] = _dot(invb, bd(u['kg']))

        stages = [decay_and_products, square, square_and_extend, extend]
        size = INV_BASE
        while size < CHUNK:
            stages += couple(size)
            size *= 2
        return stages + [solve]

    n_chunks = rows // CHUNK
    units = [[] for _ in range(nb)]

    conv_rows = {}

    def conv_piece(bi, c):
        r0 = c * CHUNK
        r1 = min(r0 + CHUNK, tb)
        if c == 0:
            tail = tail_scr[bi]
            tail_scr[bi] = x_ref[bi, tb - 8:tb, :]
        else:
            tail = x_ref[bi, r0 - 8:r0, :]
        conv_rows[(bi, c)] = _silu(_causal_conv(x_ref[bi, r0:r1, :], tail, cw))

    def front(bi):
        pieces = [conv_rows[(bi, c)] for c in range(n_chunks)]
        y = jnp.concatenate(pieces, axis=0) if n_chunks > 1 else pieces[0]
        gb = gb_ref[bi]
        if rows > tb:
            y = jnp.concatenate([y, jnp.zeros((rows - tb, GDN_CONV_DIM), F32)], axis=0)
            gb = jnp.concatenate([gb, jnp.zeros((rows - tb, LANES), F32)], axis=0)
        q, k, v = y[:, :GDN_QK], y[:, GDN_QK:2 * GDN_QK], y[:, 2 * GDN_QK:]
        qn = q * lax.rsqrt(_split_dot(q * q, gsum, 2) + L2_EPS) * (GDN_DK ** -0.5)
        kn = k * lax.rsqrt(_split_dot(k * k, gsum, 2) + L2_EPS)
        ge = _split_dot(gb, eg_ref[...], 3)
        be = _split_dot(gb, eb_ref[...], 3)
        for c in range(n_chunks):
            rs = slice(c * CHUNK, (c + 1) * CHUNK)
            for p in range(N_GROUPS):
                ls = slice(p * GROUP_W, (p + 1) * GROUP_W)
                units[bi].append(dict(bi=bi, c=c, p=p, q=qn[rs, ls], k=kn[rs, ls], v=v[rs, ls], g=ge[rs, ls],
                                      b=be[rs, ls]))

    state = {}
    outs = {}

    def rec_read(bi, c):
        for u in units[bi]:
            if u['c'] == c:
                s = state.setdefault((bi, u['p']), s_scr[bi, u['p']]) if c == 0 else state[(bi, u['p'])]
                u['r'] = _dot(jnp.concatenate([u['kcum'], u['qd']], axis=0).astype(BF16), s.astype(BF16))

    def rec_update(bi, c):
        for u in units[bi]:
            if u['c'] == c:
                v_new = u['value'] - u['r'][:CHUNK]
                outs[(bi, c, u['p'])] = u['r'][CHUNK:] + _dot(u['attn'].astype(BF16), bd(v_new))
                upd = _dot_tn(u['kd'].astype(BF16), v_new.astype(BF16))
                key = (bi, u['p'])
                state[key] = state[key] * u['eglast'] + upd * bdm_f32

    def finish(bi):
        for p in range(N_GROUPS):
            s_scr[bi, p] = state[(bi, p)]
        o_rows = [jnp.concatenate([outs[(bi, c, p)] for p in range(N_GROUPS)], axis=1) for c in range(n_chunks)]
        o = jnp.concatenate(o_rows, axis=0) if n_chunks > 1 else o_rows[0]
        o = o[:tb]
        ms = _split_dot(o * o, gsum, 2) * (1.0 / GDN_DV)
        o_ref[bi] = (o * lax.rsqrt(ms + NORM_EPS) * nw_ref[...] * _silu(z_ref[bi])).astype(BF16)

    def bind(fn, *args):
        return lambda: fn(*args)

    fronts = [[bind(front, bi)] for bi in range(nb)]
    pres = [precompute(units[bi]) for bi in range(nb)]
    recs = [[f for c in range(n_chunks) for f in (bind(rec_read, bi, c), bind(rec_update, bi, c))]
            + [bind(finish, bi)] for bi in range(nb)]

    def interleave(*seqs):
        seqs = [s for s in seqs if s]
        n = max(len(s) for s in seqs)
        keyed = [((i + 0.5) * n / len(s), k, f) for k, s in enumerate(seqs) for i, f in enumerate(s)]
        return [f for _, _, f in sorted(keyed, key=lambda e: (e[0], e[1]))]

    program = []
    for bi in range(nb):
        program += [bind(conv_piece, bi, c) for c in range(n_chunks)] + fronts[bi] + pres[bi]
    program += interleave(*recs)
    for thunk in program:
        thunk()

    @pl.when(t == nt - 1)
    def _():
        sout_ref[...] = s_scr[...]


def _gdn_call(qkv, z, gb, cbuf8, s0_bd, lw, consts):
    b, t, _ = qkv.shape
    tb = min(GDN_TB, t)
    nt = t // tb
    nb = GDN_NB
    blk = lambda w: pl.BlockSpec((nb, tb, w), lambda bi, ti: (bi, ti, 0))
    per_b = lambda shape: pl.BlockSpec((nb,) + shape, lambda bi, ti: (bi,) + (0,) * len(shape))
    cs = [lw['convw'], lw['gnw'], consts['gsum'], consts['eg'], consts['eb'], consts['bdm']]
    return pl.pallas_call(
        functools.partial(_gdn_kernel, nb=nb, tb=tb, nt=nt), grid=(b // nb, nt),
        in_specs=[blk(GDN_CONV_DIM), blk(GDN_VW), blk(LANES), per_b((8, GDN_CONV_DIM)),
                  per_b((N_GROUPS, GROUP_W, GROUP_W))] + [_const_spec(c.shape) for c in cs],
        out_specs=(blk(GDN_VW), per_b((N_GROUPS, GROUP_W, GROUP_W))),
        out_shape=(jax.ShapeDtypeStruct((b, t, GDN_VW), BF16),
                   jax.ShapeDtypeStruct((b, N_GROUPS, GROUP_W, GROUP_W), F32)),
        scratch_shapes=[pltpu.VMEM((nb, N_GROUPS, GROUP_W, GROUP_W), F32),
                        pltpu.VMEM((nb, 8, GDN_CONV_DIM), F32)],
        compiler_params=_params(("parallel", "arbitrary")), name="gdn",
    )(qkv, z, gb, cbuf8, s0_bd, *cs)


def _layer_weights(l, p):
    c1 = GDN_CONV_DIM
    c2 = c1 + GDN_VW
    c3 = c2 + GDN_HEADS
    c4 = c3 + GDN_HEADS
    c5 = c4 + Q_RANK
    c6 = c5 + KV_RANK
    w_in = p['w_in'][l]
    half = ROPE_DIM // 2
    kpe_w = w_in[:, c6:]
    kpe_sw = jnp.concatenate([kpe_w[:, half:], kpe_w[:, :half]], axis=1)
    zcol = lambda n: jnp.zeros((D_MODEL, n), F32)
    wsm = jnp.concatenate([kpe_w, w_in[:, c2:c3], w_in[:, c3:c4], zcol(LANES - B_LANE0 - GDN_HEADS),
                           kpe_sw, zcol(LANES - ROPE_DIM)], axis=1)
    wmain = jnp.concatenate([w_in[:, :c2], w_in[:, c4:c6]], axis=1)
    gvec = jnp.zeros((2, LANES), F32)
    gvec = gvec.at[0, G_LANE0:G_LANE0 + GDN_HEADS].set(p['gdn_a_log'][l])
    gvec = gvec.at[1, G_LANE0:G_LANE0 + GDN_HEADS].set(p['gdn_dt_bias'][l])
    wq = p['w_uq'][l].reshape(Q_RANK, MLA_HEADS, NOPE_DIM + ROPE_DIM)
    nope, rope = wq[..., :NOPE_DIM], wq[..., NOPE_DIM:]
    rope_sw = jnp.concatenate([rope[..., half:], rope[..., :half]], axis=-1)
    zq = lambda n: jnp.zeros((Q_RANK, MLA_HEADS, n), F32)
    tail = HEAD_PAD - NOPE_DIM - ROPE_DIM
    wq_a = jnp.concatenate([nope, rope, zq(tail)], axis=-1).reshape(Q_RANK, MLA_HEADS * HEAD_PAD)
    wq_b = jnp.concatenate([zq(NOPE_DIM), rope_sw, zq(tail)], axis=-1).reshape(Q_RANK, MLA_HEADS * HEAD_PAD)
    wkv = p['w_ukv'][l].reshape(KV_RANK, MLA_HEADS, NOPE_DIM + V_DIM)
    wk = jnp.concatenate([wkv[..., :NOPE_DIM], jnp.zeros((KV_RANK, MLA_HEADS, HEAD_PAD - NOPE_DIM), F32)],
                         axis=-1).reshape(KV_RANK, MLA_HEADS * HEAD_PAD)
    wv = wkv[..., NOPE_DIM:].reshape(KV_RANK, MLA_OUT)
    we = np.zeros((ROPE_DIM, MLA_HEADS, HEAD_PAD), np.float32)
    for r in range(ROPE_DIM):
        we[r, :, NOPE_DIM + r] = 1.0
    bf = lambda a: a.astype(BF16)
    r2 = lambda a: a.reshape(1, -1)
    return dict(
        nf1=r2(p['norm_ffn1'][l]), wg1=bf(p['w_ffn1_gate'][l]), wu1=bf(p['w_ffn1_up'][l]),
        wd1=bf(p['w_ffn1_down'][l]), nmix=r2(p['norm_mix'][l]), wmain=bf(wmain), wsm=bf(wsm), gvec=gvec,
        qnorm=r2(p['mla_q_norm'][l]), kvnorm=r2(p['mla_kv_norm'][l]),
        wq2=bf(jnp.concatenate([wq_a, wq_b], axis=1)), wv=bf(wv), wvt=bf(wv.T),
        wke=bf(jnp.concatenate([wk, jnp.asarray(we.reshape(ROPE_DIM, MLA_HEADS * HEAD_PAD))], axis=0)),
        wo=bf(p['w_out'][l]), nf2=r2(p['norm_ffn2'][l]), wg2=bf(p['w_ffn2_gate'][l]),
        wu2=bf(p['w_ffn2_up'][l]), wd2=bf(p['w_ffn2_down'][l]),
        convw=p['gdn_conv_w'][l], gnw=jnp.tile(p['gdn_norm_w'][l], GDN_HEADS).reshape(1, GDN_VW),
    )


def _gdn_consts():
    lane = np.arange(GDN_QK)
    gsum = (lane[:, None] // GDN_DK == lane[None, :] // GDN_DK).astype(np.float32)
    eg = np.zeros((LANES, GDN_QK), np.float32)
    eb = np.zeros((LANES, GDN_QK), np.float32)
    for h in range(GDN_HEADS):
        eg[G_LANE0 + h, h * GDN_DK:(h + 1) * GDN_DK] = 1.0
        eb[B_LANE0 + h, h * GDN_DK:(h + 1) * GDN_DK] = 1.0
    gl = np.arange(GROUP_W)
    bdm = (gl[:, None] // GDN_DK == gl[None, :] // GDN_DK).astype(np.float32)
    return dict(gsum=jnp.asarray(gsum, BF16), eg=jnp.asarray(eg, BF16), eb=jnp.asarray(eb, BF16),
                bdm=jnp.asarray(bdm, BF16))


def _rotary_table(pos, rows):
    half = ROPE_DIM // 2
    inv_freq = 1.0 / (ROPE_THETA ** (jnp.arange(half, dtype=F32) / half))
    ang = pos.astype(F32)[:, None] * inv_freq[None, :]
    cos, sin = jnp.cos(ang), jnp.sin(ang)
    t = pos.shape[0]
    cos2 = jnp.concatenate([cos, cos], axis=1)
    sin2 = jnp.concatenate([-sin, sin], axis=1)
    z = lambda n: jnp.zeros((t, n), F32)
    tail = HEAD_PAD - NOPE_DIM - ROPE_DIM
    tab = jnp.concatenate([jnp.ones((t, NOPE_DIM), F32), cos2, z(tail), z(NOPE_DIM), sin2, z(tail),
                           cos2, z(LANES - ROPE_DIM), sin2, z(LANES - ROPE_DIM)], axis=1)
    return jnp.tile(tab, (rows // t, 1)) if rows > t else tab


def _state_to_blockdiag(s):
    b = s.shape[0]
    s = s.reshape(b, N_GROUPS, GROUP, GDN_DK, GDN_DV)
    eye = jnp.eye(GROUP, dtype=s.dtype)
    bdiag = jnp.einsum('bghij,hk->bghikj', s, eye)
    return bdiag.reshape(b, N_GROUPS, GROUP_W, GROUP_W)


def _blockdiag_to_state(sb):
    b = sb.shape[0]
    s6 = sb.reshape(b, N_GROUPS, GROUP, GDN_DK, GROUP, GDN_DV)
    heads = [s6[:, :, h, :, h, :] for h in range(GROUP)]
    return jnp.stack(heads, axis=2).reshape(b, GDN_HEADS, GDN_DK, GDN_DV)


def _run_trunk(x, pos, past_ckv, past_kpe, s_gdn, s_conv, weights, consts):
    b, t, _ = x.shape
    n = b * t
    rot = _rotary_table(pos, max(t, min(TOK_TILE, n)))
    xf = x.reshape(n, D_MODEL)
    ckv_rows, kpe_rows, states, conv_bufs = [], [], [], []
    depth = len(weights)
    for l, lw in enumerate(weights):
        if s_gdn is None:
            cbuf8 = jnp.zeros((b, 8, GDN_CONV_DIM), F32)
            s0_bd = jnp.zeros((b, N_GROUPS, GROUP_W, GROUP_W), F32)
        else:
            cbuf8 = jnp.pad(s_conv[l], ((0, 0), (8 - (CONV_W - 1), 0), (0, 0)))
            s0_bd = _state_to_blockdiag(s_gdn[l])
        xf, qkv, z, q, c_new, kpe_new, gb = _pre_call(xf, lw, rot, t)
        k_new, v_new = _kv_call(c_new, kpe_new, lw, v_transposed=past_ckv is None)
        q3 = q.reshape(b, t, -1)
        k3 = k_new.reshape(b, t, -1)
        if past_ckv is None:
            m_out = _attn_prompt_call(q3, k3, v_new)
        else:
            pl_len = past_ckv.shape[2]
            k_past, v_past = _kv_call(past_ckv[l].reshape(b * pl_len, KV_RANK),
                                      past_kpe[l].reshape(b * pl_len, ROPE_DIM), lw, v_transposed=False)
            m_out = _attn_sample_call(q3, k_past.reshape(b, pl_len, -1), v_past.reshape(b, pl_len, -1), k3,
                                      v_new.reshape(b, t, -1))
        qkv3 = qkv.reshape(b, t, GDN_CONV_DIM)
        g_out, s_bd = _gdn_call(qkv3, z.reshape(b, t, GDN_VW), gb.reshape(b, t, LANES), cbuf8, s0_bd, lw,
                                consts)
        final = l == depth - 1
        xf = _post_call(xf, g_out.reshape(n, GDN_VW), m_out.reshape(n, MLA_OUT), lw,
                        weights[-1]['nfinal'], final)
        ckv_rows.append(c_new.reshape(b, t, KV_RANK))
        kpe_rows.append(kpe_new.reshape(b, t, ROPE_DIM))
        states.append(_blockdiag_to_state(s_bd))
        conv_bufs.append(qkv3[:, t - (CONV_W - 1):, :])
    return (xf.reshape(b, t, D_MODEL), jnp.stack(ckv_rows), jnp.stack(kpe_rows), jnp.stack(states),
            jnp.stack(conv_bufs))


def kernel(x_prompt, x_sample, cache_mla_ckv, cache_mla_krope, state_gdn, state_gdn_conv,
           norm_ffn1, w_ffn1_gate, w_ffn1_up, w_ffn1_down, norm_mix, w_in, gdn_conv_w, gdn_a_log,
           gdn_dt_bias, gdn_norm_w, mla_q_norm, mla_kv_norm, w_uq, w_ukv, w_out, norm_ffn2,
           w_ffn2_gate, w_ffn2_up, w_ffn2_down, norm_final):
    params = dict(norm_ffn1=norm_ffn1, w_ffn1_gate=w_ffn1_gate, w_ffn1_up=w_ffn1_up, w_ffn1_down=w_ffn1_down,
                  norm_mix=norm_mix, w_in=w_in, gdn_conv_w=gdn_conv_w, gdn_a_log=gdn_a_log,
                  gdn_dt_bias=gdn_dt_bias, gdn_norm_w=gdn_norm_w, mla_q_norm=mla_q_norm,
                  mla_kv_norm=mla_kv_norm, w_uq=w_uq, w_ukv=w_ukv, w_out=w_out, norm_ffn2=norm_ffn2,
                  w_ffn2_gate=w_ffn2_gate, w_ffn2_up=w_ffn2_up, w_ffn2_down=w_ffn2_down)
    depth = w_in.shape[0]
    weights = [_layer_weights(l, params) for l in range(depth)]
    weights[-1]['nfinal'] = norm_final.reshape(1, D_MODEL)
    consts = _gdn_consts()
    t_p = x_prompt.shape[1]
    out_p = _run_trunk(x_prompt, jnp.arange(t_p, dtype=jnp.int32), None, None, None, None, weights, consts)
    pos_s = cache_mla_ckv.shape[2] + jnp.arange(x_sample.shape[1], dtype=jnp.int32)
    out_s = _run_trunk(x_sample, pos_s, cache_mla_ckv, cache_mla_krope, state_gdn, state_gdn_conv,
                       weights, consts)
    return (out_p[0], out_s[0]) + out_p[1:] + out_s[1:]
```

```python
import functools

import numpy as np
import jax
import jax.numpy as jnp
from jax import lax
from jax.experimental import pallas as pl
from jax.experimental.pallas import tpu as pltpu

F32 = jnp.float32
BF16 = jnp.bfloat16

D_MODEL = 1024
D_FF = 2816
NORM_EPS = 1e-6
L2_EPS = 1e-6
CHUNK = 64
GDN_HEADS = 8
GDN_DK = 64
GDN_DV = 64
CONV_W = 4
GDN_QK = GDN_HEADS * GDN_DK
GDN_VW = GDN_HEADS * GDN_DV
GDN_CONV_DIM = 2 * GDN_QK + GDN_VW
MLA_HEADS = 8
Q_RANK = 256
KV_RANK = 128
NOPE_DIM = 64
ROPE_DIM = 32
V_DIM = 64
ROPE_THETA = 10000.0
MLA_OUT = MLA_HEADS * V_DIM
MASK_VALUE = -1e30

LANES = 128
HEAD_PAD = 128
MAIN_COLS = GDN_CONV_DIM + GDN_VW + Q_RANK + KV_RANK
FF_CHUNK = 512
GROUP = 2
GROUP_W = GROUP * GDN_DK
N_GROUPS = GDN_HEADS // GROUP
G_LANE0 = 32
B_LANE0 = 40
VMEM_LIMIT = 56 * 1024 * 1024
TOK_TILE = 512
POST_TILE = 512
KV_TILE = 1024
ATT_TILE = 256
ATT_HEADS = 8
GDN_TB = 256
GDN_NB = 2
INV_BASE = 8


def _dot(a, b):
    return jnp.dot(a, b, preferred_element_type=F32)


def _dot_nt(a, b):
    return lax.dot_general(a, b, (((1,), (1,)), ((), ())), preferred_element_type=F32)


def _dot_tn(a, b):
    return lax.dot_general(a, b, (((0,), (0,)), ((), ())), preferred_element_type=F32)


def _split_dot(x, w01, passes, left=False):
    acc = None
    r = x
    for i in range(passes):
        part = r.astype(BF16)
        d = _dot(w01, part) if left else _dot(part, w01)
        acc = d if acc is None else acc + d
        if i + 1 < passes:
            r = r - part.astype(F32)
    return acc


def _rms(x, w):
    return x * lax.rsqrt(jnp.mean(x * x, axis=-1, keepdims=True) + NORM_EPS) * w


def _silu(x):
    return x * jax.nn.sigmoid(x)


def _ffn_residual(x, nw, wg_ref, wu_ref, wd_ref):
    xn = _rms(x, nw).astype(BF16)
    acc = None
    for c0 in range(0, D_FF, FF_CHUNK):
        c1 = min(c0 + FF_CHUNK, D_FF)
        g = _dot(xn, wg_ref[:, c0:c1])
        u = _dot(xn, wu_ref[:, c0:c1])
        hid = (_silu(g) * u).astype(BF16)
        d = _dot(hid, wd_ref[c0:c1, :])
        acc = d if acc is None else acc + d
    return x + 0.5 * acc


def _causal_conv(x, tail, cw):
    row8 = lax.broadcasted_iota(jnp.int32, (8, x.shape[1]), 0)
    conv = x * cw[CONV_W - 1:CONV_W, :]
    for sh in range(1, CONV_W):
        xr = pltpu.roll(x, sh, axis=0)
        head = jnp.where(row8 < sh, pltpu.roll(tail, sh, axis=0), xr[:8])
        conv = conv + jnp.concatenate([head, xr[8:]], axis=0) * cw[CONV_W - 1 - sh:CONV_W - sh, :]
    return conv


def _pre_kernel(x_ref, nf_ref, wg_ref, wu_ref, wd_ref, nm_ref, wmain_ref, wsm_ref, gvec_ref,
                qn_ref, kvn_ref, wq2_ref, rot_ref,
                xo_ref, qkv_ref, z_ref, q_ref, c_ref, kpe_ref, gb_ref):
    x1 = _ffn_residual(x_ref[...], nf_ref[...], wg_ref, wu_ref, wd_ref)
    xo_ref[...] = x1
    h = _rms(x1, nm_ref[...]).astype(BF16)
    qkv_ref[...] = _dot(h, wmain_ref[:, :GDN_CONV_DIM])
    c0 = GDN_CONV_DIM + GDN_VW
    z_ref[...] = _dot(h, wmain_ref[:, GDN_CONV_DIM:c0])
    lat = _dot(h, wmain_ref[:, c0:])
    cq = lat[:, :Q_RANK]
    ckv = lat[:, Q_RANK:]
    rot = rot_ref[...]
    cosq, sinq = rot[:, :LANES], rot[:, LANES:2 * LANES]
    cosk, sink = rot[:, 2 * LANES:3 * LANES], rot[:, 3 * LANES:]
    cqn = _rms(cq, qn_ref[...]).astype(BF16)
    nq = MLA_HEADS * HEAD_PAD
    cos8 = jnp.concatenate([cosq] * MLA_HEADS, axis=1)
    sin8 = jnp.concatenate([sinq] * MLA_HEADS, axis=1)
    q_ref[...] = (_dot(cqn, wq2_ref[:, :nq]) * cos8 + _dot(cqn, wq2_ref[:, nq:]) * sin8).astype(BF16)
    c_ref[...] = _rms(ckv, kvn_ref[...])
    small = _dot(h, wsm_ref[...])
    s1, s2 = small[:, :LANES], small[:, LANES:]
    kpe_ref[...] = (s1 * cosk + s2 * sink)[:, :ROPE_DIM]
    gvec = gvec_ref[...]
    xa = s1 + gvec[1:2, :]
    softplus = jnp.maximum(xa, 0.0) + jnp.log1p(jnp.exp(-jnp.abs(xa)))
    gfull = -jnp.exp(gvec[0:1, :]) * softplus
    bfull = jax.nn.sigmoid(s1)
    lane = lax.broadcasted_iota(jnp.int32, s1.shape, 1)
    is_g = (lane >= G_LANE0) & (lane < G_LANE0 + GDN_HEADS)
    is_b = (lane >= B_LANE0) & (lane < B_LANE0 + GDN_HEADS)
    gb_ref[...] = jnp.where(is_g, gfull, jnp.where(is_b, bfull, 0.0))


def _post_kernel(x_ref, g_ref, m_ref, wo_ref, nf_ref, wg_ref, wu_ref, wd_ref, nfin_ref, y_ref, *, final):
    x = x_ref[...] + _dot(g_ref[...], wo_ref[:GDN_VW, :]) + _dot(m_ref[...], wo_ref[GDN_VW:, :])
    x2 = _ffn_residual(x, nf_ref[...], wg_ref, wu_ref, wd_ref)
    y_ref[...] = _rms(x2, nfin_ref[...]) if final else x2


def _const_spec(shape):
    nd = len(shape)
    return pl.BlockSpec(shape, lambda *_: (0,) * nd, pipeline_mode=pl.Buffered(1))


def _params(sem):
    return pltpu.CompilerParams(dimension_semantics=sem, vmem_limit_bytes=VMEM_LIMIT)


def _pre_call(x, lw, rot, seq_len):
    n = x.shape[0]
    tm = min(TOK_TILE, n)
    rot_blocks = rot.shape[0] // tm
    row = lambda w: pl.BlockSpec((tm, w), lambda i: (i, 0))
    consts = [lw['nf1'], lw['wg1'], lw['wu1'], lw['wd1'], lw['nmix'], lw['wmain'], lw['wsm'], lw['gvec'],
              lw['qnorm'], lw['kvnorm'], lw['wq2']]
    in_specs = [row(D_MODEL)] + [_const_spec(c.shape) for c in consts]
    in_specs.append(pl.BlockSpec((tm, 4 * LANES), lambda i: (i % rot_blocks, 0)))
    out_shape = (
        jax.ShapeDtypeStruct((n, D_MODEL), F32),
        jax.ShapeDtypeStruct((n, GDN_CONV_DIM), F32),
        jax.ShapeDtypeStruct((n, GDN_VW), F32),
        jax.ShapeDtypeStruct((n, MLA_HEADS * HEAD_PAD), BF16),
        jax.ShapeDtypeStruct((n, KV_RANK), F32),
        jax.ShapeDtypeStruct((n, ROPE_DIM), F32),
        jax.ShapeDtypeStruct((n, LANES), F32),
    )
    out_specs = (row(D_MODEL), row(GDN_CONV_DIM), row(GDN_VW), row(MLA_HEADS * HEAD_PAD), row(KV_RANK),
                 row(ROPE_DIM), row(LANES))
    return pl.pallas_call(
        _pre_kernel, grid=(n // tm,), in_specs=in_specs, out_specs=out_specs, out_shape=out_shape,
        compiler_params=_params(("parallel",)), name="ffn_proj",
    )(x, *consts, rot)


def _post_call(x, g_out, m_out, lw, nfin, final):
    n = x.shape[0]
    tm = min(POST_TILE, n)
    row = lambda w: pl.BlockSpec((tm, w), lambda i: (i, 0))
    consts = [lw['wo'], lw['nf2'], lw['wg2'], lw['wu2'], lw['wd2'], nfin]
    in_specs = [row(D_MODEL), row(GDN_VW), row(MLA_OUT)] + [_const_spec(c.shape) for c in consts]
    return pl.pallas_call(
        functools.partial(_post_kernel, final=final), grid=(n // tm,), in_specs=in_specs,
        out_specs=row(D_MODEL), out_shape=jax.ShapeDtypeStruct((n, D_MODEL), F32),
        compiler_params=_params(("parallel",)), name="mix_ffn",
    )(x, g_out, m_out, *consts)


def _kv_kernel(c_ref, kpe_ref, wk_ref, wv_ref, k_ref, v_ref, *, v_transposed):
    c = c_ref[...].astype(BF16)
    ck = jnp.concatenate([c, kpe_ref[...].astype(BF16)], axis=1)
    k_ref[...] = _dot(ck, wk_ref[...]).astype(BF16)
    if v_transposed:
        for i in range(v_ref.shape[0]):
            v_ref[i] = _dot_nt(wv_ref[...], c[i * ATT_TILE:(i + 1) * ATT_TILE]).astype(BF16)
    else:
        v_ref[...] = _dot(c, wv_ref[...]).astype(BF16)


def _kv_call(c, kpe, lw, v_transposed):
    n = c.shape[0]
    tm = min(KV_TILE, n)
    row = lambda w: pl.BlockSpec((tm, w), lambda i: (i, 0))
    consts = [lw['wke'], lw['wvt'] if v_transposed else lw['wv']]
    if v_transposed:
        v_spec = pl.BlockSpec((tm // ATT_TILE, MLA_OUT, ATT_TILE), lambda i: (i, 0, 0))
        v_shape = jax.ShapeDtypeStruct((n // ATT_TILE, MLA_OUT, ATT_TILE), BF16)
    else:
        v_spec = row(MLA_OUT)
        v_shape = jax.ShapeDtypeStruct((n, MLA_OUT), BF16)
    return pl.pallas_call(
        functools.partial(_kv_kernel, v_transposed=v_transposed), grid=(n // tm,),
        in_specs=[row(KV_RANK), row(ROPE_DIM)] + [_const_spec(w.shape) for w in consts],
        out_specs=(row(MLA_HEADS * HEAD_PAD), v_spec),
        out_shape=(jax.ShapeDtypeStruct((n, MLA_HEADS * HEAD_PAD), BF16), v_shape),
        compiler_params=_params(("parallel",)), name="kv_up",
    )(c, kpe, *consts)


_ATT_SCALE = float((NOPE_DIM + ROPE_DIM) ** -0.5)
_ATT_SCALE_LOG2 = float(_ATT_SCALE * np.log2(np.e))


def _merge_heads(outs):
    lane = lax.broadcasted_iota(jnp.int32, outs[0].shape, 1)
    return jnp.where(lane < V_DIM, outs[0], outs[1])


def _attn_prompt_kernel(q_ref, k_ref, vt_ref, o_ref, s_scr, p_scr, m_scr, l_scr, acc_scr, *, tile, nh):
    qi = pl.program_id(2)
    heads = tuple(range(nh))
    pair_rows = [slice((h // 2) * 2 * V_DIM, (h // 2 + 1) * 2 * V_DIM) for h in heads]
    hsl = [slice(h * HEAD_PAD, (h + 1) * HEAD_PAD) for h in heads]

    def scores(j, h):
        k0 = pl.multiple_of(j * tile, tile)
        return _dot_nt(k_ref[0, pl.ds(k0, tile), hsl[h]], q_ref[0, :, hsl[h]]) * _ATT_SCALE_LOG2

    def pv(j, h):
        return _dot(vt_ref[j, pair_rows[h], :], p_scr[h])

    def softmax(h, s, acc_in):
        m_prev = m_scr[h]
        m_new = jnp.maximum(m_prev, jnp.max(s, axis=0, keepdims=True))
        alpha = jnp.exp2(m_prev - m_new)
        p = jnp.exp2(s - m_new)
        l_scr[h] = alpha * l_scr[h] + jnp.sum(p, axis=0, keepdims=True)
        acc_scr[h] = alpha * acc_in
        m_scr[h] = m_new
        p_scr[h] = p.astype(BF16)

    for h in heads:
        s_scr[h] = scores(0, h)
        p_scr[h] = jnp.zeros((tile, tile), BF16)
        m_scr[h] = jnp.full((1, tile), MASK_VALUE, F32)
        l_scr[h] = jnp.zeros((1, tile), F32)
        acc_scr[h] = jnp.zeros((2 * V_DIM, tile), F32)

    def body(j, carry):
        jm1 = jnp.maximum(j - 1, 0)
        accs = [acc_scr[h] + pv(jm1, h) for h in heads]
        s_next = [scores(j + 1, h) for h in heads]
        for h in heads:
            softmax(h, s_scr[h], accs[h])
        for h in heads:
            s_scr[h] = s_next[h]
        return carry

    lax.fori_loop(0, qi, body, 0)

    kc = lax.broadcasted_iota(jnp.int32, (tile, tile), 0) // CHUNK
    qc = lax.broadcasted_iota(jnp.int32, (tile, tile), 1) // CHUNK
    jm1 = jnp.maximum(qi - 1, 0)
    accs = [acc_scr[h] + pv(jm1, h) for h in heads]
    for h in heads:
        softmax(h, jnp.where(kc <= qc, s_scr[h], MASK_VALUE), accs[h])
    outs = [(acc_scr[h] + pv(qi, h)) / l_scr[h] for h in heads]
    vrow = lax.broadcasted_iota(jnp.int32, outs[0].shape, 0)
    merged = [jnp.where(vrow < V_DIM, outs[h], outs[h + 1]) for h in heads[::2]]
    o_ref[0] = jnp.concatenate(merged, axis=0).T.astype(BF16)


def _attn_prompt_call(q, k, vt):
    b, t, _ = q.shape
    tile = ATT_TILE
    nh = ATT_HEADS
    groups = MLA_HEADS // nh
    nk = t // tile
    return pl.pallas_call(
        functools.partial(_attn_prompt_kernel, tile=tile, nh=nh), grid=(b, groups, nk),
        in_specs=[pl.BlockSpec((1, tile, nh * HEAD_PAD), lambda bi, p, i: (bi, i, p)),
                  pl.BlockSpec((1, t, nh * HEAD_PAD), lambda bi, p, i: (bi, 0, p)),
                  pl.BlockSpec((nk, nh * V_DIM, tile), lambda bi, p, i: (bi, p, 0))],
        out_specs=pl.BlockSpec((1, tile, nh * V_DIM), lambda bi, p, i: (bi, i, p)),
        out_shape=jax.ShapeDtypeStruct((b, t, MLA_OUT), BF16),
        scratch_shapes=[pltpu.VMEM((nh, tile, tile), F32), pltpu.VMEM((nh, tile, tile), BF16),
                        pltpu.VMEM((nh, 1, tile), F32), pltpu.VMEM((nh, 1, tile), F32),
                        pltpu.VMEM((nh, 2 * V_DIM, tile), F32)],
        compiler_params=_params(("parallel", "parallel", "arbitrary")), name="attn_prompt",
    )(q, k, vt)


def _attn_sample_kernel(q_ref, cp_ref, kpp_ref, kn_ref, vn_ref, wabs_ref, wv_ref, o_ref, *, tq, past):
    pad = LANES - tq
    heads = range(MLA_HEADS)
    hsl = [slice(h * HEAD_PAD, (h + 1) * HEAD_PAD) for h in heads]
    cp = cp_ref[0].astype(BF16)
    ckp = jnp.concatenate([cp, kpp_ref[0].astype(BF16)], axis=1)
    qa = jnp.concatenate([_dot(q_ref[0, :, hsl[h]], wabs_ref[h]).astype(BF16) for h in heads], axis=0)
    q_chunk = jnp.concatenate([(past + lax.broadcasted_iota(jnp.int32, (tq, 1), 0)) // CHUNK] * MLA_HEADS, axis=0)
    s_p = _dot_nt(qa, ckp) * _ATT_SCALE
    kc_p = lax.broadcasted_iota(jnp.int32, s_p.shape, 1) // CHUNK
    s_p = jnp.where(kc_p <= q_chunk, s_p, MASK_VALUE)
    s_n = []
    for h in heads:
        kn = jnp.concatenate([kn_ref[0, :, hsl[h]], jnp.zeros((pad, HEAD_PAD), BF16)], axis=0)
        s_n.append(_dot_nt(q_ref[0, :, hsl[h]], kn))
    s_n = jnp.concatenate(s_n, axis=0) * _ATT_SCALE
    col = lax.broadcasted_iota(jnp.int32, s_n.shape, 1)
    s_n = jnp.where((col < tq) & ((past + col) // CHUNK <= q_chunk), s_n, MASK_VALUE)
    m = jnp.maximum(jnp.max(s_p, axis=-1, keepdims=True), jnp.max(s_n, axis=-1, keepdims=True))
    p_p = jnp.exp(s_p - m)
    e_n = jnp.exp(s_n - m)
    p_n = e_n.astype(BF16)
    l = jnp.sum(p_p, axis=-1, keepdims=True) + jnp.sum(e_n, axis=-1, keepdims=True)
    pc = _dot(p_p.astype(BF16), cp).astype(BF16)
    pairs = []
    for pr in range(MLA_HEADS // 2):
        vs = slice(pr * 2 * V_DIM, (pr + 1) * 2 * V_DIM)
        vn = jnp.concatenate([vn_ref[0, :, vs], jnp.zeros((pad, 2 * V_DIM), BF16)], axis=0)
        outs = []
        for h in (2 * pr, 2 * pr + 1):
            rs = slice(h * tq, (h + 1) * tq)
            outs.append((_dot(pc[rs], wv_ref[:, vs]) + _dot(p_n[rs], vn)) / l[rs])
        pairs.append(_merge_heads(outs))
    o_ref[0] = jnp.concatenate(pairs, axis=1).astype(BF16)


def _attn_sample_call(q, c_past, kpe_past, k_new, v_new, lw):
    b, tq, _ = q.shape
    past = c_past.shape[1]
    blk = lambda rows, w: pl.BlockSpec((1, rows, w), lambda bi: (bi, 0, 0))
    consts = [lw['wabs'], lw['wv']]
    return pl.pallas_call(
        functools.partial(_attn_sample_kernel, tq=tq, past=past), grid=(b,),
        in_specs=[blk(tq, MLA_HEADS * HEAD_PAD), blk(past, KV_RANK), blk(past, ROPE_DIM),
                  blk(tq, MLA_HEADS * HEAD_PAD), blk(tq, MLA_OUT)] + [_const_spec(w.shape) for w in consts],
        out_specs=blk(tq, MLA_OUT),
        out_shape=jax.ShapeDtypeStruct((b, tq, MLA_OUT), BF16),
        compiler_params=_params(("parallel",)), name="attn_sample",
    )(q, c_past, kpe_past, k_new, v_new, *consts)


def _gdn_kernel(x_ref, z_ref, gb_ref, cbuf_ref, s0_ref, cw_ref, nw_ref, gsum_ref, eg_ref, eb_ref, bdm_ref,
                o_ref, sout_ref, s_scr, tail_scr, *, nb, tb, nt):
    t = pl.program_id(1)
    rows = max(tb, CHUNK)

    @pl.when(t == 0)
    def _():
        s_scr[...] = s0_ref[...]
        tail_scr[...] = cbuf_ref[...]

    cw = cw_ref[...]
    gsum = gsum_ref[...]
    bdm = bdm_ref[...]
    bdm_f32 = bdm.astype(F32)

    def bd(m):
        mb = m.astype(BF16)
        return jnp.concatenate([mb] * GROUP, axis=0) * bdm

    ri = lax.broadcasted_iota(jnp.int32, (CHUNK, GROUP_W), 0)
    ci = lax.broadcasted_iota(jnp.int32, (CHUNK, GROUP_W), 1) % CHUNK
    causal = ri >= ci
    strict = ri > ci
    eye = jnp.where(ri == ci, 1.0, 0.0).astype(F32)
    tri = jnp.where(lax.broadcasted_iota(jnp.int32, (CHUNK, CHUNK), 0)
                    >= lax.broadcasted_iota(jnp.int32, (CHUNK, CHUNK), 1), 1.0, 0.0).astype(BF16)

    def same_block(size):
        return (ri // size) == (ci // size)

    def precompute(units):
        def decay_and_products():
            for u in units:
                gcum = _split_dot(u['g'], tri, 3, left=True)
                grow = jnp.sum(jnp.where(ri <= ci, u['g'], 0.0), axis=0, keepdims=True)
                eg = jnp.exp(gcum)
                glast = gcum[CHUNK - 1:CHUNK, :]
                kb = u['k'] * u['b']
                u['qd'] = u['q'] * eg
                u['kd'] = u['k'] * jnp.exp(glast - gcum)
                u['eglast'] = jnp.exp(glast)
                u['vb'] = u['v'] * u['b']
                u['kg'] = kb * eg
                decay = jnp.where(causal, jnp.exp(jnp.where(causal, gcum - grow, 0.0)), 0.0)
                kq = _dot_nt(jnp.concatenate([kb, u['q']], axis=0).astype(BF16), bd(u['k']))
                u['attn'] = kq[CHUNK:] * decay
                u['lower'] = jnp.where(strict, kq[:CHUNK] * decay, 0.0)
                u['m'] = -jnp.where(same_block(INV_BASE), u['lower'], 0.0)
                u['inv'] = eye + u['m']

        def square():
            for u in units:
                u['m'] = _dot(u['m'].astype(BF16), bd(u['m']))

        def square_and_extend():
            for u in units:
                r = _dot(jnp.concatenate([u['m'], u['inv']], axis=0).astype(BF16), bd(u['m']))
                u['m'] = r[:CHUNK]
                u['inv'] = u['inv'] + r[CHUNK:]

        def extend():
            for u in units:
                u['inv'] = u['inv'] + _dot(u['inv'].astype(BF16), bd(u['m']))

        def couple(size):
            def left():
                for u in units:
                    off = jnp.where(same_block(2 * size) & ~same_block(size), u['lower'], 0.0)
                    u['m'] = _dot(off.astype(BF16), bd(u['inv']))

            def right():
                for u in units:
                    u['inv'] = u['inv'] - _dot(u['inv'].astype(BF16), bd(u['m']))
            return [left, right]

        def solve():
            for u in units:
                invb = u['inv'].astype(BF16)
                u['value'] = _dot(invb, bd(u['vb']))
                u['kcum'] = _dot(invb, bd(u['kg']))

        stages = [decay_and_products, square, square_and_extend, extend]
        size = INV_BASE
        while size < CHUNK:
            stages += couple(size)
            size *= 2
        return stages + [solve]

    n_chunks = rows // CHUNK
    units = [[] for _ in range(nb)]

    conv_rows = {}

    def conv_piece(bi, c):
        r0 = c * CHUNK
        r1 = min(r0 + CHUNK, tb)
        if c == 0:
            tail = tail_scr[bi]
            tail_scr[bi] = x_ref[bi, tb - 8:tb, :]
        else:
            tail = x_ref[bi, r0 - 8:r0, :]
        conv_rows[(bi, c)] = _silu(_causal_conv(x_ref[bi, r0:r1, :], tail, cw))

    def front(bi):
        pieces = [conv_rows[(bi, c)] for c in range(n_chunks)]
        y = jnp.concatenate(pieces, axis=0) if n_chunks > 1 else pieces[0]
        gb = gb_ref[bi]
        if rows > tb:
            y = jnp.concatenate([y, jnp.zeros((rows - tb, GDN_CONV_DIM), F32)], axis=0)
            gb = jnp.concatenate([gb, jnp.zeros((rows - tb, LANES), F32)], axis=0)
        q, k, v = y[:, :GDN_QK], y[:, GDN_QK:2 * GDN_QK], y[:, 2 * GDN_QK:]
        qn = q * lax.rsqrt(_split_dot(q * q, gsum, 2) + L2_EPS) * (GDN_DK ** -0.5)
        kn = k * lax.rsqrt(_split_dot(k * k, gsum, 2) + L2_EPS)
        ge = _split_dot(gb, eg_ref[...], 3)
        be = _split_dot(gb, eb_ref[...], 3)
        for c in range(n_chunks):
            rs = slice(c * CHUNK, (c + 1) * CHUNK)
            for p in range(N_GROUPS):
                ls = slice(p * GROUP_W, (p + 1) * GROUP_W)
                units[bi].append(dict(bi=bi, c=c, p=p, q=qn[rs, ls], k=kn[rs, ls], v=v[rs, ls], g=ge[rs, ls],
                                      b=be[rs, ls]))

    state = {}
    outs = {}

    def rec_read(bi, c):
        for u in units[bi]:
            if u['c'] == c:
                s = state.setdefault((bi, u['p']), s_scr[bi, u['p']]) if c == 0 else state[(bi, u['p'])]
                u['r'] = _dot(jnp.concatenate([u['kcum'], u['qd']], axis=0).astype(BF16), s.astype(BF16))

    def rec_update(bi, c):
        for u in units[bi]:
            if u['c'] == c:
                v_new = u['value'] - u['r'][:CHUNK]
                outs[(bi, c, u['p'])] = u['r'][CHUNK:] + _dot(u['attn'].astype(BF16), bd(v_new))
                upd = _dot_tn(u['kd'].astype(BF16), v_new.astype(BF16))
                key = (bi, u['p'])
                state[key] = state[key] * u['eglast'] + upd * bdm_f32

    def finish(bi):
        for p in range(N_GROUPS):
            s_scr[bi, p] = state[(bi, p)]
        o_rows = [jnp.concatenate([outs[(bi, c, p)] for p in range(N_GROUPS)], axis=1) for c in range(n_chunks)]
        o = jnp.concatenate(o_rows, axis=0) if n_chunks > 1 else o_rows[0]
        o = o[:tb]
        ms = _split_dot(o * o, gsum, 2) * (1.0 / GDN_DV)
        o_ref[bi] = (o * lax.rsqrt(ms + NORM_EPS) * nw_ref[...] * _silu(z_ref[bi])).astype(BF16)

    def bind(fn, *args):
        return lambda: fn(*args)

    fronts = [[bind(front, bi)] for bi in range(nb)]
    pres = [precompute(units[bi]) for bi in range(nb)]
    recs = [[f for c in range(n_chunks) for f in (bind(rec_read, bi, c), bind(rec_update, bi, c))]
            + [bind(finish, bi)] for bi in range(nb)]

    def interleave(*seqs):
        seqs = [s for s in seqs if s]
        n = max(len(s) for s in seqs)
        keyed = [((i + 0.5) * n / len(s), k, f) for k, s in enumerate(seqs) for i, f in enumerate(s)]
        return [f for _, _, f in sorted(keyed, key=lambda e: (e[0], e[1]))]

    program = []
    for bi in range(nb):
        program += [bind(conv_piece, bi, c) for c in range(n_chunks)] + fronts[bi] + pres[bi]
    program += interleave(*recs)
    for thunk in program:
        thunk()

    @pl.when(t == nt - 1)
    def _():
        sout_ref[...] = s_scr[...]


def _gdn_call(qkv, z, gb, cbuf8, s0_bd, lw, consts):
    b, t, _ = qkv.shape
    tb = min(GDN_TB, t)
    nt = t // tb
    nb = GDN_NB
    blk = lambda w: pl.BlockSpec((nb, tb, w), lambda bi, ti: (bi, ti, 0))
    per_b = lambda shape: pl.BlockSpec((nb,) + shape, lambda bi, ti: (bi,) + (0,) * len(shape))
    cs = [lw['convw'], lw['gnw'], consts['gsum'], consts['eg'], consts['eb'], consts['bdm']]
    return pl.pallas_call(
        functools.partial(_gdn_kernel, nb=nb, tb=tb, nt=nt), grid=(b // nb, nt),
        in_specs=[blk(GDN_CONV_DIM), blk(GDN_VW), blk(LANES), per_b((8, GDN_CONV_DIM)),
                  per_b((N_GROUPS, GROUP_W, GROUP_W))] + [_const_spec(c.shape) for c in cs],
        out_specs=(blk(GDN_VW), per_b((N_GROUPS, GROUP_W, GROUP_W))),
        out_shape=(jax.ShapeDtypeStruct((b, t, GDN_VW), BF16),
                   jax.ShapeDtypeStruct((b, N_GROUPS, GROUP_W, GROUP_W), F32)),
        scratch_shapes=[pltpu.VMEM((nb, N_GROUPS, GROUP_W, GROUP_W), F32),
                        pltpu.VMEM((nb, 8, GDN_CONV_DIM), F32)],
        compiler_params=_params(("parallel", "arbitrary")), name="gdn",
    )(qkv, z, gb, cbuf8, s0_bd, *cs)


def _layer_weights(l, p):
    c1 = GDN_CONV_DIM
    c2 = c1 + GDN_VW
    c3 = c2 + GDN_HEADS
    c4 = c3 + GDN_HEADS
    c5 = c4 + Q_RANK
    c6 = c5 + KV_RANK
    w_in = p['w_in'][l]
    half = ROPE_DIM // 2
    kpe_w = w_in[:, c6:]
    kpe_sw = jnp.concatenate([kpe_w[:, half:], kpe_w[:, :half]], axis=1)
    zcol = lambda n: jnp.zeros((D_MODEL, n), F32)
    wsm = jnp.concatenate([kpe_w, w_in[:, c2:c3], w_in[:, c3:c4], zcol(LANES - B_LANE0 - GDN_HEADS),
                           kpe_sw, zcol(LANES - ROPE_DIM)], axis=1)
    wmain = jnp.concatenate([w_in[:, :c2], w_in[:, c4:c6]], axis=1)
    gvec = jnp.zeros((2, LANES), F32)
    gvec = gvec.at[0, G_LANE0:G_LANE0 + GDN_HEADS].set(p['gdn_a_log'][l])
    gvec = gvec.at[1, G_LANE0:G_LANE0 + GDN_HEADS].set(p['gdn_dt_bias'][l])
    wq = p['w_uq'][l].reshape(Q_RANK, MLA_HEADS, NOPE_DIM + ROPE_DIM)
    nope, rope = wq[..., :NOPE_DIM], wq[..., NOPE_DIM:]
    rope_sw = jnp.concatenate([rope[..., half:], rope[..., :half]], axis=-1)
    zq = lambda n: jnp.zeros((Q_RANK, MLA_HEADS, n), F32)
    tail = HEAD_PAD - NOPE_DIM - ROPE_DIM
    wq_a = jnp.concatenate([nope, rope, zq(tail)], axis=-1).reshape(Q_RANK, MLA_HEADS * HEAD_PAD)
    wq_b = jnp.concatenate([zq(NOPE_DIM), rope_sw, zq(tail)], axis=-1).reshape(Q_RANK, MLA_HEADS * HEAD_PAD)
    wkv = p['w_ukv'][l].reshape(KV_RANK, MLA_HEADS, NOPE_DIM + V_DIM)
    wk = jnp.concatenate([wkv[..., :NOPE_DIM], jnp.zeros((KV_RANK, MLA_HEADS, HEAD_PAD - NOPE_DIM), F32)],
                         axis=-1).reshape(KV_RANK, MLA_HEADS * HEAD_PAD)
    wv = wkv[..., NOPE_DIM:].reshape(KV_RANK, MLA_OUT)
    we = np.zeros((ROPE_DIM, MLA_HEADS, HEAD_PAD), np.float32)
    for r in range(ROPE_DIM):
        we[r, :, NOPE_DIM + r] = 1.0
    rope_pass = np.zeros((HEAD_PAD - NOPE_DIM, KV_RANK + ROPE_DIM), np.float32)
    for r in range(ROPE_DIM):
        rope_pass[r, KV_RANK + r] = 1.0
    wk_t = jnp.transpose(wkv[..., :NOPE_DIM], (1, 2, 0))
    wabs = jnp.concatenate([jnp.concatenate([wk_t, jnp.zeros((MLA_HEADS, NOPE_DIM, ROPE_DIM), F32)], axis=2),
                            jnp.broadcast_to(rope_pass, (MLA_HEADS,) + rope_pass.shape)], axis=1)
    bf = lambda a: a.astype(BF16)
    r2 = lambda a: a.reshape(1, -1)
    return dict(
        nf1=r2(p['norm_ffn1'][l]), wg1=bf(p['w_ffn1_gate'][l]), wu1=bf(p['w_ffn1_up'][l]),
        wd1=bf(p['w_ffn1_down'][l]), nmix=r2(p['norm_mix'][l]), wmain=bf(wmain), wsm=bf(wsm), gvec=gvec,
        qnorm=r2(p['mla_q_norm'][l]), kvnorm=r2(p['mla_kv_norm'][l]),
        wq2=bf(jnp.concatenate([wq_a, wq_b], axis=1)), wv=bf(wv), wvt=bf(wv.T), wabs=bf(wabs),
        wke=bf(jnp.concatenate([wk, jnp.asarray(we.reshape(ROPE_DIM, MLA_HEADS * HEAD_PAD))], axis=0)),
        wo=bf(p['w_out'][l]), nf2=r2(p['norm_ffn2'][l]), wg2=bf(p['w_ffn2_gate'][l]),
        wu2=bf(p['w_ffn2_up'][l]), wd2=bf(p['w_ffn2_down'][l]),
        convw=p['gdn_conv_w'][l], gnw=jnp.tile(p['gdn_norm_w'][l], GDN_HEADS).reshape(1, GDN_VW),
    )


def _gdn_consts():
    lane = np.arange(GDN_QK)
    gsum = (lane[:, None] // GDN_DK == lane[None, :] // GDN_DK).astype(np.float32)
    eg = np.zeros((LANES, GDN_QK), np.float32)
    eb = np.zeros((LANES, GDN_QK), np.float32)
    for h in range(GDN_HEADS):
        eg[G_LANE0 + h, h * GDN_DK:(h + 1) * GDN_DK] = 1.0
        eb[B_LANE0 + h, h * GDN_DK:(h + 1) * GDN_DK] = 1.0
    gl = np.arange(GROUP_W)
    bdm = (gl[:, None] // GDN_DK == gl[None, :] // GDN_DK).astype(np.float32)
    return dict(gsum=jnp.asarray(gsum, BF16), eg=jnp.asarray(eg, BF16), eb=jnp.asarray(eb, BF16),
                bdm=jnp.asarray(bdm, BF16))


def _rotary_table(pos, rows):
    half = ROPE_DIM // 2
    inv_freq = 1.0 / (ROPE_THETA ** (jnp.arange(half, dtype=F32) / half))
    ang = pos.astype(F32)[:, None] * inv_freq[None, :]
    cos, sin = jnp.cos(ang), jnp.sin(ang)
    t = pos.shape[0]
    cos2 = jnp.concatenate([cos, cos], axis=1)
    sin2 = jnp.concatenate([-sin, sin], axis=1)
    z = lambda n: jnp.zeros((t, n), F32)
    tail = HEAD_PAD - NOPE_DIM - ROPE_DIM
    tab = jnp.concatenate([jnp.ones((t, NOPE_DIM), F32), cos2, z(tail), z(NOPE_DIM), sin2, z(tail),
                           cos2, z(LANES - ROPE_DIM), sin2, z(LANES - ROPE_DIM)], axis=1)
    return jnp.tile(tab, (rows // t, 1)) if rows > t else tab


def _state_to_blockdiag(s):
    b = s.shape[0]
    s = s.reshape(b, N_GROUPS, GROUP, GDN_DK, GDN_DV)
    eye = jnp.eye(GROUP, dtype=s.dtype)
    bdiag = jnp.einsum('bghij,hk->bghikj', s, eye)
    return bdiag.reshape(b, N_GROUPS, GROUP_W, GROUP_W)


def _blockdiag_to_state(sb):
    b = sb.shape[0]
    s6 = sb.reshape(b, N_GROUPS, GROUP, GDN_DK, GROUP, GDN_DV)
    heads = [s6[:, :, h, :, h, :] for h in range(GROUP)]
    return jnp.stack(heads, axis=2).reshape(b, GDN_HEADS, GDN_DK, GDN_DV)


def _run_trunk(x, pos, past_ckv, past_kpe, s_gdn, s_conv, weights, consts):
    b, t, _ = x.shape
    n = b * t
    rot = _rotary_table(pos, max(t, min(TOK_TILE, n)))
    xf = x.reshape(n, D_MODEL)
    ckv_rows, kpe_rows, states, conv_bufs = [], [], [], []
    depth = len(weights)
    for l, lw in enumerate(weights):
        if s_gdn is None:
            cbuf8 = jnp.zeros((b, 8, GDN_CONV_DIM), F32)
            s0_bd = jnp.zeros((b, N_GROUPS, GROUP_W, GROUP_W), F32)
        else:
            cbuf8 = jnp.pad(s_conv[l], ((0, 0), (8 - (CONV_W - 1), 0), (0, 0)))
            s0_bd = _state_to_blockdiag(s_gdn[l])
        xf, qkv, z, q, c_new, kpe_new, gb = _pre_call(xf, lw, rot, t)
        k_new, v_new = _kv_call(c_new, kpe_new, lw, v_transposed=past_ckv is None)
        q3 = q.reshape(b, t, -1)
        k3 = k_new.reshape(b, t, -1)
        if past_ckv is None:
            m_out = _attn_prompt_call(q3, k3, v_new)
        else:
            m_out = _attn_sample_call(q3, past_ckv[l], past_kpe[l], k3, v_new.reshape(b, t, -1), lw)
        qkv3 = qkv.reshape(b, t, GDN_CONV_DIM)
        g_out, s_bd = _gdn_call(qkv3, z.reshape(b, t, GDN_VW), gb.reshape(b, t, LANES), cbuf8, s0_bd, lw,
                                consts)
        final = l == depth - 1
        xf = _post_call(xf, g_out.reshape(n, GDN_VW), m_out.reshape(n, MLA_OUT), lw,
                        weights[-1]['nfinal'], final)
        ckv_rows.append(c_new.reshape(b, t, KV_RANK))
        kpe_rows.append(kpe_new.reshape(b, t, ROPE_DIM))
        states.append(_blockdiag_to_state(s_bd))
        conv_bufs.append(qkv3[:, t - (CONV_W - 1):, :])
    return (xf.reshape(b, t, D_MODEL), jnp.stack(ckv_rows), jnp.stack(kpe_rows), jnp.stack(states),
            jnp.stack(conv_bufs))


def kernel(x_prompt, x_sample, cache_mla_ckv, cache_mla_krope, state_gdn, state_gdn_conv,
           norm_ffn1, w_ffn1_gate, w_ffn1_up, w_ffn1_down, norm_mix, w_in, gdn_conv_w, gdn_a_log,
           gdn_dt_bias, gdn_norm_w, mla_q_norm, mla_kv_norm, w_uq, w_ukv, w_out, norm_ffn2,
           w_ffn2_gate, w_ffn2_up, w_ffn2_down, norm_final):
    params = dict(norm_ffn1=norm_ffn1, w_ffn1_gate=w_ffn1_gate, w_ffn1_up=w_ffn1_up, w_ffn1_down=w_ffn1_down,
                  norm_mix=norm_mix, w_in=w_in, gdn_conv_w=gdn_conv_w, gdn_a_log=gdn_a_log,
                  gdn_dt_bias=gdn_dt_bias, gdn_norm_w=gdn_norm_w, mla_q_norm=mla_q_norm,
                  mla_kv_norm=mla_kv_norm, w_uq=w_uq, w_ukv=w_ukv, w_out=w_out, norm_ffn2=norm_ffn2,
                  w_ffn2_gate=w_ffn2_gate, w_ffn2_up=w_ffn2_up, w_ffn2_down=w_ffn2_down)
    depth = w_in.shape[0]
    weights = [_layer_weights(l, params) for l in range(depth)]
    weights[-1]['nfinal'] = norm_final.reshape(1, D_MODEL)
    consts = _gdn_consts()
    t_p = x_prompt.shape[1]
    out_p = _run_trunk(x_prompt, jnp.arange(t_p, dtype=jnp.int32), None, None, None, None, weights, consts)
    pos_s = cache_mla_ckv.shape[2] + jnp.arange(x_sample.shape[1], dtype=jnp.int32)
    out_s = _run_trunk(x_sample, pos_s, cache_mla_ckv, cache_mla_krope, state_gdn, state_gdn_conv,
                       weights, consts)
    return (out_p[0], out_s[0]) + out_p[1:] + out_s[1:]
```

```python
import functools

import numpy as np
import jax
import jax.numpy as jnp
from jax import lax
from jax.experimental import pallas as pl
from jax.experimental.pallas import tpu as pltpu

F32 = jnp.float32
BF16 = jnp.bfloat16

D_MODEL = 1024
D_FF = 2816
NORM_EPS = 1e-6
L2_EPS = 1e-6
CHUNK = 64
GDN_HEADS = 8
GDN_DK = 64
GDN_DV = 64
CONV_W = 4
GDN_QK = GDN_HEADS * GDN_DK
GDN_VW = GDN_HEADS * GDN_DV
GDN_CONV_DIM = 2 * GDN_QK + GDN_VW
MLA_HEADS = 8
Q_RANK = 256
KV_RANK = 128
NOPE_DIM = 64
ROPE_DIM = 32
V_DIM = 64
ROPE_THETA = 10000.0
MLA_OUT = MLA_HEADS * V_DIM
MASK_VALUE = -1e30

LANES = 128
HEAD_PAD = 128
MAIN_COLS = GDN_CONV_DIM + GDN_VW + Q_RANK + KV_RANK
FF_CHUNK = 512
GROUP = 2
GROUP_W = GROUP * GDN_DK
N_GROUPS = GDN_HEADS // GROUP
G_LANE0 = 32
B_LANE0 = 40
VMEM_LIMIT = 56 * 1024 * 1024
TOK_TILE = 512
POST_TILE = 512
KV_TILE = 1024
ATT_TILE = 256
ATT_HEADS = 8
GDN_TB = 256
GDN_NB = 2
INV_BASE = 8


def _dot(a, b):
    return jnp.dot(a, b, preferred_element_type=F32)


def _dot_nt(a, b):
    return lax.dot_general(a, b, (((1,), (1,)), ((), ())), preferred_element_type=F32)


def _dot_tn(a, b):
    return lax.dot_general(a, b, (((0,), (0,)), ((), ())), preferred_element_type=F32)


def _split_dot(x, w01, passes, left=False):
    acc = None
    r = x
    for i in range(passes):
        part = r.astype(BF16)
        d = _dot(w01, part) if left else _dot(part, w01)
        acc = d if acc is None else acc + d
        if i + 1 < passes:
            r = r - part.astype(F32)
    return acc


def _rms(x, w):
    return x * lax.rsqrt(jnp.mean(x * x, axis=-1, keepdims=True) + NORM_EPS) * w


def _silu(x):
    return x * jax.nn.sigmoid(x)


def _ffn_residual(x, nw, wg_ref, wu_ref, wd_ref):
    xn = _rms(x, nw).astype(BF16)
    acc = None
    for c0 in range(0, D_FF, FF_CHUNK):
        c1 = min(c0 + FF_CHUNK, D_FF)
        g = _dot(xn, wg_ref[:, c0:c1])
        u = _dot(xn, wu_ref[:, c0:c1])
        hid = (_silu(g) * u).astype(BF16)
        d = _dot(hid, wd_ref[c0:c1, :])
        acc = d if acc is None else acc + d
    return x + 0.5 * acc


def _causal_conv(x, tail, cw):
    row8 = lax.broadcasted_iota(jnp.int32, (8, x.shape[1]), 0)
    conv = x * cw[CONV_W - 1:CONV_W, :]
    for sh in range(1, CONV_W):
        xr = pltpu.roll(x, sh, axis=0)
        head = jnp.where(row8 < sh, pltpu.roll(tail, sh, axis=0), xr[:8])
        conv = conv + jnp.concatenate([head, xr[8:]], axis=0) * cw[CONV_W - 1 - sh:CONV_W - sh, :]
    return conv


def _pre_kernel(x_ref, nf_ref, wg_ref, wu_ref, wd_ref, nm_ref, wmain_ref, wsm_ref, gvec_ref,
                qn_ref, kvn_ref, wq2_ref, rot_ref,
                xo_ref, qkv_ref, z_ref, q_ref, c_ref, kpe_ref, gb_ref):
    x1 = _ffn_residual(x_ref[...], nf_ref[...], wg_ref, wu_ref, wd_ref)
    xo_ref[...] = x1
    h = _rms(x1, nm_ref[...]).astype(BF16)
    qkv_ref[...] = _dot(h, wmain_ref[:, :GDN_CONV_DIM])
    c0 = GDN_CONV_DIM + GDN_VW
    z_ref[...] = _dot(h, wmain_ref[:, GDN_CONV_DIM:c0])
    lat = _dot(h, wmain_ref[:, c0:])
    cq = lat[:, :Q_RANK]
    ckv = lat[:, Q_RANK:]
    rot = rot_ref[...]
    cosq, sinq = rot[:, :LANES], rot[:, LANES:2 * LANES]
    cosk, sink = rot[:, 2 * LANES:3 * LANES], rot[:, 3 * LANES:]
    cqn = _rms(cq, qn_ref[...]).astype(BF16)
    nq = MLA_HEADS * HEAD_PAD
    cos8 = jnp.concatenate([cosq] * MLA_HEADS, axis=1)
    sin8 = jnp.concatenate([sinq] * MLA_HEADS, axis=1)
    q_ref[...] = (_dot(cqn, wq2_ref[:, :nq]) * cos8 + _dot(cqn, wq2_ref[:, nq:]) * sin8).astype(BF16)
    c_ref[...] = _rms(ckv, kvn_ref[...])
    small = _dot(h, wsm_ref[...])
    s1, s2 = small[:, :LANES], small[:, LANES:]
    kpe_ref[...] = (s1 * cosk + s2 * sink)[:, :ROPE_DIM]
    gvec = gvec_ref[...]
    xa = s1 + gvec[1:2, :]
    softplus = jnp.maximum(xa, 0.0) + jnp.log1p(jnp.exp(-jnp.abs(xa)))
    gfull = -jnp.exp(gvec[0:1, :]) * softplus
    bfull = jax.nn.sigmoid(s1)
    lane = lax.broadcasted_iota(jnp.int32, s1.shape, 1)
    is_g = (lane >= G_LANE0) & (lane < G_LANE0 + GDN_HEADS)
    is_b = (lane >= B_LANE0) & (lane < B_LANE0 + GDN_HEADS)
    gb_ref[...] = jnp.where(is_g, gfull, jnp.where(is_b, bfull, 0.0))


def _post_kernel(x_ref, g_ref, m_ref, wo_ref, nf_ref, wg_ref, wu_ref, wd_ref, nfin_ref, y_ref, *, final):
    x = x_ref[...] + _dot(g_ref[...], wo_ref[:GDN_VW, :]) + _dot(m_ref[...], wo_ref[GDN_VW:, :])
    x2 = _ffn_residual(x, nf_ref[...], wg_ref, wu_ref, wd_ref)
    y_ref[...] = _rms(x2, nfin_ref[...]) if final else x2


def _const_spec(shape):
    nd = len(shape)
    return pl.BlockSpec(shape, lambda *_: (0,) * nd, pipeline_mode=pl.Buffered(1))


def _params(sem):
    return pltpu.CompilerParams(dimension_semantics=sem, vmem_limit_bytes=VMEM_LIMIT)


def _pre_call(x, lw, rot, seq_len):
    n = x.shape[0]
    tm = min(TOK_TILE, n)
    rot_blocks = rot.shape[0] // tm
    row = lambda w: pl.BlockSpec((tm, w), lambda i: (i, 0))
    consts = [lw['nf1'], lw['wg1'], lw['wu1'], lw['wd1'], lw['nmix'], lw['wmain'], lw['wsm'], lw['gvec'],
              lw['qnorm'], lw['kvnorm'], lw['wq2']]
    in_specs = [row(D_MODEL)] + [_const_spec(c.shape) for c in consts]
    in_specs.append(pl.BlockSpec((tm, 4 * LANES), lambda i: (i % rot_blocks, 0)))
    out_shape = (
        jax.ShapeDtypeStruct((n, D_MODEL), F32),
        jax.ShapeDtypeStruct((n, GDN_CONV_DIM), F32),
        jax.ShapeDtypeStruct((n, GDN_VW), F32),
        jax.ShapeDtypeStruct((n, MLA_HEADS * HEAD_PAD), BF16),
        jax.ShapeDtypeStruct((n, KV_RANK), F32),
        jax.ShapeDtypeStruct((n, ROPE_DIM), F32),
        jax.ShapeDtypeStruct((n, LANES), F32),
    )
    out_specs = (row(D_MODEL), row(GDN_CONV_DIM), row(GDN_VW), row(MLA_HEADS * HEAD_PAD), row(KV_RANK),
                 row(ROPE_DIM), row(LANES))
    return pl.pallas_call(
        _pre_kernel, grid=(n // tm,), in_specs=in_specs, out_specs=out_specs, out_shape=out_shape,
        compiler_params=_params(("parallel",)), name="ffn_proj",
    )(x, *consts, rot)


def _post_call(x, g_out, m_out, lw, nfin, final):
    n = x.shape[0]
    tm = min(POST_TILE, n)
    row = lambda w: pl.BlockSpec((tm, w), lambda i: (i, 0))
    consts = [lw['wo'], lw['nf2'], lw['wg2'], lw['wu2'], lw['wd2'], nfin]
    in_specs = [row(D_MODEL), row(GDN_VW), row(MLA_OUT)] + [_const_spec(c.shape) for c in consts]
    return pl.pallas_call(
        functools.partial(_post_kernel, final=final), grid=(n // tm,), in_specs=in_specs,
        out_specs=row(D_MODEL), out_shape=jax.ShapeDtypeStruct((n, D_MODEL), F32),
        compiler_params=_params(("parallel",)), name="mix_ffn",
    )(x, g_out, m_out, *consts)


def _kv_kernel(c_ref, kpe_ref, wk_ref, wv_ref, k_ref, v_ref, *, v_transposed):
    c = c_ref[...].astype(BF16)
    ck = jnp.concatenate([c, kpe_ref[...].astype(BF16)], axis=1)
    k_ref[...] = _dot(ck, wk_ref[...]).astype(BF16)
    if v_transposed:
        for i in range(v_ref.shape[0]):
            v_ref[i] = _dot_nt(wv_ref[...], c[i * ATT_TILE:(i + 1) * ATT_TILE]).astype(BF16)
    else:
        v_ref[...] = _dot(c, wv_ref[...]).astype(BF16)


def _kv_call(c, kpe, lw, v_transposed):
    n = c.shape[0]
    tm = min(KV_TILE, n)
    row = lambda w: pl.BlockSpec((tm, w), lambda i: (i, 0))
    consts = [lw['wke'], lw['wvt'] if v_transposed else lw['wv']]
    if v_transposed:
        v_spec = pl.BlockSpec((tm // ATT_TILE, MLA_OUT, ATT_TILE), lambda i: (i, 0, 0))
        v_shape = jax.ShapeDtypeStruct((n // ATT_TILE, MLA_OUT, ATT_TILE), BF16)
    else:
        v_spec = row(MLA_OUT)
        v_shape = jax.ShapeDtypeStruct((n, MLA_OUT), BF16)
    return pl.pallas_call(
        functools.partial(_kv_kernel, v_transposed=v_transposed), grid=(n // tm,),
        in_specs=[row(KV_RANK), row(ROPE_DIM)] + [_const_spec(w.shape) for w in consts],
        out_specs=(row(MLA_HEADS * HEAD_PAD), v_spec),
        out_shape=(jax.ShapeDtypeStruct((n, MLA_HEADS * HEAD_PAD), BF16), v_shape),
        compiler_params=_params(("parallel",)), name="kv_up",
    )(c, kpe, *consts)


_ATT_SCALE = float((NOPE_DIM + ROPE_DIM) ** -0.5)
_ATT_SCALE_LOG2 = float(_ATT_SCALE * np.log2(np.e))


def _merge_heads(outs):
    lane = lax.broadcasted_iota(jnp.int32, outs[0].shape, 1)
    return jnp.where(lane < V_DIM, outs[0], outs[1])


def _attn_prompt_kernel(q_ref, k_ref, vt_ref, o_ref, s_scr, p_scr, m_scr, l_scr, acc_scr, *, tile, nh):
    qi = pl.program_id(2)
    heads = tuple(range(nh))
    pair_rows = [slice((h // 2) * 2 * V_DIM, (h // 2 + 1) * 2 * V_DIM) for h in heads]
    hsl = [slice(h * HEAD_PAD, (h + 1) * HEAD_PAD) for h in heads]

    def scores(j, h):
        k0 = pl.multiple_of(j * tile, tile)
        return _dot_nt(k_ref[0, pl.ds(k0, tile), hsl[h]], q_ref[0, :, hsl[h]]) * _ATT_SCALE_LOG2

    def pv(j, h):
        return _dot(vt_ref[j, pair_rows[h], :], p_scr[h])

    def softmax(h, s, acc_in):
        m_prev = m_scr[h]
        m_new = jnp.maximum(m_prev, jnp.max(s, axis=0, keepdims=True))
        alpha = jnp.exp2(m_prev - m_new)
        p = jnp.exp2(s - m_new)
        l_scr[h] = alpha * l_scr[h] + jnp.sum(p, axis=0, keepdims=True)
        acc_scr[h] = alpha * acc_in
        m_scr[h] = m_new
        p_scr[h] = p.astype(BF16)

    for h in heads:
        s_scr[h] = scores(0, h)
        p_scr[h] = jnp.zeros((tile, tile), BF16)
        m_scr[h] = jnp.full((1, tile), MASK_VALUE, F32)
        l_scr[h] = jnp.zeros((1, tile), F32)
        acc_scr[h] = jnp.zeros((2 * V_DIM, tile), F32)

    def body(j, carry):
        jm1 = jnp.maximum(j - 1, 0)
        accs = [acc_scr[h] + pv(jm1, h) for h in heads]
        s_next = [scores(j + 1, h) for h in heads]
        for h in heads:
            softmax(h, s_scr[h], accs[h])
        for h in heads:
            s_scr[h] = s_next[h]
        return carry

    lax.fori_loop(0, qi, body, 0)

    kc = lax.broadcasted_iota(jnp.int32, (tile, tile), 0) // CHUNK
    qc = lax.broadcasted_iota(jnp.int32, (tile, tile), 1) // CHUNK
    jm1 = jnp.maximum(qi - 1, 0)
    accs = [acc_scr[h] + pv(jm1, h) for h in heads]
    for h in heads:
        softmax(h, jnp.where(kc <= qc, s_scr[h], MASK_VALUE), accs[h])
    outs = [(acc_scr[h] + pv(qi, h)) / l_scr[h] for h in heads]
    vrow = lax.broadcasted_iota(jnp.int32, outs[0].shape, 0)
    merged = [jnp.where(vrow < V_DIM, outs[h], outs[h + 1]) for h in heads[::2]]
    o_ref[0] = jnp.concatenate(merged, axis=0).T.astype(BF16)


def _attn_prompt_call(q, k, vt):
    b, t, _ = q.shape
    tile = ATT_TILE
    nh = ATT_HEADS
    groups = MLA_HEADS // nh
    nk = t // tile
    return pl.pallas_call(
        functools.partial(_attn_prompt_kernel, tile=tile, nh=nh), grid=(b, groups, nk),
        in_specs=[pl.BlockSpec((1, tile, nh * HEAD_PAD), lambda bi, p, i: (bi, i, p)),
                  pl.BlockSpec((1, t, nh * HEAD_PAD), lambda bi, p, i: (bi, 0, p)),
                  pl.BlockSpec((nk, nh * V_DIM, tile), lambda bi, p, i: (bi, p, 0))],
        out_specs=pl.BlockSpec((1, tile, nh * V_DIM), lambda bi, p, i: (bi, i, p)),
        out_shape=jax.ShapeDtypeStruct((b, t, MLA_OUT), BF16),
        scratch_shapes=[pltpu.VMEM((nh, tile, tile), F32), pltpu.VMEM((nh, tile, tile), BF16),
                        pltpu.VMEM((nh, 1, tile), F32), pltpu.VMEM((nh, 1, tile), F32),
                        pltpu.VMEM((nh, 2 * V_DIM, tile), F32)],
        compiler_params=_params(("parallel", "parallel", "arbitrary")), name="attn_prompt",
    )(q, k, vt)


def _attn_sample_kernel(q_ref, cp_ref, kpp_ref, kn_ref, vn_ref, wabs_ref, wv_ref, o_ref, *, tq, past):
    pad = LANES - tq
    heads = range(MLA_HEADS)
    hsl = [slice(h * HEAD_PAD, (h + 1) * HEAD_PAD) for h in heads]
    cp = cp_ref[0].astype(BF16)
    ckp = jnp.concatenate([cp, kpp_ref[0].astype(BF16)], axis=1)
    qa = jnp.concatenate([_dot(q_ref[0, :, hsl[h]], wabs_ref[h]).astype(BF16) for h in heads], axis=0)
    q_chunk = jnp.concatenate([(past + lax.broadcasted_iota(jnp.int32, (tq, 1), 0)) // CHUNK] * MLA_HEADS, axis=0)
    s_p = _dot_nt(qa, ckp) * _ATT_SCALE
    kc_p = lax.broadcasted_iota(jnp.int32, s_p.shape, 1) // CHUNK
    s_p = jnp.where(kc_p <= q_chunk, s_p, MASK_VALUE)
    s_n = []
    for h in heads:
        kn = jnp.concatenate([kn_ref[0, :, hsl[h]], jnp.zeros((pad, HEAD_PAD), BF16)], axis=0)
        s_n.append(_dot_nt(q_ref[0, :, hsl[h]], kn))
    s_n = jnp.concatenate(s_n, axis=0) * _ATT_SCALE
    col = lax.broadcasted_iota(jnp.int32, s_n.shape, 1)
    s_n = jnp.where((col < tq) & ((past + col) // CHUNK <= q_chunk), s_n, MASK_VALUE)
    m = jnp.maximum(jnp.max(s_p, axis=-1, keepdims=True), jnp.max(s_n, axis=-1, keepdims=True))
    p_p = jnp.exp(s_p - m)
    e_n = jnp.exp(s_n - m)
    p_n = e_n.astype(BF16)
    l = jnp.sum(p_p, axis=-1, keepdims=True) + jnp.sum(e_n, axis=-1, keepdims=True)
    pc = _dot(p_p.astype(BF16), cp).astype(BF16)
    pairs = []
    for pr in range(MLA_HEADS // 2):
        vs = slice(pr * 2 * V_DIM, (pr + 1) * 2 * V_DIM)
        vn = jnp.concatenate([vn_ref[0, :, vs], jnp.zeros((pad, 2 * V_DIM), BF16)], axis=0)
        outs = []
        for h in (2 * pr, 2 * pr + 1):
            rs = slice(h * tq, (h + 1) * tq)
            outs.append((_dot(pc[rs], wv_ref[:, vs]) + _dot(p_n[rs], vn)) / l[rs])
        pairs.append(_merge_heads(outs))
    o_ref[0] = jnp.concatenate(pairs, axis=1).astype(BF16)


def _attn_sample_call(q, c_past, kpe_past, layer, k_new, v_new, lw):
    b, tq, _ = q.shape
    past = c_past.shape[2]
    blk = lambda rows, w: pl.BlockSpec((1, rows, w), lambda bi: (bi, 0, 0))
    cache = lambda w: pl.BlockSpec((None, 1, past, w), lambda bi: (layer, bi, 0, 0))
    consts = [lw['wabs'], lw['wv']]
    return pl.pallas_call(
        functools.partial(_attn_sample_kernel, tq=tq, past=past), grid=(b,),
        in_specs=[blk(tq, MLA_HEADS * HEAD_PAD), cache(KV_RANK), cache(ROPE_DIM),
                  blk(tq, MLA_HEADS * HEAD_PAD), blk(tq, MLA_OUT)] + [_const_spec(w.shape) for w in consts],
        out_specs=blk(tq, MLA_OUT),
        out_shape=jax.ShapeDtypeStruct((b, tq, MLA_OUT), BF16),
        compiler_params=_params(("parallel",)), name="attn_sample",
    )(q, c_past, kpe_past, k_new, v_new, *consts)


def _gdn_kernel(x_ref, z_ref, gb_ref, cbuf_ref, s0_ref, cw_ref, nw_ref, gsum_ref, eg_ref, eb_ref, bdm_ref,
                dup_ref, dupt_ref, o_ref, sout_ref, s_scr, tail_scr, *, nb, tb, nt):
    t = pl.program_id(1)
    rows = max(tb, CHUNK)

    @pl.when(t == 0)
    def _():
        for bi in range(nb):
            for p in range(N_GROUPS):
                s_scr[bi, p] = _split_dot(s0_ref[bi, p], dup_ref[...], 3) * bdm_ref[...].astype(F32)
        tail_scr[...] = cbuf_ref[...]

    cw = cw_ref[...]
    gsum = gsum_ref[...]
    bdm = bdm_ref[...]
    bdm_f32 = bdm.astype(F32)

    def bd(m):
        mb = m.astype(BF16)
        return jnp.concatenate([mb] * GROUP, axis=0) * bdm

    ri = lax.broadcasted_iota(jnp.int32, (CHUNK, GROUP_W), 0)
    ci = lax.broadcasted_iota(jnp.int32, (CHUNK, GROUP_W), 1) % CHUNK
    causal = ri >= ci
    strict = ri > ci
    eye = jnp.where(ri == ci, 1.0, 0.0).astype(F32)
    tri = jnp.where(lax.broadcasted_iota(jnp.int32, (CHUNK, CHUNK), 0)
                    >= lax.broadcasted_iota(jnp.int32, (CHUNK, CHUNK), 1), 1.0, 0.0).astype(BF16)

    def same_block(size):
        return (ri // size) == (ci // size)

    def precompute(units):
        def decay_and_products():
            for u in units:
                gcum = _split_dot(u['g'], tri, 3, left=True)
                grow = jnp.sum(jnp.where(ri <= ci, u['g'], 0.0), axis=0, keepdims=True)
                eg = jnp.exp(gcum)
                glast = gcum[CHUNK - 1:CHUNK, :]
                kb = u['k'] * u['b']
                u['qd'] = u['q'] * eg
                u['kd'] = u['k'] * jnp.exp(glast - gcum)
                u['eglast'] = jnp.exp(glast)
                u['vb'] = u['v'] * u['b']
                u['kg'] = kb * eg
                decay = jnp.where(causal, jnp.exp(jnp.where(causal, gcum - grow, 0.0)), 0.0)
                kq = _dot_nt(jnp.concatenate([kb, u['q']], axis=0).astype(BF16), bd(u['k']))
                u['attn'] = kq[CHUNK:] * decay
                u['lower'] = jnp.where(strict, kq[:CHUNK] * decay, 0.0)
                u['m'] = -jnp.where(same_block(INV_BASE), u['lower'], 0.0)
                u['inv'] = eye + u['m']

        def square():
            for u in units:
                u['m'] = _dot(u['m'].astype(BF16), bd(u['m']))

        def square_and_extend():
            for u in units:
                r = _dot(jnp.concatenate([u['m'], u['inv']], axis=0).astype(BF16), bd(u['m']))
                u['m'] = r[:CHUNK]
                u['inv'] = u['inv'] + r[CHUNK:]

        def extend():
            for u in units:
                u['inv'] = u['inv'] + _dot(u['inv'].astype(BF16), bd(u['m']))

        def couple(size):
            def left():
                for u in units:
                    off = jnp.where(same_block(2 * size) & ~same_block(size), u['lower'], 0.0)
                    u['m'] = _dot(off.astype(BF16), bd(u['inv']))

            def right():
                for u in units:
                    u['inv'] = u['inv'] - _dot(u['inv'].astype(BF16), bd(u['m']))
            return [left, right]

        def solve():
            for u in units:
                invb = u['inv'].astype(BF16)
                u['value'] = _dot(invb, bd(u['vb']))
                u['kcum'] = _dot(invb, bd(u['kg']))

        stages = [decay_and_products, square, square_and_extend, extend]
        size = INV_BASE
        while size < CHUNK:
            stages += couple(size)
            size *= 2
        return stages + [solve]

    n_chunks = rows // CHUNK
    units = [[] for _ in range(nb)]

    conv_rows = {}

    def conv_piece(bi, c):
        r0 = c * CHUNK
        r1 = min(r0 + CHUNK, tb)
        if c == 0:
            tail = tail_scr[bi]
            tail_scr[bi] = x_ref[bi, tb - 8:tb, :]
        else:
            tail = x_ref[bi, r0 - 8:r0, :]
        conv_rows[(bi, c)] = _silu(_causal_conv(x_ref[bi, r0:r1, :], tail, cw))

    def front(bi):
        pieces = [conv_rows[(bi, c)] for c in range(n_chunks)]
        y = jnp.concatenate(pieces, axis=0) if n_chunks > 1 else pieces[0]
        gb = gb_ref[bi]
        if rows > tb:
            y = jnp.concatenate([y, jnp.zeros((rows - tb, GDN_CONV_DIM), F32)], axis=0)
            gb = jnp.concatenate([gb, jnp.zeros((rows - tb, LANES), F32)], axis=0)
        q, k, v = y[:, :GDN_QK], y[:, GDN_QK:2 * GDN_QK], y[:, 2 * GDN_QK:]
        qn = q * lax.rsqrt(_split_dot(q * q, gsum, 2) + L2_EPS) * (GDN_DK ** -0.5)
        kn = k * lax.rsqrt(_split_dot(k * k, gsum, 2) + L2_EPS)
        ge = _split_dot(gb, eg_ref[...], 3)
        be = _split_dot(gb, eb_ref[...], 3)
        for c in range(n_chunks):
            rs = slice(c * CHUNK, (c + 1) * CHUNK)
            for p in range(N_GROUPS):
                ls = slice(p * GROUP_W, (p + 1) * GROUP_W)
                units[bi].append(dict(bi=bi, c=c, p=p, q=qn[rs, ls], k=kn[rs, ls], v=v[rs, ls], g=ge[rs, ls],
                                      b=be[rs, ls]))

    state = {}
    outs = {}

    def rec_read(bi, c):
        for u in units[bi]:
            if u['c'] == c:
                s = state.setdefault((bi, u['p']), s_scr[bi, u['p']]) if c == 0 else state[(bi, u['p'])]
                u['r'] = _dot(jnp.concatenate([u['kcum'], u['qd']], axis=0).astype(BF16), s.astype(BF16))

    def rec_update(bi, c):
        for u in units[bi]:
            if u['c'] == c:
                v_new = u['value'] - u['r'][:CHUNK]
                outs[(bi, c, u['p'])] = u['r'][CHUNK:] + _dot(u['attn'].astype(BF16), bd(v_new))
                upd = _dot_tn(u['kd'].astype(BF16), v_new.astype(BF16))
                key = (bi, u['p'])
                state[key] = state[key] * u['eglast'] + upd * bdm_f32

    def finish(bi):
        for p in range(N_GROUPS):
            s_scr[bi, p] = state[(bi, p)]
        o_rows = [jnp.concatenate([outs[(bi, c, p)] for p in range(N_GROUPS)], axis=1) for c in range(n_chunks)]
        o = jnp.concatenate(o_rows, axis=0) if n_chunks > 1 else o_rows[0]
        o = o[:tb]
        ms = _split_dot(o * o, gsum, 2) * (1.0 / GDN_DV)
        o_ref[bi] = (o * lax.rsqrt(ms + NORM_EPS) * nw_ref[...] * _silu(z_ref[bi])).astype(BF16)

    def bind(fn, *args):
        return lambda: fn(*args)

    fronts = [[bind(front, bi)] for bi in range(nb)]
    pres = [precompute(units[bi]) for bi in range(nb)]
    recs = [[f for c in range(n_chunks) for f in (bind(rec_read, bi, c), bind(rec_update, bi, c))]
            + [bind(finish, bi)] for bi in range(nb)]

    def interleave(*seqs):
        seqs = [s for s in seqs if s]
        n = max(len(s) for s in seqs)
        keyed = [((i + 0.5) * n / len(s), k, f) for k, s in enumerate(seqs) for i, f in enumerate(s)]
        return [f for _, _, f in sorted(keyed, key=lambda e: (e[0], e[1]))]

    program = []
    for bi in range(nb):
        program += [bind(conv_piece, bi, c) for c in range(n_chunks)] + fronts[bi] + pres[bi]
    program += interleave(*recs)
    for thunk in program:
        thunk()

    @pl.when(t == nt - 1)
    def _():
        for bi in range(nb):
            for p in range(N_GROUPS):
                sout_ref[bi, p] = _split_dot(s_scr[bi, p], dupt_ref[...], 3)


def _gdn_call(qkv, z, gb, cbuf8, s0, lw, consts):
    b, t, _ = qkv.shape
    tb = min(GDN_TB, t)
    nt = t // tb
    nb = GDN_NB
    blk = lambda w: pl.BlockSpec((nb, tb, w), lambda bi, ti: (bi, ti, 0))
    per_b = lambda shape: pl.BlockSpec((nb,) + shape, lambda bi, ti: (bi,) + (0,) * len(shape))
    cs = [lw['convw'], lw['gnw'], consts['gsum'], consts['eg'], consts['eb'], consts['bdm'], consts['dup'],
          consts['dupt']]
    state_spec = per_b((N_GROUPS, GROUP_W, GDN_DV))
    return pl.pallas_call(
        functools.partial(_gdn_kernel, nb=nb, tb=tb, nt=nt), grid=(b // nb, nt),
        in_specs=[blk(GDN_CONV_DIM), blk(GDN_VW), blk(LANES), per_b((8, GDN_CONV_DIM)), state_spec]
        + [_const_spec(c.shape) for c in cs],
        out_specs=(blk(GDN_VW), state_spec),
        out_shape=(jax.ShapeDtypeStruct((b, t, GDN_VW), BF16),
                   jax.ShapeDtypeStruct((b, N_GROUPS, GROUP_W, GDN_DV), F32)),
        scratch_shapes=[pltpu.VMEM((nb, N_GROUPS, GROUP_W, GROUP_W), F32),
                        pltpu.VMEM((nb, 8, GDN_CONV_DIM), F32)],
        compiler_params=_params(("parallel", "arbitrary")), name="gdn",
    )(qkv, z, gb, cbuf8, s0, *cs)


def _layer_weights(l, p):
    c1 = GDN_CONV_DIM
    c2 = c1 + GDN_VW
    c3 = c2 + GDN_HEADS
    c4 = c3 + GDN_HEADS
    c5 = c4 + Q_RANK
    c6 = c5 + KV_RANK
    w_in = p['w_in'][l]
    half = ROPE_DIM // 2
    kpe_w = w_in[:, c6:]
    kpe_sw = jnp.concatenate([kpe_w[:, half:], kpe_w[:, :half]], axis=1)
    zcol = lambda n: jnp.zeros((D_MODEL, n), F32)
    wsm = jnp.concatenate([kpe_w, w_in[:, c2:c3], w_in[:, c3:c4], zcol(LANES - B_LANE0 - GDN_HEADS),
                           kpe_sw, zcol(LANES - ROPE_DIM)], axis=1)
    wmain = jnp.concatenate([w_in[:, :c2], w_in[:, c4:c6]], axis=1)
    gvec = jnp.zeros((2, LANES), F32)
    gvec = gvec.at[0, G_LANE0:G_LANE0 + GDN_HEADS].set(p['gdn_a_log'][l])
    gvec = gvec.at[1, G_LANE0:G_LANE0 + GDN_HEADS].set(p['gdn_dt_bias'][l])
    wq = p['w_uq'][l].reshape(Q_RANK, MLA_HEADS, NOPE_DIM + ROPE_DIM)
    nope, rope = wq[..., :NOPE_DIM], wq[..., NOPE_DIM:]
    rope_sw = jnp.concatenate([rope[..., half:], rope[..., :half]], axis=-1)
    zq = lambda n: jnp.zeros((Q_RANK, MLA_HEADS, n), F32)
    tail = HEAD_PAD - NOPE_DIM - ROPE_DIM
    wq_a = jnp.concatenate([nope, rope, zq(tail)], axis=-1).reshape(Q_RANK, MLA_HEADS * HEAD_PAD)
    wq_b = jnp.concatenate([zq(NOPE_DIM), rope_sw, zq(tail)], axis=-1).reshape(Q_RANK, MLA_HEADS * HEAD_PAD)
    wkv = p['w_ukv'][l].reshape(KV_RANK, MLA_HEADS, NOPE_DIM + V_DIM)
    wk = jnp.concatenate([wkv[..., :NOPE_DIM], jnp.zeros((KV_RANK, MLA_HEADS, HEAD_PAD - NOPE_DIM), F32)],
                         axis=-1).reshape(KV_RANK, MLA_HEADS * HEAD_PAD)
    wv = wkv[..., NOPE_DIM:].reshape(KV_RANK, MLA_OUT)
    we = np.zeros((ROPE_DIM, MLA_HEADS, HEAD_PAD), np.float32)
    for r in range(ROPE_DIM):
        we[r, :, NOPE_DIM + r] = 1.0
    rope_pass = np.zeros((HEAD_PAD - NOPE_DIM, KV_RANK + ROPE_DIM), np.float32)
    for r in range(ROPE_DIM):
        rope_pass[r, KV_RANK + r] = 1.0
    wk_t = jnp.transpose(wkv[..., :NOPE_DIM], (1, 2, 0))
    wabs = jnp.concatenate([jnp.concatenate([wk_t, jnp.zeros((MLA_HEADS, NOPE_DIM, ROPE_DIM), F32)], axis=2),
                            jnp.broadcast_to(rope_pass, (MLA_HEADS,) + rope_pass.shape)], axis=1)
    bf = lambda a: a.astype(BF16)
    r2 = lambda a: a.reshape(1, -1)
    return dict(
        nf1=r2(p['norm_ffn1'][l]), wg1=bf(p['w_ffn1_gate'][l]), wu1=bf(p['w_ffn1_up'][l]),
        wd1=bf(p['w_ffn1_down'][l]), nmix=r2(p['norm_mix'][l]), wmain=bf(wmain), wsm=bf(wsm), gvec=gvec,
        qnorm=r2(p['mla_q_norm'][l]), kvnorm=r2(p['mla_kv_norm'][l]),
        wq2=bf(jnp.concatenate([wq_a, wq_b], axis=1)), wv=bf(wv), wvt=bf(wv.T), wabs=bf(wabs),
        wke=bf(jnp.concatenate([wk, jnp.asarray(we.reshape(ROPE_DIM, MLA_HEADS * HEAD_PAD))], axis=0)),
        wo=bf(p['w_out'][l]), nf2=r2(p['norm_ffn2'][l]), wg2=bf(p['w_ffn2_gate'][l]),
        wu2=bf(p['w_ffn2_up'][l]), wd2=bf(p['w_ffn2_down'][l]),
        convw=p['gdn_conv_w'][l], gnw=jnp.tile(p['gdn_norm_w'][l], GDN_HEADS).reshape(1, GDN_VW),
    )


def _gdn_consts():
    lane = np.arange(GDN_QK)
    gsum = (lane[:, None] // GDN_DK == lane[None, :] // GDN_DK).astype(np.float32)
    eg = np.zeros((LANES, GDN_QK), np.float32)
    eb = np.zeros((LANES, GDN_QK), np.float32)
    for h in range(GDN_HEADS):
        eg[G_LANE0 + h, h * GDN_DK:(h + 1) * GDN_DK] = 1.0
        eb[B_LANE0 + h, h * GDN_DK:(h + 1) * GDN_DK] = 1.0
    gl = np.arange(GROUP_W)
    bdm = (gl[:, None] // GDN_DK == gl[None, :] // GDN_DK).astype(np.float32)
    dup = np.tile(np.eye(GDN_DV, dtype=np.float32), (1, GROUP))
    return dict(gsum=jnp.asarray(gsum, BF16), eg=jnp.asarray(eg, BF16), eb=jnp.asarray(eb, BF16),
                bdm=jnp.asarray(bdm, BF16), dup=jnp.asarray(dup, BF16), dupt=jnp.asarray(dup.T, BF16))


def _rotary_table(pos, rows):
    half = ROPE_DIM // 2
    inv_freq = 1.0 / (ROPE_THETA ** (jnp.arange(half, dtype=F32) / half))
    ang = pos.astype(F32)[:, None] * inv_freq[None, :]
    cos, sin = jnp.cos(ang), jnp.sin(ang)
    t = pos.shape[0]
    cos2 = jnp.concatenate([cos, cos], axis=1)
    sin2 = jnp.concatenate([-sin, sin], axis=1)
    z = lambda n: jnp.zeros((t, n), F32)
    tail = HEAD_PAD - NOPE_DIM - ROPE_DIM
    tab = jnp.concatenate([jnp.ones((t, NOPE_DIM), F32), cos2, z(tail), z(NOPE_DIM), sin2, z(tail),
                           cos2, z(LANES - ROPE_DIM), sin2, z(LANES - ROPE_DIM)], axis=1)
    return jnp.tile(tab, (rows // t, 1)) if rows > t else tab


def _run_trunk(x, pos, past_ckv, past_kpe, s_gdn, s_conv, weights, consts):
    b, t, _ = x.shape
    n = b * t
    rot = _rotary_table(pos, max(t, min(TOK_TILE, n)))
    xf = x.reshape(n, D_MODEL)
    ckv_rows, kpe_rows, states, conv_bufs = [], [], [], []
    depth = len(weights)
    for l, lw in enumerate(weights):
        if s_gdn is None:
            cbuf8 = jnp.zeros((b, 8, GDN_CONV_DIM), F32)
            s0 = jnp.zeros((b, N_GROUPS, GROUP_W, GDN_DV), F32)
        else:
            cbuf8 = jnp.pad(s_conv[l], ((0, 0), (8 - (CONV_W - 1), 0), (0, 0)))
            s0 = s_gdn[l].reshape(b, N_GROUPS, GROUP_W, GDN_DV)
        xf, qkv, z, q, c_new, kpe_new, gb = _pre_call(xf, lw, rot, t)
        k_new, v_new = _kv_call(c_new, kpe_new, lw, v_transposed=past_ckv is None)
        q3 = q.reshape(b, t, -1)
        k3 = k_new.reshape(b, t, -1)
        if past_ckv is None:
            m_out = _attn_prompt_call(q3, k3, v_new)
        else:
            m_out = _attn_sample_call(q3, past_ckv, past_kpe, l, k3, v_new.reshape(b, t, -1), lw)
        qkv3 = qkv.reshape(b, t, GDN_CONV_DIM)
        g_out, s_new = _gdn_call(qkv3, z.reshape(b, t, GDN_VW), gb.reshape(b, t, LANES), cbuf8, s0, lw,
                                 consts)
        final = l == depth - 1
        xf = _post_call(xf, g_out.reshape(n, GDN_VW), m_out.reshape(n, MLA_OUT), lw,
                        weights[-1]['nfinal'], final)
        ckv_rows.append(c_new.reshape(b, t, KV_RANK))
        kpe_rows.append(kpe_new.reshape(b, t, ROPE_DIM))
        states.append(s_new.reshape(b, GDN_HEADS, GDN_DK, GDN_DV))
        conv_bufs.append(qkv3[:, t - (CONV_W - 1):, :])
    return (xf.reshape(b, t, D_MODEL), jnp.stack(ckv_rows), jnp.stack(kpe_rows), jnp.stack(states),
            jnp.stack(conv_bufs))


def kernel(x_prompt, x_sample, cache_mla_ckv, cache_mla_krope, state_gdn, state_gdn_conv,
           norm_ffn1, w_ffn1_gate, w_ffn1_up, w_ffn1_down, norm_mix, w_in, gdn_conv_w, gdn_a_log,
           gdn_dt_bias, gdn_norm_w, mla_q_norm, mla_kv_norm, w_uq, w_ukv, w_out, norm_ffn2,
           w_ffn2_gate, w_ffn2_up, w_ffn2_down, norm_final):
    params = dict(norm_ffn1=norm_ffn1, w_ffn1_gate=w_ffn1_gate, w_ffn1_up=w_ffn1_up, w_ffn1_down=w_ffn1_down,
                  norm_mix=norm_mix, w_in=w_in, gdn_conv_w=gdn_conv_w, gdn_a_log=gdn_a_log,
                  gdn_dt_bias=gdn_dt_bias, gdn_norm_w=gdn_norm_w, mla_q_norm=mla_q_norm,
                  mla_kv_norm=mla_kv_norm, w_uq=w_uq, w_ukv=w_ukv, w_out=w_out, norm_ffn2=norm_ffn2,
                  w_ffn2_gate=w_ffn2_gate, w_ffn2_up=w_ffn2_up, w_ffn2_down=w_ffn2_down)
    depth = w_in.shape[0]
    weights = [_layer_weights(l, params) for l in range(depth)]
    weights[-1]['nfinal'] = norm_final.reshape(1, D_MODEL)
    consts = _gdn_consts()
    t_p = x_prompt.shape[1]
    out_p = _run_trunk(x_prompt, jnp.arange(t_p, dtype=jnp.int32), None, None, None, None, weights, consts)
    pos_s = cache_mla_ckv.shape[2] + jnp.arange(x_sample.shape[1], dtype=jnp.int32)
    out_s = _run_trunk(x_sample, pos_s, cache_mla_ckv, cache_mla_krope, state_gdn, state_gdn_conv,
                       weights, consts)
    return (out_p[0], out_s[0]) + out_p[1:] + out_s[1:]
```

```python
import functools

import numpy as np
import jax
import jax.numpy as jnp
from jax import lax
from jax.experimental import pallas as pl
from jax.experimental.pallas import tpu as pltpu

F32 = jnp.float32
BF16 = jnp.bfloat16

D_MODEL = 1024
D_FF = 2816
NORM_EPS = 1e-6
L2_EPS = 1e-6
CHUNK = 64
GDN_HEADS = 8
GDN_DK = 64
GDN_DV = 64
CONV_W = 4
GDN_QK = GDN_HEADS * GDN_DK
GDN_VW = GDN_HEADS * GDN_DV
GDN_CONV_DIM = 2 * GDN_QK + GDN_VW
MLA_HEADS = 8
Q_RANK = 256
KV_RANK = 128
NOPE_DIM = 64
ROPE_DIM = 32
V_DIM = 64
ROPE_THETA = 10000.0
MLA_OUT = MLA_HEADS * V_DIM
MASK_VALUE = -1e30

LANES = 128
HEAD_PAD = 128
FF_CHUNK = 512
GROUP = 2
GROUP_W = GROUP * GDN_DK
N_GROUPS = GDN_HEADS // GROUP
G_LANE0 = 32
B_LANE0 = 40
VMEM_LIMIT = 56 * 1024 * 1024
TOK_TILE = 512
POST_TILE = 512
KV_TILE = 1024
ATT_TILE = 256
ATT_HEADS = 8
GDN_TB = 256
GDN_NB = 2
INV_BASE = 8


def _dot(a, b):
    return jnp.dot(a, b, preferred_element_type=F32)


def _dot_nt(a, b):
    return lax.dot_general(a, b, (((1,), (1,)), ((), ())), preferred_element_type=F32)


def _dot_tn(a, b):
    return lax.dot_general(a, b, (((0,), (0,)), ((), ())), preferred_element_type=F32)


def _split_dot(x, w01, passes, left=False):
    acc = None
    r = x
    for i in range(passes):
        part = r.astype(BF16)
        d = _dot(w01, part) if left else _dot(part, w01)
        acc = d if acc is None else acc + d
        if i + 1 < passes:
            r = r - part.astype(F32)
    return acc


def _rms(x, w):
    return x * lax.rsqrt(jnp.mean(x * x, axis=-1, keepdims=True) + NORM_EPS) * w


def _silu(x):
    return x * jax.nn.sigmoid(x)


def _ffn_residual(x, nw, wg_ref, wu_ref, wd_ref):
    xn = _rms(x, nw).astype(BF16)
    acc = None
    for c0 in range(0, D_FF, FF_CHUNK):
        c1 = min(c0 + FF_CHUNK, D_FF)
        g = _dot(xn, wg_ref[:, c0:c1])
        u = _dot(xn, wu_ref[:, c0:c1])
        hid = (_silu(g) * u).astype(BF16)
        d = _dot(hid, wd_ref[c0:c1, :])
        acc = d if acc is None else acc + d
    return x + 0.5 * acc


def _causal_conv(x, tail, cw):
    row8 = lax.broadcasted_iota(jnp.int32, (8, x.shape[1]), 0)
    conv = x * cw[CONV_W - 1:CONV_W, :]
    for sh in range(1, CONV_W):
        xr = pltpu.roll(x, sh, axis=0)
        head = jnp.where(row8 < sh, pltpu.roll(tail, sh, axis=0), xr[:8])
        conv = conv + jnp.concatenate([head, xr[8:]], axis=0) * cw[CONV_W - 1 - sh:CONV_W - sh, :]
    return conv


def _pre_kernel(x_ref, nf_ref, wg_ref, wu_ref, wd_ref, nm_ref, wmain_ref, wsm_ref, gvec_ref,
                qn_ref, kvn_ref, wq2_ref, rot_ref, *rest, with_kv):
    if with_kv:
        wk_ref, wvt_ref, xo_ref, qkv_ref, z_ref, q_ref, c_ref, kpe_ref, gb_ref, k_ref, vt_ref = rest
    else:
        xo_ref, qkv_ref, z_ref, q_ref, c_ref, kpe_ref, gb_ref = rest
    x1 = _ffn_residual(x_ref[...], nf_ref[...], wg_ref, wu_ref, wd_ref)
    xo_ref[...] = x1
    h = _rms(x1, nm_ref[...]).astype(BF16)
    qkv_ref[...] = _dot(h, wmain_ref[:, :GDN_CONV_DIM])
    c0 = GDN_CONV_DIM + GDN_VW
    z_ref[...] = _dot(h, wmain_ref[:, GDN_CONV_DIM:c0])
    lat = _dot(h, wmain_ref[:, c0:])
    cq = lat[:, :Q_RANK]
    ckv = lat[:, Q_RANK:]
    rot = rot_ref[...]
    cosq, sinq = rot[:, :LANES], rot[:, LANES:2 * LANES]
    cosk, sink = rot[:, 2 * LANES:3 * LANES], rot[:, 3 * LANES:]
    cqn = _rms(cq, qn_ref[...]).astype(BF16)
    nq = MLA_HEADS * HEAD_PAD
    cos8 = jnp.concatenate([cosq] * MLA_HEADS, axis=1)
    sin8 = jnp.concatenate([sinq] * MLA_HEADS, axis=1)
    q_ref[...] = (_dot(cqn, wq2_ref[:, :nq]) * cos8 + _dot(cqn, wq2_ref[:, nq:]) * sin8).astype(BF16)
    c_new = _rms(ckv, kvn_ref[...])
    c_ref[...] = c_new
    small = _dot(h, wsm_ref[...])
    s1, s2 = small[:, :LANES], small[:, LANES:]
    kpe_full = s1 * cosk + s2 * sink
    kpe_ref[...] = kpe_full[:, :ROPE_DIM]
    if with_kv:
        cb = c_new.astype(BF16)
        k_ref[...] = _dot(jnp.concatenate([cb, kpe_full.astype(BF16)], axis=1), wk_ref[...]).astype(BF16)
        for i in range(vt_ref.shape[0]):
            vt_ref[i] = _dot_nt(wvt_ref[...], cb[i * ATT_TILE:(i + 1) * ATT_TILE]).astype(BF16)
    gvec = gvec_ref[...]
    xa = s1 + gvec[1:2, :]
    softplus = jnp.maximum(xa, 0.0) + jnp.log1p(jnp.exp(-jnp.abs(xa)))
    gfull = -jnp.exp(gvec[0:1, :]) * softplus
    bfull = jax.nn.sigmoid(s1)
    lane = lax.broadcasted_iota(jnp.int32, s1.shape, 1)
    is_g = (lane >= G_LANE0) & (lane < G_LANE0 + GDN_HEADS)
    is_b = (lane >= B_LANE0) & (lane < B_LANE0 + GDN_HEADS)
    gb_ref[...] = jnp.where(is_g, gfull, jnp.where(is_b, bfull, 0.0))


def _post_kernel(x_ref, g_ref, m_ref, wo_ref, nf_ref, wg_ref, wu_ref, wd_ref, nfin_ref, y_ref, *, final):
    x = x_ref[...] + _dot(g_ref[...], wo_ref[:GDN_VW, :]) + _dot(m_ref[...], wo_ref[GDN_VW:, :])
    x2 = _ffn_residual(x, nf_ref[...], wg_ref, wu_ref, wd_ref)
    y_ref[...] = _rms(x2, nfin_ref[...]) if final else x2


def _const_spec(shape):
    nd = len(shape)
    return pl.BlockSpec(shape, lambda *_: (0,) * nd, pipeline_mode=pl.Buffered(1))


def _params(sem):
    return pltpu.CompilerParams(dimension_semantics=sem, vmem_limit_bytes=VMEM_LIMIT)


def _pre_call(x, lw, rot, seq_len, with_kv):
    n = x.shape[0]
    tm = min(TOK_TILE, n)
    rot_blocks = rot.shape[0] // tm
    row = lambda w: pl.BlockSpec((tm, w), lambda i: (i, 0))
    consts = [lw['nf1'], lw['wg1'], lw['wu1'], lw['wd1'], lw['nmix'], lw['wmain'], lw['wsm'], lw['gvec'],
              lw['qnorm'], lw['kvnorm'], lw['wq2']]
    kv_consts = [lw['wke2'], lw['wvt']] if with_kv else []
    in_specs = [row(D_MODEL)] + [_const_spec(c.shape) for c in consts]
    in_specs.append(pl.BlockSpec((tm, 4 * LANES), lambda i: (i % rot_blocks, 0)))
    in_specs += [_const_spec(c.shape) for c in kv_consts]
    out_shape = (
        jax.ShapeDtypeStruct((n, D_MODEL), F32),
        jax.ShapeDtypeStruct((n, GDN_CONV_DIM), F32),
        jax.ShapeDtypeStruct((n, GDN_VW), F32),
        jax.ShapeDtypeStruct((n, MLA_HEADS * HEAD_PAD), BF16),
        jax.ShapeDtypeStruct((n, KV_RANK), F32),
        jax.ShapeDtypeStruct((n, ROPE_DIM), F32),
        jax.ShapeDtypeStruct((n, LANES), F32),
    )
    out_specs = (row(D_MODEL), row(GDN_CONV_DIM), row(GDN_VW), row(MLA_HEADS * HEAD_PAD), row(KV_RANK),
                 row(ROPE_DIM), row(LANES))
    if with_kv:
        out_shape += (jax.ShapeDtypeStruct((n, MLA_HEADS * HEAD_PAD), BF16),
                      jax.ShapeDtypeStruct((n // ATT_TILE, MLA_OUT, ATT_TILE), BF16))
        out_specs += (row(MLA_HEADS * HEAD_PAD),
                      pl.BlockSpec((tm // ATT_TILE, MLA_OUT, ATT_TILE), lambda i: (i, 0, 0)))
    return pl.pallas_call(
        functools.partial(_pre_kernel, with_kv=with_kv), grid=(n // tm,), in_specs=in_specs,
        out_specs=out_specs, out_shape=out_shape,
        compiler_params=_params(("parallel",)), name="ffn_proj",
    )(x, *consts, rot, *kv_consts)


def _post_call(x, g_out, m_out, lw, nfin, final):
    n = x.shape[0]
    tm = min(POST_TILE, n)
    row = lambda w: pl.BlockSpec((tm, w), lambda i: (i, 0))
    consts = [lw['wo'], lw['nf2'], lw['wg2'], lw['wu2'], lw['wd2'], nfin]
    in_specs = [row(D_MODEL), row(GDN_VW), row(MLA_OUT)] + [_const_spec(c.shape) for c in consts]
    return pl.pallas_call(
        functools.partial(_post_kernel, final=final), grid=(n // tm,), in_specs=in_specs,
        out_specs=row(D_MODEL), out_shape=jax.ShapeDtypeStruct((n, D_MODEL), F32),
        compiler_params=_params(("parallel",)), name="mix_ffn",
    )(x, g_out, m_out, *consts)


def _kv_kernel(c_ref, kpe_ref, wk_ref, wv_ref, k_ref, v_ref):
    c = c_ref[...].astype(BF16)
    ck = jnp.concatenate([c, kpe_ref[...].astype(BF16)], axis=1)
    k_ref[...] = _dot(ck, wk_ref[...]).astype(BF16)
    v_ref[...] = _dot(c, wv_ref[...]).astype(BF16)


def _kv_call(c, kpe, lw):
    n = c.shape[0]
    tm = min(KV_TILE, n)
    row = lambda w: pl.BlockSpec((tm, w), lambda i: (i, 0))
    consts = [lw['wke'], lw['wv']]
    return pl.pallas_call(
        _kv_kernel, grid=(n // tm,),
        in_specs=[row(KV_RANK), row(ROPE_DIM)] + [_const_spec(w.shape) for w in consts],
        out_specs=(row(MLA_HEADS * HEAD_PAD), row(MLA_OUT)),
        out_shape=(jax.ShapeDtypeStruct((n, MLA_HEADS * HEAD_PAD), BF16),
                   jax.ShapeDtypeStruct((n, MLA_OUT), BF16)),
        compiler_params=_params(("parallel",)), name="kv_up",
    )(c, kpe, *consts)


_ATT_SCALE = float((NOPE_DIM + ROPE_DIM) ** -0.5)
_ATT_SCALE_LOG2 = float(_ATT_SCALE * np.log2(np.e))


def _merge_heads(outs):
    lane = lax.broadcasted_iota(jnp.int32, outs[0].shape, 1)
    return jnp.where(lane < V_DIM, outs[0], outs[1])


def _attn_prompt_kernel(q_ref, k_ref, vt_ref, o_ref, s_scr, p_scr, m_scr, l_scr, acc_scr, *, tile, nh):
    qi = pl.program_id(2)
    heads = tuple(range(nh))
    pair_rows = [slice((h // 2) * 2 * V_DIM, (h // 2 + 1) * 2 * V_DIM) for h in heads]
    hsl = [slice(h * HEAD_PAD, (h + 1) * HEAD_PAD) for h in heads]

    def scores(j, h):
        k0 = pl.multiple_of(j * tile, tile)
        return _dot_nt(k_ref[0, pl.ds(k0, tile), hsl[h]], q_ref[0, :, hsl[h]]) * _ATT_SCALE_LOG2

    def pv(j, h):
        return _dot(vt_ref[j, pair_rows[h], :], p_scr[h])

    def softmax(h, s, acc_in):
        m_prev = m_scr[h]
        m_new = jnp.maximum(m_prev, jnp.max(s, axis=0, keepdims=True))
        alpha = jnp.exp2(m_prev - m_new)
        p = jnp.exp2(s - m_new)
        l_scr[h] = alpha * l_scr[h] + jnp.sum(p, axis=0, keepdims=True)
        acc_scr[h] = alpha * acc_in
        m_scr[h] = m_new
        p_scr[h] = p.astype(BF16)

    for h in heads:
        s_scr[h] = scores(0, h)
        p_scr[h] = jnp.zeros((tile, tile), BF16)
        m_scr[h] = jnp.full((1, tile), MASK_VALUE, F32)
        l_scr[h] = jnp.zeros((1, tile), F32)
        acc_scr[h] = jnp.zeros((2 * V_DIM, tile), F32)

    def body(j, carry):
        jm1 = jnp.maximum(j - 1, 0)
        accs = [acc_scr[h] + pv(jm1, h) for h in heads]
        s_next = [scores(j + 1, h) for h in heads]
        for h in heads:
            softmax(h, s_scr[h], accs[h])
        for h in heads:
            s_scr[h] = s_next[h]
        return carry

    lax.fori_loop(0, qi, body, 0)

    kc = lax.broadcasted_iota(jnp.int32, (tile, tile), 0) // CHUNK
    qc = lax.broadcasted_iota(jnp.int32, (tile, tile), 1) // CHUNK
    jm1 = jnp.maximum(qi - 1, 0)
    accs = [acc_scr[h] + pv(jm1, h) for h in heads]
    for h in heads:
        softmax(h, jnp.where(kc <= qc, s_scr[h], MASK_VALUE), accs[h])
    outs = [(acc_scr[h] + pv(qi, h)) / l_scr[h] for h in heads]
    vrow = lax.broadcasted_iota(jnp.int32, outs[0].shape, 0)
    merged = [jnp.where(vrow < V_DIM, outs[h], outs[h + 1]) for h in heads[::2]]
    o_ref[0] = jnp.concatenate(merged, axis=0).T.astype(BF16)


def _attn_prompt_call(q, k, vt):
    b, t, _ = q.shape
    tile = ATT_TILE
    nh = ATT_HEADS
    groups = MLA_HEADS // nh
    nk = t // tile
    return pl.pallas_call(
        functools.partial(_attn_prompt_kernel, tile=tile, nh=nh), grid=(b, groups, nk),
        in_specs=[pl.BlockSpec((1, tile, nh * HEAD_PAD), lambda bi, p, i: (bi, i, p)),
                  pl.BlockSpec((1, t, nh * HEAD_PAD), lambda bi, p, i: (bi, 0, p)),
                  pl.BlockSpec((nk, nh * V_DIM, tile), lambda bi, p, i: (bi, p, 0))],
        out_specs=pl.BlockSpec((1, tile, nh * V_DIM), lambda bi, p, i: (bi, i, p)),
        out_shape=jax.ShapeDtypeStruct((b, t, MLA_OUT), BF16),
        scratch_shapes=[pltpu.VMEM((nh, tile, tile), F32), pltpu.VMEM((nh, tile, tile), BF16),
                        pltpu.VMEM((nh, 1, tile), F32), pltpu.VMEM((nh, 1, tile), F32),
                        pltpu.VMEM((nh, 2 * V_DIM, tile), F32)],
        compiler_params=_params(("parallel", "parallel", "arbitrary")), name="attn_prompt",
    )(q, k, vt)


def _attn_sample_kernel(q_ref, cp_ref, kpp_ref, kn_ref, vn_ref, wabs_ref, wv_ref, o_ref, *, tq, past):
    pad = LANES - tq
    heads = range(MLA_HEADS)
    hsl = [slice(h * HEAD_PAD, (h + 1) * HEAD_PAD) for h in heads]
    cp = cp_ref[0].astype(BF16)
    ckp = jnp.concatenate([cp, kpp_ref[0].astype(BF16)], axis=1)
    qa = jnp.concatenate([_dot(q_ref[0, :, hsl[h]], wabs_ref[h]).astype(BF16) for h in heads], axis=0)
    q_chunk = jnp.concatenate([(past + lax.broadcasted_iota(jnp.int32, (tq, 1), 0)) // CHUNK] * MLA_HEADS, axis=0)
    s_p = _dot_nt(qa, ckp) * _ATT_SCALE
    kc_p = lax.broadcasted_iota(jnp.int32, s_p.shape, 1) // CHUNK
    s_p = jnp.where(kc_p <= q_chunk, s_p, MASK_VALUE)
    s_n = []
    for h in heads:
        kn = jnp.concatenate([kn_ref[0, :, hsl[h]], jnp.zeros((pad, HEAD_PAD), BF16)], axis=0)
        s_n.append(_dot_nt(q_ref[0, :, hsl[h]], kn))
    s_n = jnp.concatenate(s_n, axis=0) * _ATT_SCALE
    col = lax.broadcasted_iota(jnp.int32, s_n.shape, 1)
    s_n = jnp.where((col < tq) & ((past + col) // CHUNK <= q_chunk), s_n, MASK_VALUE)
    m = jnp.maximum(jnp.max(s_p, axis=-1, keepdims=True), jnp.max(s_n, axis=-1, keepdims=True))
    p_p = jnp.exp(s_p - m)
    e_n = jnp.exp(s_n - m)
    p_n = e_n.astype(BF16)
    l = jnp.sum(p_p, axis=-1, keepdims=True) + jnp.sum(e_n, axis=-1, keepdims=True)
    pc = _dot(p_p.astype(BF16), cp).astype(BF16)
    pairs = []
    for pr in range(MLA_HEADS // 2):
        vs = slice(pr * 2 * V_DIM, (pr + 1) * 2 * V_DIM)
        vn = jnp.concatenate([vn_ref[0, :, vs], jnp.zeros((pad, 2 * V_DIM), BF16)], axis=0)
        outs = []
        for h in (2 * pr, 2 * pr + 1):
            rs = slice(h * tq, (h + 1) * tq)
            outs.append((_dot(pc[rs], wv_ref[:, vs]) + _dot(p_n[rs], vn)) / l[rs])
        pairs.append(_merge_heads(outs))
    o_ref[0] = jnp.concatenate(pairs, axis=1).astype(BF16)


def _attn_sample_call(q, c_past, kpe_past, layer, k_new, v_new, lw):
    b, tq, _ = q.shape
    past = c_past.shape[2]
    blk = lambda rows, w: pl.BlockSpec((1, rows, w), lambda bi: (bi, 0, 0))
    cache = lambda w: pl.BlockSpec((None, 1, past, w), lambda bi: (layer, bi, 0, 0))
    consts = [lw['wabs'], lw['wv']]
    return pl.pallas_call(
        functools.partial(_attn_sample_kernel, tq=tq, past=past), grid=(b,),
        in_specs=[blk(tq, MLA_HEADS * HEAD_PAD), cache(KV_RANK), cache(ROPE_DIM),
                  blk(tq, MLA_HEADS * HEAD_PAD), blk(tq, MLA_OUT)] + [_const_spec(w.shape) for w in consts],
        out_specs=blk(tq, MLA_OUT),
        out_shape=jax.ShapeDtypeStruct((b, tq, MLA_OUT), BF16),
        compiler_params=_params(("parallel",)), name="attn_sample",
    )(q, c_past, kpe_past, k_new, v_new, *consts)


def _gdn_kernel(x_ref, z_ref, gb_ref, cbuf_ref, s0_ref, cw_ref, nw_ref, gsum_ref, eg_ref, eb_ref, bdm_ref,
                dup_ref, dupt_ref, o_ref, sout_ref, s_scr, tail_scr, *, nb, tb, nt):
    t = pl.program_id(1)
    rows = max(tb, CHUNK)

    @pl.when(t == 0)
    def _():
        for bi in range(nb):
            for p in range(N_GROUPS):
                s_scr[bi, p] = _split_dot(s0_ref[bi, p], dup_ref[...], 3) * bdm_ref[...].astype(F32)
        tail_scr[...] = cbuf_ref[...]

    cw = cw_ref[...]
    gsum = gsum_ref[...]
    bdm = bdm_ref[...]
    bdm_f32 = bdm.astype(F32)

    def bd(m):
        mb = m.astype(BF16)
        return jnp.concatenate([mb] * GROUP, axis=0) * bdm

    ri = lax.broadcasted_iota(jnp.int32, (CHUNK, GROUP_W), 0)
    ci = lax.broadcasted_iota(jnp.int32, (CHUNK, GROUP_W), 1) % CHUNK
    causal = ri >= ci
    strict = ri > ci
    eye = jnp.where(ri == ci, 1.0, 0.0).astype(F32)
    tri = jnp.where(lax.broadcasted_iota(jnp.int32, (CHUNK, CHUNK), 0)
                    >= lax.broadcasted_iota(jnp.int32, (CHUNK, CHUNK), 1), 1.0, 0.0).astype(BF16)

    def same_block(size):
        return (ri // size) == (ci // size)

    def precompute(units):
        def decay_and_products():
            for u in units:
                gcum = _split_dot(u['g'], tri, 3, left=True)
                grow = jnp.sum(jnp.where(ri <= ci, u['g'], 0.0), axis=0, keepdims=True)
                eg = jnp.exp(gcum)
                glast = gcum[CHUNK - 1:CHUNK, :]
                kb = u['k'] * u['b']
                u['qd'] = u['q'] * eg
                u['kd'] = u['k'] * jnp.exp(glast - gcum)
                u['eglast'] = jnp.exp(glast)
                u['vb'] = u['v'] * u['b']
                u['kg'] = kb * eg
                decay = jnp.where(causal, jnp.exp(jnp.where(causal, gcum - grow, 0.0)), 0.0)
                kq = _dot_nt(jnp.concatenate([kb, u['q']], axis=0).astype(BF16), bd(u['k']))
                u['attn'] = kq[CHUNK:] * decay
                u['lower'] = jnp.where(strict, kq[:CHUNK] * decay, 0.0)
                u['m'] = -jnp.where(same_block(INV_BASE), u['lower'], 0.0)
                u['inv'] = eye + u['m']

        def square():
            for u in units:
                u['m'] = _dot(u['m'].astype(BF16), bd(u['m']))

        def square_and_extend():
            for u in units:
                r = _dot(jnp.concatenate([u['m'], u['inv']], axis=0).astype(BF16), bd(u['m']))
                u['m'] = r[:CHUNK]
                u['inv'] = u['inv'] + r[CHUNK:]

        def extend():
            for u in units:
                u['inv'] = u['inv'] + _dot(u['inv'].astype(BF16), bd(u['m']))

        def couple(size):
            def left():
                for u in units:
                    off = jnp.where(same_block(2 * size) & ~same_block(size), u['lower'], 0.0)
                    u['m'] = _dot(off.astype(BF16), bd(u['inv']))

            def right():
                for u in units:
                    u['inv'] = u['inv'] - _dot(u['inv'].astype(BF16), bd(u['m']))
            return [left, right]

        def solve():
            for u in units:
                invb = u['inv'].astype(BF16)
                u['value'] = _dot(invb, bd(u['vb']))
                u['kcum'] = _dot(invb, bd(u['kg']))

        stages = [decay_and_products, square, square_and_extend, extend]
        size = INV_BASE
        while size < CHUNK:
            stages += couple(size)
            size *= 2
        return stages + [solve]

    n_chunks = rows // CHUNK
    units = [[] for _ in range(nb)]

    conv_rows = {}

    def conv_piece(bi, c):
        r0 = c * CHUNK
        r1 = min(r0 + CHUNK, tb)
        if c == 0:
            tail = tail_scr[bi]
            tail_scr[bi] = x_ref[bi, tb - 8:tb, :]
        else:
            tail = x_ref[bi, r0 - 8:r0, :]
        conv_rows[(bi, c)] = _silu(_causal_conv(x_ref[bi, r0:r1, :], tail, cw))

    def front(bi):
        pieces = [conv_rows[(bi, c)] for c in range(n_chunks)]
        y = jnp.concatenate(pieces, axis=0) if n_chunks > 1 else pieces[0]
        gb = gb_ref[bi]
        if rows > tb:
            y = jnp.concatenate([y, jnp.zeros((rows - tb, GDN_CONV_DIM), F32)], axis=0)
            gb = jnp.concatenate([gb, jnp.zeros((rows - tb, LANES), F32)], axis=0)
        q, k, v = y[:, :GDN_QK], y[:, GDN_QK:2 * GDN_QK], y[:, 2 * GDN_QK:]
        qn = q * lax.rsqrt(_split_dot(q * q, gsum, 2) + L2_EPS) * (GDN_DK ** -0.5)
        kn = k * lax.rsqrt(_split_dot(k * k, gsum, 2) + L2_EPS)
        ge = _split_dot(gb, eg_ref[...], 3)
        be = _split_dot(gb, eb_ref[...], 3)
        for c in range(n_chunks):
            rs = slice(c * CHUNK, (c + 1) * CHUNK)
            for p in range(N_GROUPS):
                ls = slice(p * GROUP_W, (p + 1) * GROUP_W)
                units[bi].append(dict(bi=bi, c=c, p=p, q=qn[rs, ls], k=kn[rs, ls], v=v[rs, ls], g=ge[rs, ls],
                                      b=be[rs, ls]))

    state = {}
    outs = {}

    def rec_read(bi, c):
        for u in units[bi]:
            if u['c'] == c:
                s = state.setdefault((bi, u['p']), s_scr[bi, u['p']]) if c == 0 else state[(bi, u['p'])]
                u['r'] = _dot(jnp.concatenate([u['kcum'], u['qd']], axis=0).astype(BF16), s.astype(BF16))

    def rec_update(bi, c):
        for u in units[bi]:
            if u['c'] == c:
                v_new = u['value'] - u['r'][:CHUNK]
                outs[(bi, c, u['p'])] = u['r'][CHUNK:] + _dot(u['attn'].astype(BF16), bd(v_new))
                upd = _dot_tn(u['kd'].astype(BF16), v_new.astype(BF16))
                key = (bi, u['p'])
                state[key] = state[key] * u['eglast'] + upd * bdm_f32

    def finish(bi):
        for p in range(N_GROUPS):
            s_scr[bi, p] = state[(bi, p)]
        o_rows = [jnp.concatenate([outs[(bi, c, p)] for p in range(N_GROUPS)], axis=1) for c in range(n_chunks)]
        o = jnp.concatenate(o_rows, axis=0) if n_chunks > 1 else o_rows[0]
        o = o[:tb]
        ms = _split_dot(o * o, gsum, 2) * (1.0 / GDN_DV)
        o_ref[bi] = (o * lax.rsqrt(ms + NORM_EPS) * nw_ref[...] * _silu(z_ref[bi])).astype(BF16)

    def bind(fn, *args):
        return lambda: fn(*args)

    fronts = [[bind(front, bi)] for bi in range(nb)]
    pres = [precompute(units[bi]) for bi in range(nb)]
    recs = [[f for c in range(n_chunks) for f in (bind(rec_read, bi, c), bind(rec_update, bi, c))]
            + [bind(finish, bi)] for bi in range(nb)]

    def interleave(*seqs):
        seqs = [s for s in seqs if s]
        n = max(len(s) for s in seqs)
        keyed = [((i + 0.5) * n / len(s), k, f) for k, s in enumerate(seqs) for i, f in enumerate(s)]
        return [f for _, _, f in sorted(keyed, key=lambda e: (e[0], e[1]))]

    program = []
    for bi in range(nb):
        program += [bind(conv_piece, bi, c) for c in range(n_chunks)] + fronts[bi] + pres[bi]
    program += interleave(*recs)
    for thunk in program:
        thunk()

    @pl.when(t == nt - 1)
    def _():
        for bi in range(nb):
            for p in range(N_GROUPS):
                sout_ref[bi, p] = _split_dot(s_scr[bi, p], dupt_ref[...], 3)


def _gdn_call(qkv, z, gb, cbuf8, s0, lw, consts):
    b, t, _ = qkv.shape
    tb = min(GDN_TB, t)
    nt = t // tb
    nb = GDN_NB
    blk = lambda w: pl.BlockSpec((nb, tb, w), lambda bi, ti: (bi, ti, 0))
    per_b = lambda shape: pl.BlockSpec((nb,) + shape, lambda bi, ti: (bi,) + (0,) * len(shape))
    cs = [lw['convw'], lw['gnw'], consts['gsum'], consts['eg'], consts['eb'], consts['bdm'], consts['dup'],
          consts['dupt']]
    state_spec = per_b((N_GROUPS, GROUP_W, GDN_DV))
    return pl.pallas_call(
        functools.partial(_gdn_kernel, nb=nb, tb=tb, nt=nt), grid=(b // nb, nt),
        in_specs=[blk(GDN_CONV_DIM), blk(GDN_VW), blk(LANES), per_b((8, GDN_CONV_DIM)), state_spec]
        + [_const_spec(c.shape) for c in cs],
        out_specs=(blk(GDN_VW), state_spec),
        out_shape=(jax.ShapeDtypeStruct((b, t, GDN_VW), BF16),
                   jax.ShapeDtypeStruct((b, N_GROUPS, GROUP_W, GDN_DV), F32)),
        scratch_shapes=[pltpu.VMEM((nb, N_GROUPS, GROUP_W, GROUP_W), F32),
                        pltpu.VMEM((nb, 8, GDN_CONV_DIM), F32)],
        compiler_params=_params(("parallel", "arbitrary")), name="gdn",
    )(qkv, z, gb, cbuf8, s0, *cs)


def _layer_weights(l, p):
    c1 = GDN_CONV_DIM
    c2 = c1 + GDN_VW
    c3 = c2 + GDN_HEADS
    c4 = c3 + GDN_HEADS
    c5 = c4 + Q_RANK
    c6 = c5 + KV_RANK
    w_in = p['w_in'][l]
    half = ROPE_DIM // 2
    kpe_w = w_in[:, c6:]
    kpe_sw = jnp.concatenate([kpe_w[:, half:], kpe_w[:, :half]], axis=1)
    zcol = lambda n: jnp.zeros((D_MODEL, n), F32)
    wsm = jnp.concatenate([kpe_w, w_in[:, c2:c3], w_in[:, c3:c4], zcol(LANES - B_LANE0 - GDN_HEADS),
                           kpe_sw, zcol(LANES - ROPE_DIM)], axis=1)
    wmain = jnp.concatenate([w_in[:, :c2], w_in[:, c4:c6]], axis=1)
    gvec = jnp.zeros((2, LANES), F32)
    gvec = gvec.at[0, G_LANE0:G_LANE0 + GDN_HEADS].set(p['gdn_a_log'][l])
    gvec = gvec.at[1, G_LANE0:G_LANE0 + GDN_HEADS].set(p['gdn_dt_bias'][l])
    wq = p['w_uq'][l].reshape(Q_RANK, MLA_HEADS, NOPE_DIM + ROPE_DIM)
    nope, rope = wq[..., :NOPE_DIM], wq[..., NOPE_DIM:]
    rope_sw = jnp.concatenate([rope[..., half:], rope[..., :half]], axis=-1)
    zq = lambda n: jnp.zeros((Q_RANK, MLA_HEADS, n), F32)
    tail = HEAD_PAD - NOPE_DIM - ROPE_DIM
    wq_a = jnp.concatenate([nope, rope, zq(tail)], axis=-1).reshape(Q_RANK, MLA_HEADS * HEAD_PAD)
    wq_b = jnp.concatenate([zq(NOPE_DIM), rope_sw, zq(tail)], axis=-1).reshape(Q_RANK, MLA_HEADS * HEAD_PAD)
    wkv = p['w_ukv'][l].reshape(KV_RANK, MLA_HEADS, NOPE_DIM + V_DIM)
    wk = jnp.concatenate([wkv[..., :NOPE_DIM], jnp.zeros((KV_RANK, MLA_HEADS, HEAD_PAD - NOPE_DIM), F32)],
                         axis=-1).reshape(KV_RANK, MLA_HEADS * HEAD_PAD)
    wv = wkv[..., NOPE_DIM:].reshape(KV_RANK, MLA_OUT)
    we = np.zeros((ROPE_DIM, MLA_HEADS, HEAD_PAD), np.float32)
    for r in range(ROPE_DIM):
        we[r, :, NOPE_DIM + r] = 1.0
    rope_pass = np.zeros((HEAD_PAD - NOPE_DIM, KV_RANK + ROPE_DIM), np.float32)
    for r in range(ROPE_DIM):
        rope_pass[r, KV_RANK + r] = 1.0
    wk_t = jnp.transpose(wkv[..., :NOPE_DIM], (1, 2, 0))
    wabs = jnp.concatenate([jnp.concatenate([wk_t, jnp.zeros((MLA_HEADS, NOPE_DIM, ROPE_DIM), F32)], axis=2),
                            jnp.broadcast_to(rope_pass, (MLA_HEADS,) + rope_pass.shape)], axis=1)
    bf = lambda a: a.astype(BF16)
    r2 = lambda a: a.reshape(1, -1)
    return dict(
        nf1=r2(p['norm_ffn1'][l]), wg1=bf(p['w_ffn1_gate'][l]), wu1=bf(p['w_ffn1_up'][l]),
        wd1=bf(p['w_ffn1_down'][l]), nmix=r2(p['norm_mix'][l]), wmain=bf(wmain), wsm=bf(wsm), gvec=gvec,
        qnorm=r2(p['mla_q_norm'][l]), kvnorm=r2(p['mla_kv_norm'][l]),
        wq2=bf(jnp.concatenate([wq_a, wq_b], axis=1)), wv=bf(wv), wvt=bf(wv.T), wabs=bf(wabs),
        wke=bf(jnp.concatenate([wk, jnp.asarray(we.reshape(ROPE_DIM, MLA_HEADS * HEAD_PAD))], axis=0)),
        wke2=bf(jnp.concatenate([wk, jnp.asarray(we.reshape(ROPE_DIM, MLA_HEADS * HEAD_PAD)),
                                 jnp.zeros((LANES - ROPE_DIM, MLA_HEADS * HEAD_PAD), F32)], axis=0)),
        wo=bf(p['w_out'][l]), nf2=r2(p['norm_ffn2'][l]), wg2=bf(p['w_ffn2_gate'][l]),
        wu2=bf(p['w_ffn2_up'][l]), wd2=bf(p['w_ffn2_down'][l]),
        convw=p['gdn_conv_w'][l], gnw=jnp.tile(p['gdn_norm_w'][l], GDN_HEADS).reshape(1, GDN_VW),
    )


def _gdn_consts():
    lane = np.arange(GDN_QK)
    gsum = (lane[:, None] // GDN_DK == lane[None, :] // GDN_DK).astype(np.float32)
    eg = np.zeros((LANES, GDN_QK), np.float32)
    eb = np.zeros((LANES, GDN_QK), np.float32)
    for h in range(GDN_HEADS):
        eg[G_LANE0 + h, h * GDN_DK:(h + 1) * GDN_DK] = 1.0
        eb[B_LANE0 + h, h * GDN_DK:(h + 1) * GDN_DK] = 1.0
    gl = np.arange(GROUP_W)
    bdm = (gl[:, None] // GDN_DK == gl[None, :] // GDN_DK).astype(np.float32)
    dup = np.tile(np.eye(GDN_DV, dtype=np.float32), (1, GROUP))
    return dict(gsum=jnp.asarray(gsum, BF16), eg=jnp.asarray(eg, BF16), eb=jnp.asarray(eb, BF16),
                bdm=jnp.asarray(bdm, BF16), dup=jnp.asarray(dup, BF16), dupt=jnp.asarray(dup.T, BF16))


def _rotary_table(pos, rows):
    half = ROPE_DIM // 2
    inv_freq = 1.0 / (ROPE_THETA ** (jnp.arange(half, dtype=F32) / half))
    ang = pos.astype(F32)[:, None] * inv_freq[None, :]
    cos, sin = jnp.cos(ang), jnp.sin(ang)
    t = pos.shape[0]
    cos2 = jnp.concatenate([cos, cos], axis=1)
    sin2 = jnp.concatenate([-sin, sin], axis=1)
    z = lambda n: jnp.zeros((t, n), F32)
    tail = HEAD_PAD - NOPE_DIM - ROPE_DIM
    tab = jnp.concatenate([jnp.ones((t, NOPE_DIM), F32), cos2, z(tail), z(NOPE_DIM), sin2, z(tail),
                           cos2, z(LANES - ROPE_DIM), sin2, z(LANES - ROPE_DIM)], axis=1)
    return jnp.tile(tab, (rows // t, 1)) if rows > t else tab


def _run_trunk(x, pos, past_ckv, past_kpe, s_gdn, s_conv, weights, consts):
    b, t, _ = x.shape
    n = b * t
    rot = _rotary_table(pos, max(t, min(TOK_TILE, n)))
    xf = x.reshape(n, D_MODEL)
    ckv_rows, kpe_rows, states, conv_bufs = [], [], [], []
    depth = len(weights)
    for l, lw in enumerate(weights):
        if s_gdn is None:
            cbuf8 = jnp.zeros((b, 8, GDN_CONV_DIM), F32)
            s0 = jnp.zeros((b, N_GROUPS, GROUP_W, GDN_DV), F32)
        else:
            cbuf8 = jnp.pad(s_conv[l], ((0, 0), (8 - (CONV_W - 1), 0), (0, 0)))
            s0 = s_gdn[l].reshape(b, N_GROUPS, GROUP_W, GDN_DV)
        if past_ckv is None:
            xf, qkv, z, q, c_new, kpe_new, gb, k_new, v_new = _pre_call(xf, lw, rot, t, with_kv=True)
        else:
            xf, qkv, z, q, c_new, kpe_new, gb = _pre_call(xf, lw, rot, t, with_kv=False)
            k_new, v_new = _kv_call(c_new, kpe_new, lw)
        q3 = q.reshape(b, t, -1)
        k3 = k_new.reshape(b, t, -1)
        if past_ckv is None:
            m_out = _attn_prompt_call(q3, k3, v_new)
        else:
            m_out = _attn_sample_call(q3, past_ckv, past_kpe, l, k3, v_new.reshape(b, t, -1), lw)
        qkv3 = qkv.reshape(b, t, GDN_CONV_DIM)
        g_out, s_new = _gdn_call(qkv3, z.reshape(b, t, GDN_VW), gb.reshape(b, t, LANES), cbuf8, s0, lw,
                                 consts)
        final = l == depth - 1
        xf = _post_call(xf, g_out.reshape(n, GDN_VW), m_out.reshape(n, MLA_OUT), lw,
                        weights[-1]['nfinal'], final)
        ckv_rows.append(c_new.reshape(b, t, KV_RANK))
        kpe_rows.append(kpe_new.reshape(b, t, ROPE_DIM))
        states.append(s_new.reshape(b, GDN_HEADS, GDN_DK, GDN_DV))
        conv_bufs.append(qkv3[:, t - (CONV_W - 1):, :])
    return (xf.reshape(b, t, D_MODEL), jnp.stack(ckv_rows), jnp.stack(kpe_rows), jnp.stack(states),
            jnp.stack(conv_bufs))


def kernel(x_prompt, x_sample, cache_mla_ckv, cache_mla_krope, state_gdn, state_gdn_conv,
           norm_ffn1, w_ffn1_gate, w_ffn1_up, w_ffn1_down, norm_mix, w_in, gdn_conv_w, gdn_a_log,
           gdn_dt_bias, gdn_norm_w, mla_q_norm, mla_kv_norm, w_uq, w_ukv, w_out, norm_ffn2,
           w_ffn2_gate, w_ffn2_up, w_ffn2_down, norm_final):
    params = dict(norm_ffn1=norm_ffn1, w_ffn1_gate=w_ffn1_gate, w_ffn1_up=w_ffn1_up, w_ffn1_down=w_ffn1_down,
                  norm_mix=norm_mix, w_in=w_in, gdn_conv_w=gdn_conv_w, gdn_a_log=gdn_a_log,
                  gdn_dt_bias=gdn_dt_bias, gdn_norm_w=gdn_norm_w, mla_q_norm=mla_q_norm,
                  mla_kv_norm=mla_kv_norm, w_uq=w_uq, w_ukv=w_ukv, w_out=w_out, norm_ffn2=norm_ffn2,
                  w_ffn2_gate=w_ffn2_gate, w_ffn2_up=w_ffn2_up, w_ffn2_down=w_ffn2_down)
    depth = w_in.shape[0]
    weights = [_layer_weights(l, params) for l in range(depth)]
    weights[-1]['nfinal'] = norm_final.reshape(1, D_MODEL)
    consts = _gdn_consts()
    t_p = x_prompt.shape[1]
    out_p = _run_trunk(x_prompt, jnp.arange(t_p, dtype=jnp.int32), None, None, None, None, weights, consts)
    pos_s = cache_mla_ckv.shape[2] + jnp.arange(x_sample.shape[1], dtype=jnp.int32)
    out_s = _run_trunk(x_sample, pos_s, cache_mla_ckv, cache_mla_krope, state_gdn, state_gdn_conv,
                       weights, consts)
    return (out_p[0], out_s[0]) + out_p[1:] + out_s[1:]
```

```python
import functools

import numpy as np
import jax
import jax.numpy as jnp
from jax import lax
from jax.experimental import pallas as pl
from jax.experimental.pallas import tpu as pltpu

F32 = jnp.float32
BF16 = jnp.bfloat16

D_MODEL = 1024
D_FF = 2816
NORM_EPS = 1e-6
L2_EPS = 1e-6
CHUNK = 64
GDN_HEADS = 8
GDN_DK = 64
GDN_DV = 64
CONV_W = 4
GDN_QK = GDN_HEADS * GDN_DK
GDN_VW = GDN_HEADS * GDN_DV
GDN_CONV_DIM = 2 * GDN_QK + GDN_VW
MLA_HEADS = 8
Q_RANK = 256
KV_RANK = 128
NOPE_DIM = 64
ROPE_DIM = 32
V_DIM = 64
ROPE_THETA = 10000.0
MLA_OUT = MLA_HEADS * V_DIM
MASK_VALUE = -1e30

LANES = 128
HEAD_PAD = 128
MAIN_COLS = GDN_CONV_DIM + GDN_VW + Q_RANK + KV_RANK
FF_CHUNK = 512
GROUP = 2
GROUP_W = GROUP * GDN_DK
N_GROUPS = GDN_HEADS // GROUP
G_LANE0 = 32
B_LANE0 = 40
VMEM_LIMIT = 56 * 1024 * 1024
TOK_TILE = 512
POST_TILE = 512
KV_TILE = 1024
ATT_TILE = 256
ATT_HEADS = 8
GDN_TB = 256
GDN_NB = 2
INV_BASE = 8


def _dot(a, b):
    return jnp.dot(a, b, preferred_element_type=F32)


def _dot_nt(a, b):
    return lax.dot_general(a, b, (((1,), (1,)), ((), ())), preferred_element_type=F32)


def _dot_tn(a, b):
    return lax.dot_general(a, b, (((0,), (0,)), ((), ())), preferred_element_type=F32)


def _split_dot(x, w01, passes, left=False):
    acc = None
    r = x
    for i in range(passes):
        part = r.astype(BF16)
        d = _dot(w01, part) if left else _dot(part, w01)
        acc = d if acc is None else acc + d
        if i + 1 < passes:
            r = r - part.astype(F32)
    return acc


def _rms(x, w):
    return x * lax.rsqrt(jnp.mean(x * x, axis=-1, keepdims=True) + NORM_EPS) * w


def _silu(x):
    return x * jax.nn.sigmoid(x)


def _ffn_residual(x, nw, wg_ref, wu_ref, wd_ref):
    xn = _rms(x, nw).astype(BF16)
    acc = None
    for c0 in range(0, D_FF, FF_CHUNK):
        c1 = min(c0 + FF_CHUNK, D_FF)
        g = _dot(xn, wg_ref[:, c0:c1])
        u = _dot(xn, wu_ref[:, c0:c1])
        hid = (_silu(g) * u).astype(BF16)
        d = _dot(hid, wd_ref[c0:c1, :])
        acc = d if acc is None else acc + d
    return x + 0.5 * acc


def _causal_conv(x, tail, cw):
    row8 = lax.broadcasted_iota(jnp.int32, (8, x.shape[1]), 0)
    conv = x * cw[CONV_W - 1:CONV_W, :]
    for sh in range(1, CONV_W):
        xr = pltpu.roll(x, sh, axis=0)
        head = jnp.where(row8 < sh, pltpu.roll(tail, sh, axis=0), xr[:8])
        conv = conv + jnp.concatenate([head, xr[8:]], axis=0) * cw[CONV_W - 1 - sh:CONV_W - sh, :]
    return conv


def _pre_kernel(x_ref, nf_ref, wg_ref, wu_ref, wd_ref, nm_ref, wmain_ref, wsm_ref, gvec_ref,
                qn_ref, kvn_ref, wq2_ref, rot_ref,
                xo_ref, qkv_ref, z_ref, q_ref, c_ref, kpe_ref, gb_ref):
    x1 = _ffn_residual(x_ref[...], nf_ref[...], wg_ref, wu_ref, wd_ref)
    xo_ref[...] = x1
    h = _rms(x1, nm_ref[...]).astype(BF16)
    qkv_ref[...] = _dot(h, wmain_ref[:, :GDN_CONV_DIM])
    c0 = GDN_CONV_DIM + GDN_VW
    z_ref[...] = _dot(h, wmain_ref[:, GDN_CONV_DIM:c0])
    lat = _dot(h, wmain_ref[:, c0:])
    cq = lat[:, :Q_RANK]
    ckv = lat[:, Q_RANK:]
    rot = rot_ref[...]
    cosq, sinq = rot[:, :LANES], rot[:, LANES:2 * LANES]
    cosk, sink = rot[:, 2 * LANES:3 * LANES], rot[:, 3 * LANES:]
    cqn = _rms(cq, qn_ref[...]).astype(BF16)
    nq = MLA_HEADS * HEAD_PAD
    cos8 = jnp.concatenate([cosq] * MLA_HEADS, axis=1)
    sin8 = jnp.concatenate([sinq] * MLA_HEADS, axis=1)
    q_ref[...] = (_dot(cqn, wq2_ref[:, :nq]) * cos8 + _dot(cqn, wq2_ref[:, nq:]) * sin8).astype(BF16)
    c_ref[...] = _rms(ckv, kvn_ref[...])
    small = _dot(h, wsm_ref[...])
    s1, s2 = small[:, :LANES], small[:, LANES:]
    kpe_ref[...] = (s1 * cosk + s2 * sink)[:, :ROPE_DIM]
    gvec = gvec_ref[...]
    xa = s1 + gvec[1:2, :]
    softplus = jnp.maximum(xa, 0.0) + jnp.log1p(jnp.exp(-jnp.abs(xa)))
    gfull = -jnp.exp(gvec[0:1, :]) * softplus
    bfull = jax.nn.sigmoid(s1)
    lane = lax.broadcasted_iota(jnp.int32, s1.shape, 1)
    is_g = (lane >= G_LANE0) & (lane < G_LANE0 + GDN_HEADS)
    is_b = (lane >= B_LANE0) & (lane < B_LANE0 + GDN_HEADS)
    gb_ref[...] = jnp.where(is_g, gfull, jnp.where(is_b, bfull, 0.0))


def _post_kernel(x_ref, g_ref, m_ref, wo_ref, nf_ref, wg_ref, wu_ref, wd_ref, nfin_ref, y_ref, *, final):
    x = x_ref[...] + _dot(g_ref[...], wo_ref[:GDN_VW, :]) + _dot(m_ref[...], wo_ref[GDN_VW:, :])
    x2 = _ffn_residual(x, nf_ref[...], wg_ref, wu_ref, wd_ref)
    y_ref[...] = _rms(x2, nfin_ref[...]) if final else x2


def _const_spec(shape):
    nd = len(shape)
    return pl.BlockSpec(shape, lambda *_: (0,) * nd, pipeline_mode=pl.Buffered(1))


def _params(sem):
    return pltpu.CompilerParams(dimension_semantics=sem, vmem_limit_bytes=VMEM_LIMIT)


def _pre_call(x, lw, rot, seq_len):
    n = x.shape[0]
    tm = min(TOK_TILE, n)
    rot_blocks = rot.shape[0] // tm
    row = lambda w: pl.BlockSpec((tm, w), lambda i: (i, 0))
    consts = [lw['nf1'], lw['wg1'], lw['wu1'], lw['wd1'], lw['nmix'], lw['wmain'], lw['wsm'], lw['gvec'],
              lw['qnorm'], lw['kvnorm'], lw['wq2']]
    in_specs = [row(D_MODEL)] + [_const_spec(c.shape) for c in consts]
    in_specs.append(pl.BlockSpec((tm, 4 * LANES), lambda i: (i % rot_blocks, 0)))
    out_shape = (
        jax.ShapeDtypeStruct((n, D_MODEL), F32),
        jax.ShapeDtypeStruct((n, GDN_CONV_DIM), F32),
        jax.ShapeDtypeStruct((n, GDN_VW), F32),
        jax.ShapeDtypeStruct((n, MLA_HEADS * HEAD_PAD), BF16),
        jax.ShapeDtypeStruct((n, KV_RANK), F32),
        jax.ShapeDtypeStruct((n, ROPE_DIM), F32),
        jax.ShapeDtypeStruct((n, LANES), F32),
    )
    out_specs = (row(D_MODEL), row(GDN_CONV_DIM), row(GDN_VW), row(MLA_HEADS * HEAD_PAD), row(KV_RANK),
                 row(ROPE_DIM), row(LANES))
    return pl.pallas_call(
        _pre_kernel, grid=(n // tm,), in_specs=in_specs, out_specs=out_specs, out_shape=out_shape,
        compiler_params=_params(("parallel",)), name="ffn_proj",
    )(x, *consts, rot)


def _post_call(x, g_out, m_out, lw, nfin, final):
    n = x.shape[0]
    tm = min(POST_TILE, n)
    row = lambda w: pl.BlockSpec((tm, w), lambda i: (i, 0))
    consts = [lw['wo'], lw['nf2'], lw['wg2'], lw['wu2'], lw['wd2'], nfin]
    in_specs = [row(D_MODEL), row(GDN_VW), row(MLA_OUT)] + [_const_spec(c.shape) for c in consts]
    return pl.pallas_call(
        functools.partial(_post_kernel, final=final), grid=(n // tm,), in_specs=in_specs,
        out_specs=row(D_MODEL), out_shape=jax.ShapeDtypeStruct((n, D_MODEL), F32),
        compiler_params=_params(("parallel",)), name="mix_ffn",
    )(x, g_out, m_out, *consts)


def _kv_kernel(c_ref, kpe_ref, wk_ref, wv_ref, k_ref, v_ref, *, v_transposed):
    c = c_ref[...].astype(BF16)
    ck = jnp.concatenate([c, kpe_ref[...].astype(BF16)], axis=1)
    k_ref[...] = _dot(ck, wk_ref[...]).astype(BF16)
    if v_transposed:
        for i in range(v_ref.shape[0]):
            v_ref[i] = _dot_nt(wv_ref[...], c[i * ATT_TILE:(i + 1) * ATT_TILE]).astype(BF16)
    else:
        v_ref[...] = _dot(c, wv_ref[...]).astype(BF16)


def _kv_call(c, kpe, lw, v_transposed):
    n = c.shape[0]
    tm = min(KV_TILE, n)
    row = lambda w: pl.BlockSpec((tm, w), lambda i: (i, 0))
    consts = [lw['wke'], lw['wvt'] if v_transposed else lw['wv']]
    if v_transposed:
        v_spec = pl.BlockSpec((tm // ATT_TILE, MLA_OUT, ATT_TILE), lambda i: (i, 0, 0))
        v_shape = jax.ShapeDtypeStruct((n // ATT_TILE, MLA_OUT, ATT_TILE), BF16)
    else:
        v_spec = row(MLA_OUT)
        v_shape = jax.ShapeDtypeStruct((n, MLA_OUT), BF16)
    return pl.pallas_call(
        functools.partial(_kv_kernel, v_transposed=v_transposed), grid=(n // tm,),
        in_specs=[row(KV_RANK), row(ROPE_DIM)] + [_const_spec(w.shape) for w in consts],
        out_specs=(row(MLA_HEADS * HEAD_PAD), v_spec),
        out_shape=(jax.ShapeDtypeStruct((n, MLA_HEADS * HEAD_PAD), BF16), v_shape),
        compiler_params=_params(("parallel",)), name="kv_up",
    )(c, kpe, *consts)


_ATT_SCALE = float((NOPE_DIM + ROPE_DIM) ** -0.5)
_ATT_SCALE_LOG2 = float(_ATT_SCALE * np.log2(np.e))


def _merge_heads(outs):
    lane = lax.broadcasted_iota(jnp.int32, outs[0].shape, 1)
    return jnp.where(lane < V_DIM, outs[0], outs[1])


def _attn_prompt_kernel(q_ref, k_ref, vt_ref, o_ref, s_scr, p_scr, m_scr, l_scr, acc_scr, *, tile, nh):
    qi = pl.program_id(2)
    heads = tuple(range(nh))
    pair_rows = [slice((h // 2) * 2 * V_DIM, (h // 2 + 1) * 2 * V_DIM) for h in heads]
    hsl = [slice(h * HEAD_PAD, (h + 1) * HEAD_PAD) for h in heads]

    def scores(j, h):
        k0 = pl.multiple_of(j * tile, tile)
        return _dot_nt(k_ref[0, pl.ds(k0, tile), hsl[h]], q_ref[0, :, hsl[h]]) * _ATT_SCALE_LOG2

    def pv(j, h):
        return _dot(vt_ref[j, pair_rows[h], :], p_scr[h])

    def softmax(h, s, acc_in):
        m_prev = m_scr[h]
        m_new = jnp.maximum(m_prev, jnp.max(s, axis=0, keepdims=True))
        alpha = jnp.exp2(m_prev - m_new)
        p = jnp.exp2(s - m_new)
        l_scr[h] = alpha * l_scr[h] + jnp.sum(p, axis=0, keepdims=True)
        acc_scr[h] = alpha * acc_in
        m_scr[h] = m_new
        p_scr[h] = p.astype(BF16)

    for h in heads:
        s_scr[h] = scores(0, h)
        p_scr[h] = jnp.zeros((tile, tile), BF16)
        m_scr[h] = jnp.full((1, tile), MASK_VALUE, F32)
        l_scr[h] = jnp.zeros((1, tile), F32)
        acc_scr[h] = jnp.zeros((2 * V_DIM, tile), F32)

    def body(j, carry):
        jm1 = jnp.maximum(j - 1, 0)
        accs = [acc_scr[h] + pv(jm1, h) for h in heads]
        s_next = [scores(j + 1, h) for h in heads]
        for h in heads:
            softmax(h, s_scr[h], accs[h])
        for h in heads:
            s_scr[h] = s_next[h]
        return carry

    lax.fori_loop(0, qi, body, 0)

    kc = lax.broadcasted_iota(jnp.int32, (tile, tile), 0) // CHUNK
    qc = lax.broadcasted_iota(jnp.int32, (tile, tile), 1) // CHUNK
    jm1 = jnp.maximum(qi - 1, 0)
    accs = [acc_scr[h] + pv(jm1, h) for h in heads]
    for h in heads:
        softmax(h, jnp.where(kc <= qc, s_scr[h], MASK_VALUE), accs[h])
    outs = [(acc_scr[h] + pv(qi, h)) / l_scr[h] for h in heads]
    vrow = lax.broadcasted_iota(jnp.int32, outs[0].shape, 0)
    merged = [jnp.where(vrow < V_DIM, outs[h], outs[h + 1]) for h in heads[::2]]
    o_ref[0] = jnp.concatenate(merged, axis=0).T.astype(BF16)


def _attn_prompt_call(q, k, vt):
    b, t, _ = q.shape
    tile = ATT_TILE
    nh = ATT_HEADS
    groups = MLA_HEADS // nh
    nk = t // tile
    return pl.pallas_call(
        functools.partial(_attn_prompt_kernel, tile=tile, nh=nh), grid=(b, groups, nk),
        in_specs=[pl.BlockSpec((1, tile, nh * HEAD_PAD), lambda bi, p, i: (bi, i, p)),
                  pl.BlockSpec((1, t, nh * HEAD_PAD), lambda bi, p, i: (bi, 0, p)),
                  pl.BlockSpec((nk, nh * V_DIM, tile), lambda bi, p, i: (bi, p, 0))],
        out_specs=pl.BlockSpec((1, tile, nh * V_DIM), lambda bi, p, i: (bi, i, p)),
        out_shape=jax.ShapeDtypeStruct((b, t, MLA_OUT), BF16),
        scratch_shapes=[pltpu.VMEM((nh, tile, tile), F32), pltpu.VMEM((nh, tile, tile), BF16),
                        pltpu.VMEM((nh, 1, tile), F32), pltpu.VMEM((nh, 1, tile), F32),
                        pltpu.VMEM((nh, 2 * V_DIM, tile), F32)],
        compiler_params=_params(("parallel", "parallel", "arbitrary")), name="attn_prompt",
    )(q, k, vt)


def _attn_sample_kernel(q_ref, cp_ref, kpp_ref, kn_ref, vn_ref, wabs_ref, wv_ref, o_ref, *, tq, past):
    pad = LANES - tq
    heads = range(MLA_HEADS)
    hsl = [slice(h * HEAD_PAD, (h + 1) * HEAD_PAD) for h in heads]
    cp = cp_ref[0].astype(BF16)
    ckp = jnp.concatenate([cp, kpp_ref[0].astype(BF16)], axis=1)
    qa = jnp.concatenate([_dot(q_ref[0, :, hsl[h]], wabs_ref[h]).astype(BF16) for h in heads], axis=0)
    q_chunk = jnp.concatenate([(past + lax.broadcasted_iota(jnp.int32, (tq, 1), 0)) // CHUNK] * MLA_HEADS, axis=0)
    s_p = _dot_nt(qa, ckp) * _ATT_SCALE
    kc_p = lax.broadcasted_iota(jnp.int32, s_p.shape, 1) // CHUNK
    s_p = jnp.where(kc_p <= q_chunk, s_p, MASK_VALUE)
    s_n = []
    for h in heads:
        kn = jnp.concatenate([kn_ref[0, :, hsl[h]], jnp.zeros((pad, HEAD_PAD), BF16)], axis=0)
        s_n.append(_dot_nt(q_ref[0, :, hsl[h]], kn))
    s_n = jnp.concatenate(s_n, axis=0) * _ATT_SCALE
    col = lax.broadcasted_iota(jnp.int32, s_n.shape, 1)
    s_n = jnp.where((col < tq) & ((past + col) // CHUNK <= q_chunk), s_n, MASK_VALUE)
    m = jnp.maximum(jnp.max(s_p, axis=-1, keepdims=True), jnp.max(s_n, axis=-1, keepdims=True))
    p_p = jnp.exp(s_p - m)
    e_n = jnp.exp(s_n - m)
    p_n = e_n.astype(BF16)
    l = jnp.sum(p_p, axis=-1, keepdims=True) + jnp.sum(e_n, axis=-1, keepdims=True)
    pc = _dot(p_p.astype(BF16), cp).astype(BF16)
    pairs = []
    for pr in range(MLA_HEADS // 2):
        vs = slice(pr * 2 * V_DIM, (pr + 1) * 2 * V_DIM)
        vn = jnp.concatenate([vn_ref[0, :, vs], jnp.zeros((pad, 2 * V_DIM), BF16)], axis=0)
        outs = []
        for h in (2 * pr, 2 * pr + 1):
            rs = slice(h * tq, (h + 1) * tq)
            outs.append((_dot(pc[rs], wv_ref[:, vs]) + _dot(p_n[rs], vn)) / l[rs])
        pairs.append(_merge_heads(outs))
    o_ref[0] = jnp.concatenate(pairs, axis=1).astype(BF16)


def _attn_sample_call(q, c_past, kpe_past, layer, k_new, v_new, lw):
    b, tq, _ = q.shape
    past = c_past.shape[2]
    blk = lambda rows, w: pl.BlockSpec((1, rows, w), lambda bi: (bi, 0, 0))
    cache = lambda w: pl.BlockSpec((None, 1, past, w), lambda bi: (layer, bi, 0, 0))
    consts = [lw['wabs'], lw['wv']]
    return pl.pallas_call(
        functools.partial(_attn_sample_kernel, tq=tq, past=past), grid=(b,),
        in_specs=[blk(tq, MLA_HEADS * HEAD_PAD), cache(KV_RANK), cache(ROPE_DIM),
                  blk(tq, MLA_HEADS * HEAD_PAD), blk(tq, MLA_OUT)] + [_const_spec(w.shape) for w in consts],
        out_specs=blk(tq, MLA_OUT),
        out_shape=jax.ShapeDtypeStruct((b, tq, MLA_OUT), BF16),
        compiler_params=_params(("parallel",)), name="attn_sample",
    )(q, c_past, kpe_past, k_new, v_new, *consts)


def _gdn_kernel(x_ref, z_ref, gb_ref, cbuf_ref, s0_ref, cw_ref, nw_ref, gsum_ref, eg_ref, eb_ref, bdm_ref,
                dup_ref, dupt_ref, o_ref, sout_ref, s_scr, tail_scr, *, nb, tb, nt):
    t = pl.program_id(1)
    rows = max(tb, CHUNK)

    @pl.when(t == 0)
    def _():
        for bi in range(nb):
            for p in range(N_GROUPS):
                s_scr[bi, p] = _split_dot(s0_ref[bi, p], dup_ref[...], 3) * bdm_ref[...].astype(F32)
        tail_scr[...] = cbuf_ref[...]

    cw = cw_ref[...]
    gsum = gsum_ref[...]
    bdm = bdm_ref[...]
    bdm_f32 = bdm.astype(F32)

    def bd(m):
        mb = m.astype(BF16)
        return jnp.concatenate([mb] * GROUP, axis=0) * bdm

    ri = lax.broadcasted_iota(jnp.int32, (CHUNK, GROUP_W), 0)
    ci = lax.broadcasted_iota(jnp.int32, (CHUNK, GROUP_W), 1) % CHUNK
    causal = ri >= ci
    strict = ri > ci
    eye = jnp.where(ri == ci, 1.0, 0.0).astype(F32)
    tri = jnp.where(lax.broadcasted_iota(jnp.int32, (CHUNK, CHUNK), 0)
                    >= lax.broadcasted_iota(jnp.int32, (CHUNK, CHUNK), 1), 1.0, 0.0).astype(BF16)

    def same_block(size):
        return (ri // size) == (ci // size)

    def precompute(units):
        def decay_and_products():
            for u in units:
                gcum = _split_dot(u['g'], tri, 3, left=True)
                grow = jnp.sum(jnp.where(ri <= ci, u['g'], 0.0), axis=0, keepdims=True)
                eg = jnp.exp(gcum)
                glast = gcum[CHUNK - 1:CHUNK, :]
                kb = u['k'] * u['b']
                u['qd'] = u['q'] * eg
                u['kd'] = u['k'] * jnp.exp(glast - gcum)
                u['eglast'] = jnp.exp(glast)
                u['vb'] = u['v'] * u['b']
                u['kg'] = kb * eg
                decay = jnp.where(causal, jnp.exp(jnp.where(causal, gcum - grow, 0.0)), 0.0)
                kq = _dot_nt(jnp.concatenate([kb, u['q']], axis=0).astype(BF16), bd(u['k']))
                u['attn'] = kq[CHUNK:] * decay
                u['lower'] = jnp.where(strict, kq[:CHUNK] * decay, 0.0)
                u['m'] = -jnp.where(same_block(INV_BASE), u['lower'], 0.0)
                u['inv'] = eye + u['m']

        def square():
            for u in units:
                u['m'] = _dot(u['m'].astype(BF16), bd(u['m']))

        def square_and_extend():
            for u in units:
                r = _dot(jnp.concatenate([u['m'], u['inv']], axis=0).astype(BF16), bd(u['m']))
                u['m'] = r[:CHUNK]
                u['inv'] = u['inv'] + r[CHUNK:]

        def extend():
            for u in units:
                u['inv'] = u['inv'] + _dot(u['inv'].astype(BF16), bd(u['m']))

        def couple(size):
            def left():
                for u in units:
                    off = jnp.where(same_block(2 * size) & ~same_block(size), u['lower'], 0.0)
                    u['m'] = _dot(off.astype(BF16), bd(u['inv']))

            def right():
                for u in units:
                    u['inv'] = u['inv'] - _dot(u['inv'].astype(BF16), bd(u['m']))
            return [left, right]

        def solve():
            for u in units:
                invb = u['inv'].astype(BF16)
                u['value'] = _dot(invb, bd(u['vb']))
                u['kcum'] = _dot(invb, bd(u['kg']))

        stages = [decay_and_products, square, square_and_extend, extend]
        size = INV_BASE
        while size < CHUNK:
            stages += couple(size)
            size *= 2
        return stages + [solve]

    n_chunks = rows // CHUNK
    units = [[] for _ in range(nb)]

    conv_rows = {}

    def conv_piece(bi, c):
        r0 = c * CHUNK
        r1 = min(r0 + CHUNK, tb)
        if c == 0:
            tail = tail_scr[bi]
            tail_scr[bi] = x_ref[bi, tb - 8:tb, :]
        else:
            tail = x_ref[bi, r0 - 8:r0, :]
        taps = cw
        if bi > 0:
            word = pltpu.bitcast(units[bi - 1][0]['attn'][0:8, 0:LANES], jnp.uint32)
            zero = lax.shift_right_logical(lax.shift_right_logical(word, jnp.uint32(16)), jnp.uint32(16))
            zero_row = zero[0:1, :].astype(F32)
            taps = cw + jnp.concatenate([zero_row] * (GDN_CONV_DIM // LANES), axis=1)
        conv_rows[(bi, c)] = _silu(_causal_conv(x_ref[bi, r0:r1, :], tail, taps))

    def front(bi):
        pieces = [conv_rows[(bi, c)] for c in range(n_chunks)]
        y = jnp.concatenate(pieces, axis=0) if n_chunks > 1 else pieces[0]
        gb = gb_ref[bi]
        if rows > tb:
            y = jnp.concatenate([y, jnp.zeros((rows - tb, GDN_CONV_DIM), F32)], axis=0)
            gb = jnp.concatenate([gb, jnp.zeros((rows - tb, LANES), F32)], axis=0)
        q, k, v = y[:, :GDN_QK], y[:, GDN_QK:2 * GDN_QK], y[:, 2 * GDN_QK:]
        qn = q * lax.rsqrt(_split_dot(q * q, gsum, 2) + L2_EPS) * (GDN_DK ** -0.5)
        kn = k * lax.rsqrt(_split_dot(k * k, gsum, 2) + L2_EPS)
        ge = _split_dot(gb, eg_ref[...], 3)
        be = _split_dot(gb, eb_ref[...], 3)
        for c in range(n_chunks):
            rs = slice(c * CHUNK, (c + 1) * CHUNK)
            for p in range(N_GROUPS):
                ls = slice(p * GROUP_W, (p + 1) * GROUP_W)
                units[bi].append(dict(bi=bi, c=c, p=p, q=qn[rs, ls], k=kn[rs, ls], v=v[rs, ls], g=ge[rs, ls],
                                      b=be[rs, ls]))

    state = {}
    outs = {}

    def rec_read(bi, c):
        for u in units[bi]:
            if u['c'] == c:
                s = state.setdefault((bi, u['p']), s_scr[bi, u['p']]) if c == 0 else state[(bi, u['p'])]
                u['r'] = _dot(jnp.concatenate([u['kcum'], u['qd']], axis=0).astype(BF16), s.astype(BF16))

    def rec_update(bi, c):
        for u in units[bi]:
            if u['c'] == c:
                v_new = u['value'] - u['r'][:CHUNK]
                outs[(bi, c, u['p'])] = u['r'][CHUNK:] + _dot(u['attn'].astype(BF16), bd(v_new))
                upd = _dot_tn(u['kd'].astype(BF16), v_new.astype(BF16))
                key = (bi, u['p'])
                state[key] = state[key] * u['eglast'] + upd * bdm_f32

    def finish(bi):
        for p in range(N_GROUPS):
            s_scr[bi, p] = state[(bi, p)]
        o_rows = [jnp.concatenate([outs[(bi, c, p)] for p in range(N_GROUPS)], axis=1) for c in range(n_chunks)]
        o = jnp.concatenate(o_rows, axis=0) if n_chunks > 1 else o_rows[0]
        o = o[:tb]
        ms = _split_dot(o * o, gsum, 2) * (1.0 / GDN_DV)
        o_ref[bi] = (o * lax.rsqrt(ms + NORM_EPS) * nw_ref[...] * _silu(z_ref[bi])).astype(BF16)

    def bind(fn, *args):
        return lambda: fn(*args)

    fronts = [[bind(front, bi)] for bi in range(nb)]
    pres = [precompute(units[bi]) for bi in range(nb)]
    recs = [[f for c in range(n_chunks) for f in (bind(rec_read, bi, c), bind(rec_update, bi, c))]
            + [bind(finish, bi)] for bi in range(nb)]

    def interleave(*seqs):
        seqs = [s for s in seqs if s]
        n = max(len(s) for s in seqs)
        keyed = [((i + 0.5) * n / len(s), k, f) for k, s in enumerate(seqs) for i, f in enumerate(s)]
        return [f for _, _, f in sorted(keyed, key=lambda e: (e[0], e[1]))]

    program = []
    for bi in range(nb):
        program += [bind(conv_piece, bi, c) for c in range(n_chunks)] + fronts[bi] + pres[bi]
    program += interleave(*recs)
    for thunk in program:
        thunk()

    @pl.when(t == nt - 1)
    def _():
        for bi in range(nb):
            for p in range(N_GROUPS):
                sout_ref[bi, p] = _split_dot(s_scr[bi, p], dupt_ref[...], 3)


def _gdn_call(qkv, z, gb, cbuf8, s0, lw, consts):
    b, t, _ = qkv.shape
    tb = min(GDN_TB, t)
    nt = t // tb
    nb = GDN_NB
    blk = lambda w: pl.BlockSpec((nb, tb, w), lambda bi, ti: (bi, ti, 0))
    per_b = lambda shape: pl.BlockSpec((nb,) + shape, lambda bi, ti: (bi,) + (0,) * len(shape))
    cs = [lw['convw'], lw['gnw'], consts['gsum'], consts['eg'], consts['eb'], consts['bdm'], consts['dup'],
          consts['dupt']]
    state_spec = per_b((N_GROUPS, GROUP_W, GDN_DV))
    return pl.pallas_call(
        functools.partial(_gdn_kernel, nb=nb, tb=tb, nt=nt), grid=(b // nb, nt),
        in_specs=[blk(GDN_CONV_DIM), blk(GDN_VW), blk(LANES), per_b((8, GDN_CONV_DIM)), state_spec]
        + [_const_spec(c.shape) for c in cs],
        out_specs=(blk(GDN_VW), state_spec),
        out_shape=(jax.ShapeDtypeStruct((b, t, GDN_VW), BF16),
                   jax.ShapeDtypeStruct((b, N_GROUPS, GROUP_W, GDN_DV), F32)),
        scratch_shapes=[pltpu.VMEM((nb, N_GROUPS, GROUP_W, GROUP_W), F32),
                        pltpu.VMEM((nb, 8, GDN_CONV_DIM), F32)],
        compiler_params=_params(("parallel", "arbitrary")), name="gdn",
    )(qkv, z, gb, cbuf8, s0, *cs)


def _layer_weights(l, p):
    c1 = GDN_CONV_DIM
    c2 = c1 + GDN_VW
    c3 = c2 + GDN_HEADS
    c4 = c3 + GDN_HEADS
    c5 = c4 + Q_RANK
    c6 = c5 + KV_RANK
    w_in = p['w_in'][l]
    half = ROPE_DIM // 2
    kpe_w = w_in[:, c6:]
    kpe_sw = jnp.concatenate([kpe_w[:, half:], kpe_w[:, :half]], axis=1)
    zcol = lambda n: jnp.zeros((D_MODEL, n), F32)
    wsm = jnp.concatenate([kpe_w, w_in[:, c2:c3], w_in[:, c3:c4], zcol(LANES - B_LANE0 - GDN_HEADS),
                           kpe_sw, zcol(LANES - ROPE_DIM)], axis=1)
    wmain = jnp.concatenate([w_in[:, :c2], w_in[:, c4:c6]], axis=1)
    gvec = jnp.zeros((2, LANES), F32)
    gvec = gvec.at[0, G_LANE0:G_LANE0 + GDN_HEADS].set(p['gdn_a_log'][l])
    gvec = gvec.at[1, G_LANE0:G_LANE0 + GDN_HEADS].set(p['gdn_dt_bias'][l])
    wq = p['w_uq'][l].reshape(Q_RANK, MLA_HEADS, NOPE_DIM + ROPE_DIM)
    nope, rope = wq[..., :NOPE_DIM], wq[..., NOPE_DIM:]
    rope_sw = jnp.concatenate([rope[..., half:], rope[..., :half]], axis=-1)
    zq = lambda n: jnp.zeros((Q_RANK, MLA_HEADS, n), F32)
    tail = HEAD_PAD - NOPE_DIM - ROPE_DIM
    wq_a = jnp.concatenate([nope, rope, zq(tail)], axis=-1).reshape(Q_RANK, MLA_HEADS * HEAD_PAD)
    wq_b = jnp.concatenate([zq(NOPE_DIM), rope_sw, zq(tail)], axis=-1).reshape(Q_RANK, MLA_HEADS * HEAD_PAD)
    wkv = p['w_ukv'][l].reshape(KV_RANK, MLA_HEADS, NOPE_DIM + V_DIM)
    wk = jnp.concatenate([wkv[..., :NOPE_DIM], jnp.zeros((KV_RANK, MLA_HEADS, HEAD_PAD - NOPE_DIM), F32)],
                         axis=-1).reshape(KV_RANK, MLA_HEADS * HEAD_PAD)
    wv = wkv[..., NOPE_DIM:].reshape(KV_RANK, MLA_OUT)
    we = np.zeros((ROPE_DIM, MLA_HEADS, HEAD_PAD), np.float32)
    for r in range(ROPE_DIM):
        we[r, :, NOPE_DIM + r] = 1.0
    rope_pass = np.zeros((HEAD_PAD - NOPE_DIM, KV_RANK + ROPE_DIM), np.float32)
    for r in range(ROPE_DIM):
        rope_pass[r, KV_RANK + r] = 1.0
    wk_t = jnp.transpose(wkv[..., :NOPE_DIM], (1, 2, 0))
    wabs = jnp.concatenate([jnp.concatenate([wk_t, jnp.zeros((MLA_HEADS, NOPE_DIM, ROPE_DIM), F32)], axis=2),
                            jnp.broadcast_to(rope_pass, (MLA_HEADS,) + rope_pass.shape)], axis=1)
    bf = lambda a: a.astype(BF16)
    r2 = lambda a: a.reshape(1, -1)
    return dict(
        nf1=r2(p['norm_ffn1'][l]), wg1=bf(p['w_ffn1_gate'][l]), wu1=bf(p['w_ffn1_up'][l]),
        wd1=bf(p['w_ffn1_down'][l]), nmix=r2(p['norm_mix'][l]), wmain=bf(wmain), wsm=bf(wsm), gvec=gvec,
        qnorm=r2(p['mla_q_norm'][l]), kvnorm=r2(p['mla_kv_norm'][l]),
        wq2=bf(jnp.concatenate([wq_a, wq_b], axis=1)), wv=bf(wv), wvt=bf(wv.T), wabs=bf(wabs),
        wke=bf(jnp.concatenate([wk, jnp.asarray(we.reshape(ROPE_DIM, MLA_HEADS * HEAD_PAD))], axis=0)),
        wo=bf(p['w_out'][l]), nf2=r2(p['norm_ffn2'][l]), wg2=bf(p['w_ffn2_gate'][l]),
        wu2=bf(p['w_ffn2_up'][l]), wd2=bf(p['w_ffn2_down'][l]),
        convw=p['gdn_conv_w'][l], gnw=jnp.tile(p['gdn_norm_w'][l], GDN_HEADS).reshape(1, GDN_VW),
    )


def _gdn_consts():
    lane = np.arange(GDN_QK)
    gsum = (lane[:, None] // GDN_DK == lane[None, :] // GDN_DK).astype(np.float32)
    eg = np.zeros((LANES, GDN_QK), np.float32)
    eb = np.zeros((LANES, GDN_QK), np.float32)
    for h in range(GDN_HEADS):
        eg[G_LANE0 + h, h * GDN_DK:(h + 1) * GDN_DK] = 1.0
        eb[B_LANE0 + h, h * GDN_DK:(h + 1) * GDN_DK] = 1.0
    gl = np.arange(GROUP_W)
    bdm = (gl[:, None] // GDN_DK == gl[None, :] // GDN_DK).astype(np.float32)
    dup = np.tile(np.eye(GDN_DV, dtype=np.float32), (1, GROUP))
    return dict(gsum=jnp.asarray(gsum, BF16), eg=jnp.asarray(eg, BF16), eb=jnp.asarray(eb, BF16),
                bdm=jnp.asarray(bdm, BF16), dup=jnp.asarray(dup, BF16), dupt=jnp.asarray(dup.T, BF16))


def _rotary_table(pos, rows):
    half = ROPE_DIM // 2
    inv_freq = 1.0 / (ROPE_THETA ** (jnp.arange(half, dtype=F32) / half))
    ang = pos.astype(F32)[:, None] * inv_freq[None, :]
    cos, sin = jnp.cos(ang), jnp.sin(ang)
    t = pos.shape[0]
    cos2 = jnp.concatenate([cos, cos], axis=1)
    sin2 = jnp.concatenate([-sin, sin], axis=1)
    z = lambda n: jnp.zeros((t, n), F32)
    tail = HEAD_PAD - NOPE_DIM - ROPE_DIM
    tab = jnp.concatenate([jnp.ones((t, NOPE_DIM), F32), cos2, z(tail), z(NOPE_DIM), sin2, z(tail),
                           cos2, z(LANES - ROPE_DIM), sin2, z(LANES - ROPE_DIM)], axis=1)
    return jnp.tile(tab, (rows // t, 1)) if rows > t else tab


def _run_trunk(x, pos, past_ckv, past_kpe, s_gdn, s_conv, weights, consts):
    b, t, _ = x.shape
    n = b * t
    rot = _rotary_table(pos, max(t, min(TOK_TILE, n)))
    xf = x.reshape(n, D_MODEL)
    ckv_rows, kpe_rows, states, conv_bufs = [], [], [], []
    depth = len(weights)
    for l, lw in enumerate(weights):
        if s_gdn is None:
            cbuf8 = jnp.zeros((b, 8, GDN_CONV_DIM), F32)
            s0 = jnp.zeros((b, N_GROUPS, GROUP_W, GDN_DV), F32)
        else:
            cbuf8 = jnp.pad(s_conv[l], ((0, 0), (8 - (CONV_W - 1), 0), (0, 0)))
            s0 = s_gdn[l].reshape(b, N_GROUPS, GROUP_W, GDN_DV)
        xf, qkv, z, q, c_new, kpe_new, gb = _pre_call(xf, lw, rot, t)
        k_new, v_new = _kv_call(c_new, kpe_new, lw, v_transposed=past_ckv is None)
        q3 = q.reshape(b, t, -1)
        k3 = k_new.reshape(b, t, -1)
        if past_ckv is None:
            m_out = _attn_prompt_call(q3, k3, v_new)
        else:
            m_out = _attn_sample_call(q3, past_ckv, past_kpe, l, k3, v_new.reshape(b, t, -1), lw)
        qkv3 = qkv.reshape(b, t, GDN_CONV_DIM)
        g_out, s_new = _gdn_call(qkv3, z.reshape(b, t, GDN_VW), gb.reshape(b, t, LANES), cbuf8, s0, lw,
                                 consts)
        final = l == depth - 1
        xf = _post_call(xf, g_out.reshape(n, GDN_VW), m_out.reshape(n, MLA_OUT), lw,
                        weights[-1]['nfinal'], final)
        ckv_rows.append(c_new.reshape(b, t, KV_RANK))
        kpe_rows.append(kpe_new.reshape(b, t, ROPE_DIM))
        states.append(s_new.reshape(b, GDN_HEADS, GDN_DK, GDN_DV))
        conv_bufs.append(qkv3[:, t - (CONV_W - 1):, :])
    return (xf.reshape(b, t, D_MODEL), jnp.stack(ckv_rows), jnp.stack(kpe_rows), jnp.stack(states),
            jnp.stack(conv_bufs))


def kernel(x_prompt, x_sample, cache_mla_ckv, cache_mla_krope, state_gdn, state_gdn_conv,
           norm_ffn1, w_ffn1_gate, w_ffn1_up, w_ffn1_down, norm_mix, w_in, gdn_conv_w, gdn_a_log,
           gdn_dt_bias, gdn_norm_w, mla_q_norm, mla_kv_norm, w_uq, w_ukv, w_out, norm_ffn2,
           w_ffn2_gate, w_ffn2_up, w_ffn2_down, norm_final):
    params = dict(norm_ffn1=norm_ffn1, w_ffn1_gate=w_ffn1_gate, w_ffn1_up=w_ffn1_up, w_ffn1_down=w_ffn1_down,
                  norm_mix=norm_mix, w_in=w_in, gdn_conv_w=gdn_conv_w, gdn_a_log=gdn_a_log,
                  gdn_dt_bias=gdn_dt_bias, gdn_norm_w=gdn_norm_w, mla_q_norm=mla_q_norm,
                  mla_kv_norm=mla_kv_norm, w_uq=w_uq, w_ukv=w_ukv, w_out=w_out, norm_ffn2=norm_ffn2,
                  w_ffn2_gate=w_ffn2_gate, w_ffn2_up=w_ffn2_up, w_ffn2_down=w_ffn2_down)
    depth = w_in.shape[0]
    weights = [_layer_weights(l, params) for l in range(depth)]
    weights[-1]['nfinal'] = norm_final.reshape(1, D_MODEL)
    consts = _gdn_consts()
    t_p = x_prompt.shape[1]
    out_p = _run_trunk(x_prompt, jnp.arange(t_p, dtype=jnp.int32), None, None, None, None, weights, consts)
    pos_s = cache_mla_ckv.shape[2] + jnp.arange(x_sample.shape[1], dtype=jnp.int32)
    out_s = _run_trunk(x_sample, pos_s, cache_mla_ckv, cache_mla_krope, state_gdn, state_gdn_conv,
                       weights, consts)
    return (out_p[0], out_s[0]) + out_p[1:] + out_s[1:]
```
